```python
import math
import jax, jax.numpy as jnp
from jax import lax
import numpy as np

D_MODEL = 1024
BATCH = 8
SEQ = 2048
DEPTH = 2
DEC_BATCH = 32
DEC_SEQ = 8
PAST_LEN = 16384
PAGE_SIZE = 128

D_FF = ((8 * D_MODEL // 3 + 127) // 128) * 128
MLA_HEADS = 8
MLA_NOPE = 64
MLA_ROPE = 32
MLA_V = 64
Q_LORA = 384
KV_LORA = 256
MLA_SCALE = (MLA_NOPE + MLA_ROPE) ** -0.5
S5_WIDTH = D_MODEL // 2
S5_GROUP = 16
S5_GROUPS = S5_WIDTH // S5_GROUP
S5_STATE = 64
LOG_DT_MIN = math.log(1e-3)
LOG_DT_MAX = math.log(1e-1)
FOX_HEADS = 8
FOX_HD = D_MODEL // FOX_HEADS
FOX_SCALE = FOX_HD ** -0.5
FOX_F_BIAS = 3.0
Q_BLOCK = 128
ROPE_THETA = 10000.0
RMS_EPS = 1e-6
N_A_LAYERS = (DEPTH + 1) // 2
N_C_LAYERS = DEPTH // 2
A_IN = Q_LORA + KV_LORA + MLA_ROPE + S5_WIDTH
A_MIX = MLA_HEADS * MLA_V + S5_WIDTH
C_IN = 3 * FOX_HEADS * FOX_HD + FOX_HEADS
C_MIX = FOX_HEADS * FOX_HD

kernel_name = 'hybrid_mla_s5_fox_macaron_adaln_step'


def rms_norm(x, g):
    x32 = x.astype(jnp.float32)
    y = x32 * lax.rsqrt(jnp.mean(x32 * x32, axis=-1, keepdims=True) + RMS_EPS)
    return (y * g.astype(jnp.float32)).astype(x.dtype)


def rope(x, pos):
    half = x.shape[-1] // 2
    freq = ROPE_THETA ** (-jnp.arange(half, dtype=jnp.float32) / half)
    ang = pos.astype(jnp.float32)[:, None] * freq[None, :]
    shape = (1, ang.shape[0]) + (1,) * (x.ndim - 3) + (half,)
    cos = jnp.cos(ang).reshape(shape)
    sin = jnp.sin(ang).reshape(shape)
    x32 = x.astype(jnp.float32)
    x1, x2 = x32[..., :half], x32[..., half:]
    return jnp.concatenate([x1 * cos - x2 * sin, x2 * cos + x1 * sin], axis=-1).astype(x.dtype)


def modulate(x, g, shift, scale):
    return rms_norm(x, g) * (1 + scale[:, None, :]) + shift[:, None, :]


def swiglu(h, w1, w3, w2):
    return (jax.nn.silu(h @ w1) * (h @ w3)) @ w2


def ada_modulation(c, w_ada, b_ada):
    m = jax.nn.silu(c) @ w_ada + b_ada
    return m.reshape(c.shape[0], 3, 3, D_MODEL)


def ffn_sublayer(x, m, g, w1, w3, w2):
    h = modulate(x, g, m[:, 0], m[:, 1])
    return x + 0.5 * m[:, 2][:, None, :] * swiglu(h, w1, w3, w2)


def mla_project(h, pos, w_in, q_norm, kv_norm, w_uq, qn_norm, qr_norm, kr_norm):
    B, L = h.shape[:2]
    proj = h @ w_in
    o1 = Q_LORA
    o2 = o1 + KV_LORA
    o3 = o2 + MLA_ROPE
    cq, ckv_raw, kpe_raw, u = proj[..., :o1], proj[..., o1:o2], proj[..., o2:o3], proj[..., o3:]
    q = (rms_norm(cq, q_norm) @ w_uq).reshape(B, L, MLA_HEADS, MLA_NOPE + MLA_ROPE)
    q_nope = rms_norm(q[..., :MLA_NOPE], qn_norm)
    q_pe = rope(rms_norm(q[..., MLA_NOPE:], qr_norm), pos)
    ckv = rms_norm(ckv_raw, kv_norm)
    kpe = rope(rms_norm(kpe_raw, kr_norm), pos)
    return q_nope, q_pe, ckv, kpe, u


def mla_expand(ckv, w_ukv, kn_norm):
    kv = (ckv @ w_ukv).reshape(ckv.shape[:-1] + (MLA_HEADS, MLA_NOPE + MLA_V))
    return rms_norm(kv[..., :MLA_NOPE], kn_norm), kv[..., MLA_NOPE:]


def mla_attend_prompt(q_nope, q_pe, ckv, kpe, w_ukv, kn_norm):
    B, L = q_nope.shape[:2]
    nb = L // Q_BLOCK
    k_nope, v = mla_expand(ckv, w_ukv, kn_norm)
    key_pos = jnp.arange(L)

    def to_blocks(t):
        return jnp.moveaxis(t.reshape((B, nb, Q_BLOCK) + t.shape[2:]), 1, 0)

    def block(args):
        qn, qp, b = args
        s = (jnp.einsum('bqhd,bkhd->bhqk', qn, k_nope)
             + jnp.einsum('bqhr,bkr->bhqk', qp, kpe)).astype(jnp.float32) * MLA_SCALE
        q_pos = b * Q_BLOCK + jnp.arange(Q_BLOCK)
        s = jnp.where(key_pos[None, :] <= q_pos[:, None], s, -jnp.inf)
        p = jax.nn.softmax(s, axis=-1).astype(v.dtype)
        return jnp.einsum('bhqk,bkhd->bqhd', p, v)

    o = lax.map(block, (to_blocks(q_nope), to_blocks(q_pe), jnp.arange(nb)))
    return jnp.moveaxis(o, 0, 1).reshape(B, L, MLA_HEADS * MLA_V)


def mla_attend_sample(q_nope, q_pe, ckv_new, kpe_new, pool_ckv, pool_kpe, layer, page_table, w_ukv, kn_norm):
    DB, DS = q_nope.shape[:2]
    n_past = page_table.shape[1] * PAGE_SIZE
    causal_new = jnp.arange(DS)[None, :] <= jnp.arange(DS)[:, None]
    mask = jnp.concatenate([jnp.ones((DS, n_past), bool), causal_new], axis=1)

    def one(args):
        qn, qp, cn, kn, pages = args
        ckv = jnp.concatenate([pool_ckv[layer, pages].reshape(n_past, KV_LORA), cn], axis=0)
        kpe = jnp.concatenate([pool_kpe[layer, pages].reshape(n_past, MLA_ROPE), kn], axis=0)
        k_nope, v = mla_expand(ckv, w_ukv, kn_norm)
        s = (jnp.einsum('qhd,khd->hqk', qn, k_nope)
             + jnp.einsum('qhr,kr->hqk', qp, kpe)).astype(jnp.float32) * MLA_SCALE
        s = jnp.where(mask[None], s, -jnp.inf)
        p = jax.nn.softmax(s, axis=-1).astype(v.dtype)
        return jnp.einsum('hqk,khd->qhd', p, v)

    o = lax.map(one, (q_nope, q_pe, ckv_new, kpe_new, page_table))
    return o.reshape(DB, DS, MLA_HEADS * MLA_V)


def _complex_affine_combine(e1, e2):
    a1r, a1i, b1r, b1i = e1
    a2r, a2i, b2r, b2i = e2
    return (a2r * a1r - a2i * a1i,
            a2r * a1i + a2i * a1r,
            a2r * b1r - a2i * b1i + b2r,
            a2r * b1i + a2i * b1r + b2i)


def s5_mix(u, h0_re, h0_im, a_re, a_im, log_dt, b_re, b_im, c_re, c_im, d_skip, w_glu, b_glu):
    f32 = jnp.float32
    B, L = u.shape[:2]
    ug = u.astype(f32).reshape(B, L, S5_GROUPS, S5_GROUP)
    dt = jnp.exp(log_dt.astype(f32))[:, None]
    ar, ai = a_re.astype(f32), a_im.astype(f32)
    mag = jnp.exp(dt * ar)
    abar_re, abar_im = mag * jnp.cos(dt * ai), mag * jnp.sin(dt * ai)
    den = ar * ar + ai * ai
    w_re = ((abar_re - 1) * ar + abar_im * ai) / den
    w_im = (abar_im * ar - (abar_re - 1) * ai) / den
    br, bi = b_re.astype(f32), b_im.astype(f32)
    bbar_re = w_re[..., None] * br - w_im[..., None] * bi
    bbar_im = w_re[..., None] * bi + w_im[..., None] * br
    bu_re = jnp.einsum('blgc,gnc->blgn', ug, bbar_re)
    bu_im = jnp.einsum('blgc,gnc->blgn', ug, bbar_im)
    shape = bu_re.shape
    elems = (jnp.broadcast_to(abar_re, shape), jnp.broadcast_to(abar_im, shape), bu_re, bu_im)
    pr, pi, xr, xi = lax.associative_scan(_complex_affine_combine, elems, axis=1)
    h0r = h0_re.astype(f32)[:, None]
    h0i = h0_im.astype(f32)[:, None]
    xr, xi = xr + pr * h0r - pi * h0i, xi + pr * h0i + pi * h0r
    y = (jnp.einsum('blgn,gcn->blgc', xr, c_re.astype(f32))
         - jnp.einsum('blgn,gcn->blgc', xi, c_im.astype(f32))).reshape(B, L, S5_WIDTH)
    y = jax.nn.gelu(y + d_skip.astype(f32) * u.astype(f32))
    y = y * jax.nn.sigmoid(y @ w_glu.astype(f32) + b_glu.astype(f32))
    return y.astype(u.dtype), xr[:, -1], xi[:, -1]


def fox_project(h, w_in, b_f, q_norm, k_norm):
    B, L = h.shape[:2]
    hd = FOX_HEADS * FOX_HD
    proj = h @ w_in
    q = rms_norm(proj[..., :hd].reshape(B, L, FOX_HEADS, FOX_HD), q_norm)
    k = rms_norm(proj[..., hd:2 * hd].reshape(B, L, FOX_HEADS, FOX_HD), k_norm)
    v = proj[..., 2 * hd:3 * hd].reshape(B, L, FOX_HEADS, FOX_HD)
    logf = jax.nn.log_sigmoid((proj[..., 3 * hd:] + b_f).astype(jnp.float32))
    return q, k, v, logf


def fox_attend_prompt(q, k, v, logf):
    B, L = q.shape[:2]
    nb = L // Q_BLOCK
    F = jnp.cumsum(logf, axis=1)
    F_k = jnp.swapaxes(F, 1, 2)
    key_pos = jnp.arange(L)

    def to_blocks(t):
        return jnp.moveaxis(t.reshape((B, nb, Q_BLOCK) + t.shape[2:]), 1, 0)

    def block(args):
        qb, fb, b = args
        s = jnp.einsum('bqhd,bkhd->bhqk', qb, k).astype(jnp.float32) * FOX_SCALE
        s = s + jnp.swapaxes(fb, 1, 2)[..., :, None] - F_k[:, :, None, :]
        q_pos = b * Q_BLOCK + jnp.arange(Q_BLOCK)
        s = jnp.where(key_pos[None, :] <= q_pos[:, None], s, -jnp.inf)
        p = jax.nn.softmax(s, axis=-1).astype(v.dtype)
        return jnp.einsum('bhqk,bkhd->bqhd', p, v)

    o = lax.map(block, (to_blocks(q), to_blocks(F), jnp.arange(nb)))
    return jnp.moveaxis(o, 0, 1).reshape(B, L, FOX_HEADS * FOX_HD)


def fox_attend_sample(q, k, v, logf, pool_k, pool_v, pool_logf, layer, page_table):
    DB, DS = q.shape[:2]
    n_past = page_table.shape[1] * PAGE_SIZE
    causal_new = jnp.arange(DS)[None, :] <= jnp.arange(DS)[:, None]

    def one(args):
        qs, ks, vs, lf, pages = args
        pk = pool_k[layer, pages].reshape(n_past, FOX_HEADS, FOX_HD)
        pv = pool_v[layer, pages].reshape(n_past, FOX_HEADS, FOX_HD)
        plf = pool_logf[layer, pages].reshape(n_past, FOX_HEADS).astype(jnp.float32)
        f_new = jnp.cumsum(lf, axis=0)
        r_past = lax.cumsum(plf, axis=0, reverse=True) - plf
        s_past = (jnp.einsum('qhd,khd->hqk', qs, pk).astype(jnp.float32) * FOX_SCALE
                  + r_past.T[:, None, :] + f_new.T[:, :, None])
        s_new = (jnp.einsum('qhd,khd->hqk', qs, ks).astype(jnp.float32) * FOX_SCALE
                 + f_new.T[:, :, None] - f_new.T[:, None, :])
        s_new = jnp.where(causal_new[None], s_new, -jnp.inf)
        p = jax.nn.softmax(jnp.concatenate([s_past, s_new], axis=-1), axis=-1).astype(vs.dtype)
        return (jnp.einsum('hqk,khd->qhd', p[..., :n_past], pv)
                + jnp.einsum('hqk,khd->qhd', p[..., n_past:], vs))

    o = lax.map(one, (q, k, v, logf, page_table))
    return o.reshape(DB, DS, FOX_HEADS * FOX_HD)


def setup_inputs(seed: int = 0) -> dict:
    key = jax.random.key(seed)
    ks = iter(jax.random.split(key, 64))
    f32 = jnp.float32
    D = D_MODEL

    def nrm(shape, scale):
        return jax.random.normal(next(ks), shape, f32) * scale

    def gain(shape):
        return 1.0 + nrm(shape, 0.02)

    n_pages = PAST_LEN // PAGE_SIZE
    n_used = DEC_BATCH * n_pages
    n_pool = n_used + max(1, n_used // 4)
    page_table = jax.random.permutation(next(ks), n_pool)[:n_used].reshape(DEC_BATCH, n_pages).astype(jnp.int32)
    return {
        'x_prompt': nrm((BATCH, SEQ, D), 1.0),
        'x_sample': nrm((DEC_BATCH, DEC_SEQ, D), 1.0),
        'c_prompt': nrm((BATCH, D), 1.0),
        'c_sample': nrm((DEC_BATCH, D), 1.0),
        'cache_mla_ckv': nrm((N_A_LAYERS, n_pool, PAGE_SIZE, KV_LORA), 1.0),
        'cache_mla_kpe': nrm((N_A_LAYERS, n_pool, PAGE_SIZE, MLA_ROPE), 1.0),
        'state_s5_re': nrm((N_A_LAYERS, DEC_BATCH, S5_GROUPS, S5_STATE), 0.3),
        'state_s5_im': nrm((N_A_LAYERS, DEC_BATCH, S5_GROUPS, S5_STATE), 0.3),
        'cache_fox_k': nrm((N_C_LAYERS, n_pool, PAGE_SIZE, FOX_HEADS, FOX_HD), 1.0),
        'cache_fox_v': nrm((N_C_LAYERS, n_pool, PAGE_SIZE, FOX_HEADS, FOX_HD), 1.0),
        'cache_fox_logf': jax.nn.log_sigmoid(FOX_F_BIAS + nrm((N_C_LAYERS, n_pool, PAGE_SIZE, FOX_HEADS), 1.0)),
        'page_table': page_table,
        'w_ada': nrm((DEPTH, D, 9 * D), 0.5 * D ** -0.5),
        'b_ada': nrm((DEPTH, 9 * D), 0.02),
        'norm_g': gain((DEPTH, 3, D)),
        'ffn_w1': nrm((DEPTH, 2, D, D_FF), D ** -0.5),
        'ffn_w3': nrm((DEPTH, 2, D, D_FF), D ** -0.5),
        'ffn_w2': nrm((DEPTH, 2, D_FF, D), D_FF ** -0.5),
        'a_w_in': nrm((N_A_LAYERS, D, A_IN), D ** -0.5),
        'a_q_norm': gain((N_A_LAYERS, Q_LORA)),
        'a_kv_norm': gain((N_A_LAYERS, KV_LORA)),
        'a_w_uq': nrm((N_A_LAYERS, Q_LORA, MLA_HEADS * (MLA_NOPE + MLA_ROPE)), Q_LORA ** -0.5),
        'a_w_ukv': nrm((N_A_LAYERS, KV_LORA, MLA_HEADS * (MLA_NOPE + MLA_V)), KV_LORA ** -0.5),
        'a_qn_norm': gain((N_A_LAYERS, MLA_NOPE)),
        'a_qr_norm': gain((N_A_LAYERS, MLA_ROPE)),
        'a_kn_norm': gain((N_A_LAYERS, MLA_NOPE)),
        'a_kr_norm': gain((N_A_LAYERS, MLA_ROPE)),
        's5_a_re': -0.5 + nrm((N_A_LAYERS, S5_GROUPS, S5_STATE), 0.01),
        's5_a_im': jnp.pi * jnp.arange(S5_STATE, dtype=f32) + nrm((N_A_LAYERS, S5_GROUPS, S5_STATE), 0.01),
        's5_log_dt': jax.random.uniform(next(ks), (N_A_LAYERS, S5_GROUPS), f32, LOG_DT_MIN, LOG_DT_MAX),
        's5_b_re': nrm((N_A_LAYERS, S5_GROUPS, S5_STATE, S5_GROUP), (2 * S5_GROUP) ** -0.5),
        's5_b_im': nrm((N_A_LAYERS, S5_GROUPS, S5_STATE, S5_GROUP), (2 * S5_GROUP) ** -0.5),
        's5_c_re': nrm((N_A_LAYERS, S5_GROUPS, S5_GROUP, S5_STATE), S5_STATE ** -0.5),
        's5_c_im': nrm((N_A_LAYERS, S5_GROUPS, S5_GROUP, S5_STATE), S5_STATE ** -0.5),
        's5_d': nrm((N_A_LAYERS, S5_WIDTH), 0.5),
        's5_w_glu': nrm((N_A_LAYERS, S5_WIDTH, S5_WIDTH), S5_WIDTH ** -0.5),
        's5_b_glu': nrm((N_A_LAYERS, S5_WIDTH), 0.02),
        'a_w_out': nrm((N_A_LAYERS, A_MIX, D), A_MIX ** -0.5),
        'c_w_in': nrm((N_C_LAYERS, D, C_IN), D ** -0.5),
        'c_b_f': FOX_F_BIAS + nrm((N_C_LAYERS, FOX_HEADS), 0.5),
        'c_q_norm': gain((N_C_LAYERS, FOX_HD)),
        'c_k_norm': gain((N_C_LAYERS, FOX_HD)),
        'c_w_out': nrm((N_C_LAYERS, C_MIX, D), C_MIX ** -0.5),
    }


def reference(x_prompt, x_sample, c_prompt, c_sample, cache_mla_ckv, cache_mla_kpe, state_s5_re, state_s5_im,
              cache_fox_k, cache_fox_v, cache_fox_logf, page_table, w_ada, b_ada, norm_g, ffn_w1, ffn_w3, ffn_w2,
              a_w_in, a_q_norm, a_kv_norm, a_w_uq, a_w_ukv, a_qn_norm, a_qr_norm, a_kn_norm, a_kr_norm,
              s5_a_re, s5_a_im, s5_log_dt, s5_b_re, s5_b_im, s5_c_re, s5_c_im, s5_d, s5_w_glu, s5_b_glu,
              a_w_out, c_w_in, c_b_f, c_q_norm, c_k_norm, c_w_out):
    f32 = jnp.float32
    n_past = page_table.shape[1] * PAGE_SIZE
    pos_p = jnp.arange(x_prompt.shape[1], dtype=f32)
    pos_s = n_past + jnp.arange(x_sample.shape[1], dtype=f32)
    xp, xs = x_prompt, x_sample
    ckv_p, kpe_p, s5r_p, s5i_p, fk_p, fv_p, flf_p = [], [], [], [], [], [], []
    ckv_s, kpe_s, s5r_s, s5i_s, fk_s, fv_s, flf_s = [], [], [], [], [], [], []
    for i in range(DEPTH):
        mp = ada_modulation(c_prompt, w_ada[i], b_ada[i])
        ms = ada_modulation(c_sample, w_ada[i], b_ada[i])
        ffn_pre = (ffn_w1[i, 0], ffn_w3[i, 0], ffn_w2[i, 0])
        xp = ffn_sublayer(xp, mp[:, 0], norm_g[i, 0], *ffn_pre)
        xs = ffn_sublayer(xs, ms[:, 0], norm_g[i, 0], *ffn_pre)
        hp = modulate(xp, norm_g[i, 1], mp[:, 1, 0], mp[:, 1, 1])
        hs = modulate(xs, norm_g[i, 1], ms[:, 1, 0], ms[:, 1, 1])
        j = i // 2
        if i % 2 == 0:
            proj = (a_w_in[j], a_q_norm[j], a_kv_norm[j], a_w_uq[j], a_qn_norm[j], a_qr_norm[j], a_kr_norm[j])
            s5p = (s5_a_re[j], s5_a_im[j], s5_log_dt[j], s5_b_re[j], s5_b_im[j], s5_c_re[j], s5_c_im[j],
                   s5_d[j], s5_w_glu[j], s5_b_glu[j])
            qn, qr, ckv, kpe, u = mla_project(hp, pos_p, *proj)
            att = mla_attend_prompt(qn, qr, ckv, kpe, a_w_ukv[j], a_kn_norm[j])
            h0 = jnp.zeros((hp.shape[0], S5_GROUPS, S5_STATE), f32)
            ssm, sr, si = s5_mix(u, h0, h0, *s5p)
            mix_p = jnp.concatenate([att, ssm], axis=-1) @ a_w_out[j]
            ckv_p.append(ckv)
            kpe_p.append(kpe)
            s5r_p.append(sr)
            s5i_p.append(si)
            qn, qr, ckv, kpe, u = mla_project(hs, pos_s, *proj)
            att = mla_attend_sample(qn, qr, ckv, kpe, cache_mla_ckv, cache_mla_kpe, j, page_table,
                                    a_w_ukv[j], a_kn_norm[j])
            ssm, sr, si = s5_mix(u, state_s5_re[j], state_s5_im[j], *s5p)
            mix_s = jnp.concatenate([att, ssm], axis=-1) @ a_w_out[j]
            ckv_s.append(ckv)
            kpe_s.append(kpe)
            s5r_s.append(sr)
            s5i_s.append(si)
        else:
            fox = (c_w_in[j], c_b_f[j], c_q_norm[j], c_k_norm[j])
            q, k, v, lf = fox_project(hp, *fox)
            mix_p = fox_attend_prompt(q, k, v, lf) @ c_w_out[j]
            fk_p.append(k)
            fv_p.append(v)
            flf_p.append(lf)
            q, k, v, lf = fox_project(hs, *fox)
            mix_s = fox_attend_sample(q, k, v, lf, cache_fox_k, cache_fox_v, cache_fox_logf, j, page_table) @ c_w_out[j]
            fk_s.append(k)
            fv_s.append(v)
            flf_s.append(lf)
        xp = xp + mp[:, 1, 2][:, None, :] * mix_p
        xs = xs + ms[:, 1, 2][:, None, :] * mix_s
        ffn_post = (ffn_w1[i, 1], ffn_w3[i, 1], ffn_w2[i, 1])
        xp = ffn_sublayer(xp, mp[:, 2], norm_g[i, 2], *ffn_post)
        xs = ffn_sublayer(xs, ms[:, 2], norm_g[i, 2], *ffn_post)
    return (xp, xs,
            jnp.stack(ckv_p), jnp.stack(kpe_p), jnp.stack(s5r_p), jnp.stack(s5i_p),
            jnp.stack(fk_p), jnp.stack(fv_p), jnp.stack(flf_p),
            jnp.stack(ckv_s), jnp.stack(kpe_s), jnp.stack(s5r_s), jnp.stack(s5i_s),
            jnp.stack(fk_s), jnp.stack(fv_s), jnp.stack(flf_s))
```

```python
import functools
import math

import jax
import jax.numpy as jnp
from jax import lax
from jax.experimental import pallas as pl
from jax.experimental.pallas import tpu as pltpu

F32 = jnp.float32
BF16 = jnp.bfloat16

LANE = 128
VMEM_LIMIT_BYTES = 52 * 1024 * 1024

RMS_EPS = 1e-6
ROPE_THETA = 10000.0
NEG_BIG = -1e30

MLA_HEADS = 8
MLA_NOPE = 64
MLA_ROPE = 32
MLA_V = 64
MLA_SCALE = (MLA_NOPE + MLA_ROPE) ** -0.5
Q_LORA = 384
KV_LORA = 256
S5_GROUP = 16
S5_STATE = 64
FOX_HEADS = 8
FOX_HD = 128
FOX_SCALE = FOX_HD ** -0.5
PAGE = 128

NT_DIMS = (((1,), (1,)), ((), ()))


def _params(*sem):
    return pltpu.CompilerParams(dimension_semantics=sem, vmem_limit_bytes=VMEM_LIMIT_BYTES)


def _dot(a, b):
    return jnp.dot(a, b, preferred_element_type=F32)


def _dot_nt(a, b):
    return lax.dot_general(a, b, NT_DIMS, preferred_element_type=F32)


def _split3(x):
    hi = x.astype(BF16)
    r1 = x - hi.astype(F32)
    mid = r1.astype(BF16)
    lo = (r1 - mid.astype(F32)).astype(BF16)
    return hi, mid, lo


def _rms(x, n):
    return lax.rsqrt(jnp.sum(x * x, axis=-1, keepdims=True) * (1.0 / n) + RMS_EPS)


def _modulate(x, g, scale, shift):
    return (x * _rms(x, x.shape[-1]) * g) * (1.0 + scale) + shift


def _silu(x):
    return x * jax.nn.sigmoid(x)


def _ada_kernel(c_ref, w_ref, b_ref, o_ref):
    a = _silu(c_ref[...]).astype(BF16)
    o_ref[...] = _dot(a, w_ref[...].astype(BF16)) + b_ref[...]


def ada_modulation(c_all, w_ada, b_ada):
    depth, d, n = w_ada.shape
    m = c_all.shape[0]
    tn = 1024
    return pl.pallas_call(
        _ada_kernel,
        grid=(depth, n // tn),
        in_specs=[
            pl.BlockSpec((m, d), lambda i, j: (0, 0)),
            pl.BlockSpec((None, d, tn), lambda i, j: (i, 0, j)),
            pl.BlockSpec((None, 1, tn), lambda i, j: (i, 0, j)),
        ],
        out_specs=pl.BlockSpec((None, m, tn), lambda i, j: (i, 0, j)),
        out_shape=jax.ShapeDtypeStruct((depth, m, n), F32),
        compiler_params=_params("parallel", "parallel"),
        name="ada",
    )(c_all, w_ada, b_ada.reshape(depth, 1, n))


def _row_spec(arr, tm):
    return pl.BlockSpec((None, tm, arr.shape[-1]), lambda b, l: (b, l, 0))


def _mod_spec(arr, tm):
    if arr.shape[1] == 1:
        return pl.BlockSpec((None, 1, arr.shape[-1]), lambda b, l: (b, 0, 0))
    return pl.BlockSpec((None, tm, arr.shape[-1]), lambda b, l: (b, l, 0))


def _full_spec(arr):
    nd = arr.ndim
    return pl.BlockSpec(arr.shape, lambda *_: (0,) * nd)


def _ffn_kernel(x_ref, g_ref, sh_ref, sc_ref, gt_ref, w1_ref, w3_ref, w2_ref, o_ref, h_scr, acc_scr):
    j = pl.program_id(2)

    @pl.when(j == 0)
    def _():
        h_scr[...] = _modulate(x_ref[...], g_ref[...], sc_ref[...], sh_ref[...]).astype(BF16)
        acc_scr[...] = jnp.zeros_like(acc_scr)

    h = h_scr[...]
    a = _dot(h, w1_ref[...])
    b = _dot(h, w3_ref[...])
    t = (_silu(a) * b).astype(BF16)
    acc_scr[...] += _dot(t, w2_ref[...])

    @pl.when(j == pl.num_programs(2) - 1)
    def _():
        o_ref[...] = x_ref[...] + 0.5 * gt_ref[...] * acc_scr[...]


def ffn_sublayer(x, g, shift, scale, gate, w1, w3, w2, li, half, tm, tf):
    nb, r, d = x.shape
    f = w1.shape[-1]
    wspec13 = pl.BlockSpec((None, None, d, tf), lambda b, l, j: (li, half, 0, j))
    wspec2 = pl.BlockSpec((None, None, tf, d), lambda b, l, j: (li, half, j, 0))

    def lift(spec):
        return pl.BlockSpec(spec.block_shape, lambda b, l, j, _im=spec.index_map: _im(b, l))

    return pl.pallas_call(
        _ffn_kernel,
        grid=(nb, r // tm, f // tf),
        in_specs=[lift(_row_spec(x, tm)), pl.BlockSpec((1, d), lambda b, l, j: (0, 0)),
                  lift(_mod_spec(shift, tm)), lift(_mod_spec(scale, tm)), lift(_mod_spec(gate, tm)),
                  wspec13, wspec13, wspec2],
        out_specs=lift(_row_spec(x, tm)),
        out_shape=jax.ShapeDtypeStruct(x.shape, F32),
        scratch_shapes=[pltpu.VMEM((tm, d), BF16), pltpu.VMEM((tm, d), F32)],
        compiler_params=_params("parallel", "parallel", "arbitrary"),
        name="ffn",
    )(x, g.reshape(1, d), shift, scale, gate, w1, w3, w2)


def _aproj_kernel(x_ref, g_ref, sh_ref, sc_ref, gaq_ref, gbq_ref, gak_ref, gbk_ref, gq_ref, gkv_ref, gkn_ref,
                  win_ref, wq_ref, wkv_ref, ckv_ref, kpe_ref, u_ref, qf_ref, kf_ref, v_ref):
    h = _modulate(x_ref[...], g_ref[...], sc_ref[...], sh_ref[...]).astype(BF16)
    proj = _dot(h, win_ref[...])
    o1 = Q_LORA
    o2 = o1 + KV_LORA
    o3 = o2 + 512
    cq = proj[:, :o1]
    cqn = (cq * _rms(cq, Q_LORA) * gq_ref[...]).astype(BF16)
    ckv_raw = proj[:, o1:o2]
    ckv = ckv_raw * _rms(ckv_raw, KV_LORA) * gkv_ref[...]
    ckv_ref[...] = ckv
    u_ref[...] = proj[:, o2:o3]
    ka = proj[:, o3:o3 + LANE]
    kb = proj[:, o3 + LANE:o3 + 2 * LANE]
    kpe_blk = _rms(ka, MLA_ROPE) * (ka * gak_ref[...] + kb * gbk_ref[...])
    kpe_ref[...] = kpe_blk[:, MLA_NOPE:MLA_NOPE + MLA_ROPE]

    qraw = _dot(cqn, wq_ref[...])
    kv = _dot(ckv.astype(BF16), wkv_ref[...])
    lane = lax.broadcasted_iota(jnp.int32, (1, LANE), 1)
    is_n = lane < MLA_NOPE
    is_p = jnp.logical_and(lane >= MLA_NOPE, lane < MLA_NOPE + MLA_ROPE)
    gaq = gaq_ref[...]
    gbq = gbq_ref[...]
    gkn = gkn_ref[...]
    hw = MLA_HEADS * LANE
    for hh in range(MLA_HEADS):
        sl = slice(hh * LANE, (hh + 1) * LANE)
        a = qraw[:, sl]
        b = qraw[:, hw + hh * LANE:hw + (hh + 1) * LANE]
        sq = a * a
        msn = jnp.sum(jnp.where(is_n, sq, 0.0), axis=-1, keepdims=True) * (1.0 / MLA_NOPE)
        msp = jnp.sum(jnp.where(is_p, sq, 0.0), axis=-1, keepdims=True) * (1.0 / MLA_ROPE)
        r = jnp.where(is_n, lax.rsqrt(msn + RMS_EPS), lax.rsqrt(msp + RMS_EPS))
        qf_ref[:, sl] = (r * (a * gaq + b * gbq)).astype(BF16)
        kk = kv[:, sl]
        kf_ref[:, sl] = (kk * _rms(kk, MLA_NOPE) * gkn + kpe_blk).astype(BF16)
    v_ref[...] = kv[:, hw:].astype(BF16)


def mla_s5_project(x, g, shift, scale, tabs, gq, gkv, gkn, win, wq, wkv, tm):
    nb, r, d = x.shape
    gaq, gbq, gak, gbk = tabs
    tab_spec = pl.BlockSpec((tm, LANE), lambda b, l: (l, 0))
    hw = MLA_HEADS * LANE
    hv = MLA_HEADS * MLA_V
    u_w = win.shape[1] - Q_LORA - KV_LORA - 2 * LANE
    out_shape = [
        jax.ShapeDtypeStruct((nb, r, KV_LORA), F32),
        jax.ShapeDtypeStruct((nb, r, MLA_ROPE), F32),
        jax.ShapeDtypeStruct((r, nb * u_w), F32),
        jax.ShapeDtypeStruct((nb, r, hw), BF16),
        jax.ShapeDtypeStruct((nb, r, hw), BF16),
        jax.ShapeDtypeStruct((nb, r, hv), BF16),
    ]

    def rs(c):
        return pl.BlockSpec((None, tm, c), lambda b, l: (b, l, 0))

    out_specs = [rs(KV_LORA), rs(MLA_ROPE), pl.BlockSpec((tm, u_w), lambda b, l: (l, b)), rs(hw), rs(hw), rs(hv)]
    return pl.pallas_call(
        _aproj_kernel,
        grid=(nb, r // tm),
        in_specs=[_row_spec(x, tm), _full_spec(g), _mod_spec(shift, tm), _mod_spec(scale, tm),
                  tab_spec, tab_spec, tab_spec, tab_spec, _full_spec(gq), _full_spec(gkv), _full_spec(gkn),
                  _full_spec(win), _full_spec(wq), _full_spec(wkv)],
        out_specs=out_specs,
        out_shape=out_shape,
        compiler_params=_params("parallel", "parallel"),
        name="mla_s5_project",
    )(x, g, shift, scale, gaq, gbq, gak, gbk, gq, gkv, gkn, win, wq, wkv)


def _flash_kernel(qi_tab, kj_tab, *refs, hp, dv, has_bias, tq, tk):
    if has_bias:
        q_ref, k_ref, v_ref, fq_ref, fk_ref, o_ref, m_scr, l_scr, acc_scr = refs
    else:
        q_ref, k_ref, v_ref, o_ref, m_scr, l_scr, acc_scr = refs
    p_idx = pl.program_id(2)
    qi = qi_tab[p_idx]
    kj = kj_tab[p_idx]

    @pl.when(kj == 0)
    def _():
        m_scr[...] = jnp.full_like(m_scr, NEG_BIG)
        l_scr[...] = jnp.zeros_like(l_scr)
        acc_scr[...] = jnp.zeros_like(acc_scr)

    def update(masked):
        for i in range(hp):
            q = q_ref[:, i * LANE:(i + 1) * LANE]
            k = k_ref[:, i * LANE:(i + 1) * LANE]
            s = _dot_nt(q, k)
            if has_bias:
                s = s + fq_ref[...] - fk_ref[...]
            if masked:
                row = qi * tq + lax.broadcasted_iota(jnp.int32, (tq, tk), 0)
                col = kj * tk + lax.broadcasted_iota(jnp.int32, (tq, tk), 1)
                s = jnp.where(col <= row, s, NEG_BIG)
            m_prev = m_scr[i]
            m_new = jnp.maximum(m_prev, jnp.max(s, axis=-1, keepdims=True))
            alpha = jnp.exp(m_prev - m_new)
            p = jnp.exp(s - m_new)
            l_scr[i] = alpha * l_scr[i] + jnp.sum(p, axis=-1, keepdims=True)
            acc_scr[i] = alpha * acc_scr[i] + _dot(p.astype(BF16), v_ref[...])
            m_scr[i] = m_new

    on_diag = (kj + 1) * tk > qi * tq

    @pl.when(on_diag)
    def _():
        update(True)

    @pl.when(jnp.logical_not(on_diag))
    def _():
        update(False)

    @pl.when((kj + 1) * tk == (qi + 1) * tq)
    def _():
        out = acc_scr[0] / l_scr[0]
        if hp > 1:
            lane = lax.broadcasted_iota(jnp.int32, (1, hp * dv), 1)
            for i in range(1, hp):
                out = jnp.where(lane >= i * dv, acc_scr[i] / l_scr[i], out)
        o_ref[...] = out.astype(o_ref.dtype)


def flash_prompt(q, k, v, fq, fk, hp, dv, tq, tk):
    b, l, hw = q.shape
    nh = hw // LANE
    assert hp * dv == LANE and tq % tk == 0
    pairs = [(i, j) for i in range(l // tq) for j in range((i + 1) * tq // tk)]
    qi_tab = jnp.asarray([p[0] for p in pairs], jnp.int32)
    kj_tab = jnp.asarray([p[1] for p in pairs], jnp.int32)
    has_bias = fq is not None
    in_specs = [
        pl.BlockSpec((None, tq, hp * LANE), lambda bb, h, p, qt, kt: (bb, qt[p], h)),
        pl.BlockSpec((None, tk, hp * LANE), lambda bb, h, p, qt, kt: (bb, kt[p], h)),
        pl.BlockSpec((None, tk, hp * dv), lambda bb, h, p, qt, kt: (bb, kt[p], h)),
    ]
    args = [q, k, v]
    if has_bias:
        in_specs += [
            pl.BlockSpec((None, None, tq, 1), lambda bb, h, p, qt, kt: (bb, h, qt[p], 0)),
            pl.BlockSpec((None, None, 1, tk), lambda bb, h, p, qt, kt: (bb, h, 0, kt[p])),
        ]
        args += [fq, fk]
    grid_spec = pltpu.PrefetchScalarGridSpec(
        num_scalar_prefetch=2,
        grid=(b, nh // hp, len(pairs)),
        in_specs=in_specs,
        out_specs=pl.BlockSpec((None, tq, hp * dv), lambda bb, h, p, qt, kt: (bb, qt[p], h)),
        scratch_shapes=[pltpu.VMEM((hp, tq, 1), F32), pltpu.VMEM((hp, tq, 1), F32),
                        pltpu.VMEM((hp, tq, LANE), F32)],
    )
    return pl.pallas_call(
        functools.partial(_flash_kernel, hp=hp, dv=dv, has_bias=has_bias, tq=tq, tk=tk),
        grid_spec=grid_spec,
        out_shape=jax.ShapeDtypeStruct((b, l, nh * dv), BF16),
        compiler_params=_params("parallel", "parallel", "arbitrary"),
        name="flash_bias" if has_bias else "flash",
    )(qi_tab, kj_tab, *args)


def _gelu_tanh(x):
    c = math.sqrt(2.0 / math.pi)
    return x * (0.5 * (1.0 + jnp.tanh(c * (x + 0.044715 * (x * x * x)))))


def _s5_kernel(u_ref, h0r_ref, h0i_ref, ar_ref, ai_ref, bd_ref, bdlo_ref, cd_ref, d_ref, wg_ref, bg_ref,
               y_ref, xr_ref, xi_ref, sr, si, xr_scr, xi_scr, *, t_blk, bt, precise):
    i = pl.program_id(0)
    rows = t_blk * bt
    ns = xr_scr.shape[-1]
    nh = ns // 2
    ch = u_ref.shape[-1] // 2

    @pl.when(i == 0)
    def _():
        xr_scr[...] = h0r_ref[...]
        xi_scr[...] = h0i_ref[...]

    u = u_ref[...].reshape(rows, 2 * ch)
    ub = u.astype(BF16)
    for hf in range(2):
        uh = ub[:, hf * ch:(hf + 1) * ch]
        bu = _dot(uh, bd_ref[hf])
        if precise:
            ulo = (u[:, hf * ch:(hf + 1) * ch] - uh.astype(F32)).astype(BF16)
            bu = bu + _dot(ulo, bd_ref[hf]) + _dot(uh, bdlo_ref[hf])
        sr[:, hf * nh:(hf + 1) * nh] = bu[:, :nh]
        si[:, hf * nh:(hf + 1) * nh] = bu[:, nh:]

    for hf in range(2):
        cs = slice(hf * nh, (hf + 1) * nh)
        ar = jnp.broadcast_to(ar_ref[:, cs], (bt, nh))
        ai = jnp.broadcast_to(ai_ref[:, cs], (bt, nh))

        def step(t, carry, cs=cs, ar=ar, ai=ai):
            xr, xi = carry
            o = pl.multiple_of(t * bt, bt)
            nr = ar * xr - ai * xi + sr[pl.ds(o, bt), cs]
            ni = ar * xi + ai * xr + si[pl.ds(o, bt), cs]
            sr[pl.ds(o, bt), cs] = nr
            si[pl.ds(o, bt), cs] = ni
            return nr, ni

        xr, xi = lax.fori_loop(0, t_blk, step, (xr_scr[:, cs], xi_scr[:, cs]))
        xr_scr[:, cs] = xr
        xi_scr[:, cs] = xi

    ys = []
    for hf in range(2):
        cs = slice(hf * nh, (hf + 1) * nh)
        ys.append(_dot(sr[:, cs].astype(BF16), cd_ref[hf, :nh, :]) + _dot(si[:, cs].astype(BF16), cd_ref[hf, nh:, :]))
    y = jnp.concatenate(ys, axis=1)
    y = _gelu_tanh(y + d_ref[...] * u)
    y = y * jax.nn.sigmoid(_dot(y.astype(BF16), wg_ref[...]) + bg_ref[...])
    y_ref[...] = y.reshape(t_blk, bt, 2 * ch)

    @pl.when(i == pl.num_programs(0) - 1)
    def _():
        xr_ref[...] = xr_scr[...]
        xi_ref[...] = xi_scr[...]


def s5_mix(u3, h0r, h0i, prm, t_blk, precise):
    t, bt, c = u3.shape
    ns = h0r.shape[-1]
    rows = t_blk * bt
    ar, ai, bd, bdlo, cd, d, wg, bg = prm
    blk = pl.BlockSpec((t_blk, bt, c), lambda i: (i, 0, 0))
    ins = [u3, h0r, h0i, ar, ai, bd, bdlo, cd, d, wg, bg]
    return pl.pallas_call(
        functools.partial(_s5_kernel, t_blk=t_blk, bt=bt, precise=precise),
        grid=(t // t_blk,),
        in_specs=[blk] + [_full_spec(a) for a in ins[1:]],
        out_specs=[blk, _full_spec(h0r), _full_spec(h0i)],
        out_shape=[jax.ShapeDtypeStruct((t, bt, c), F32), jax.ShapeDtypeStruct(h0r.shape, F32),
                   jax.ShapeDtypeStruct(h0i.shape, F32)],
        scratch_shapes=[pltpu.VMEM((rows, ns), F32), pltpu.VMEM((rows, ns), F32),
                        pltpu.VMEM((bt, ns), F32), pltpu.VMEM((bt, ns), F32)],
        compiler_params=_params("arbitrary"),
        name="s5",
    )(*ins)


def _outproj_kernel(x_ref, gt_ref, *rest, n_in):
    a_refs = rest[:n_in]
    w_refs = rest[n_in:2 * n_in]
    o_ref = rest[2 * n_in]
    acc = _dot(a_refs[0][...].astype(BF16), w_refs[0][...])
    for a_ref, w_ref in zip(a_refs[1:], w_refs[1:]):
        acc = acc + _dot(a_ref[...].astype(BF16), w_ref[...])
    o_ref[...] = x_ref[...] + gt_ref[...] * acc


def mixer_out(x, gate, acts, act_specs, ws, tm):
    nb, r, d = x.shape
    return pl.pallas_call(
        functools.partial(_outproj_kernel, n_in=len(acts)),
        grid=(nb, r // tm),
        in_specs=[_row_spec(x, tm), _mod_spec(gate, tm)] + list(act_specs) + [_full_spec(w) for w in ws],
        out_specs=_row_spec(x, tm),
        out_shape=jax.ShapeDtypeStruct(x.shape, F32),
        compiler_params=_params("parallel", "parallel"),
        name="mixer_out",
    )(x, gate, *acts, *ws)


def _foxproj_kernel(x_ref, g_ref, sh_ref, sc_ref, gq_ref, gk_ref, bf_ref, w_ref,
                    k_ref, v_ref, lf_ref, qb_ref, kb_ref, vb_ref, fc_ref, carry_scr, *, seg, tm):
    l_idx = pl.program_id(1)
    h = _modulate(x_ref[...], g_ref[...], sc_ref[...], sh_ref[...]).astype(BF16)
    proj = _dot(h, w_ref[...])
    hw = FOX_HEADS * FOX_HD
    gq = gq_ref[...]
    gk = gk_ref[...]
    for hh in range(FOX_HEADS):
        sl = slice(hh * FOX_HD, (hh + 1) * FOX_HD)
        qh = proj[:, sl]
        qb_ref[:, sl] = (qh * _rms(qh, FOX_HD) * gq).astype(BF16)
        kh = proj[:, hw + hh * FOX_HD:hw + (hh + 1) * FOX_HD]
        kn = kh * _rms(kh, FOX_HD) * gk
        k_ref[:, sl] = kn
        kb_ref[:, sl] = kn.astype(BF16)
    vv = proj[:, 2 * hw:3 * hw]
    v_ref[...] = vv
    vb_ref[...] = vv.astype(BF16)
    z = proj[:, 3 * hw:] + bf_ref[...]
    lf = jnp.minimum(z, 0.0) - jnp.log1p(jnp.exp(-jnp.abs(z)))
    lf_ref[...] = lf[:, :FOX_HEADS]
    row = lax.broadcasted_iota(jnp.int32, (tm, tm), 0)
    col = lax.broadcasted_iota(jnp.int32, (tm, tm), 1)
    keep = col <= row
    if seg < tm:
        keep = jnp.logical_and(keep, (col // seg) == (row // seg))
    tri = jnp.where(keep, 1.0, 0.0).astype(BF16)
    hi, mid, lo = _split3(lf)
    cs = _dot(tri, hi) + _dot(tri, mid) + _dot(tri, lo)
    if seg > tm:
        @pl.when(l_idx == 0)
        def _():
            carry_scr[...] = jnp.zeros_like(carry_scr)

        cs = cs + carry_scr[...]
        carry_scr[...] = cs[tm - 1:tm, :]
    fc_ref[...] = cs


def fox_project(x, g, shift, scale, gq, gk, bf, w, seg, tm):
    nb, r, d = x.shape
    hw = FOX_HEADS * FOX_HD

    def rs(c):
        return pl.BlockSpec((None, tm, c), lambda b, l: (b, l, 0))

    out_shape = [
        jax.ShapeDtypeStruct((nb, r, hw), F32), jax.ShapeDtypeStruct((nb, r, hw), F32),
        jax.ShapeDtypeStruct((nb, r, FOX_HEADS), F32),
        jax.ShapeDtypeStruct((nb, r, hw), BF16), jax.ShapeDtypeStruct((nb, r, hw), BF16),
        jax.ShapeDtypeStruct((nb, r, hw), BF16), jax.ShapeDtypeStruct((nb, r, LANE), F32),
    ]
    out_specs = [rs(hw), rs(hw), rs(FOX_HEADS), rs(hw), rs(hw), rs(hw), rs(LANE)]
    return pl.pallas_call(
        functools.partial(_foxproj_kernel, seg=seg, tm=tm),
        grid=(nb, r // tm),
        in_specs=[_row_spec(x, tm), _full_spec(g), _mod_spec(shift, tm), _mod_spec(scale, tm),
                  _full_spec(gq), _full_spec(gk), _full_spec(bf), _full_spec(w)],
        out_specs=out_specs,
        out_shape=out_shape,
        scratch_shapes=[pltpu.VMEM((1, LANE), F32)],
        compiler_params=_params("parallel", "arbitrary"),
        name="fox_project",
    )(x, g, shift, scale, gq, gk, bf, w)


def _mla_sample_kernel(pt_ref, q_ref, cn_ref, kn_ref, wfull_ref, wukt_ref, wuv_ref, *rest, pps):
    ckv_refs = rest[:pps]
    kpe_refs = rest[pps:2 * pps]
    o_ref, qabs_scr, qpe_scr, ckv_scr, m_scr, l_scr, acc_scr = rest[2 * pps:]
    s_idx = pl.program_id(1)
    nq = q_ref.shape[0]
    nrow = MLA_HEADS * nq

    @pl.when(s_idx == 0)
    def _():
        m_scr[...] = jnp.full_like(m_scr, NEG_BIG)
        l_scr[...] = jnp.zeros_like(l_scr)
        acc_scr[...] = jnp.zeros_like(acc_scr)
        for hh in range(MLA_HEADS):
            res = _dot(q_ref[:, hh * LANE:(hh + 1) * LANE], wfull_ref[hh])
            qabs_scr[hh * nq:(hh + 1) * nq, :] = res[:, :KV_LORA]
            qpe_scr[hh * nq:(hh + 1) * nq, :] = res[:, KV_LORA:]

    wstack = jnp.concatenate([wukt_ref[...], qabs_scr[...].astype(BF16)], axis=0)
    qpe = qpe_scr[...].astype(BF16)
    nk = MLA_HEADS * MLA_NOPE

    def scores(cb, kpe_blk):
        a = _dot_nt(wstack, cb)
        spe = _dot_nt(qpe[:, :MLA_ROPE], kpe_blk)
        rows = []
        for hh in range(MLA_HEADS):
            kr = a[hh * MLA_NOPE:(hh + 1) * MLA_NOPE, :]
            ms = jnp.sum(kr * kr, axis=0, keepdims=True) * (1.0 / MLA_NOPE)
            rows.append(a[nk + hh * nq:nk + (hh + 1) * nq, :] * lax.rsqrt(ms + RMS_EPS))
        return jnp.concatenate(rows, axis=0) + spe

    def online(s, vals):
        m_prev = m_scr[...]
        m_new = jnp.maximum(m_prev, jnp.max(s, axis=-1, keepdims=True))
        alpha = jnp.exp(m_prev - m_new)
        p = jnp.exp(s - m_new)
        l_scr[...] = alpha * l_scr[...] + jnp.sum(p, axis=-1, keepdims=True)
        acc_scr[...] = alpha * acc_scr[...] + _dot(p.astype(BF16), vals)
        m_scr[...] = m_new

    parts = []
    for i in range(pps):
        cb = ckv_refs[i][...].astype(BF16)
        ckv_scr[i * PAGE:(i + 1) * PAGE, :] = cb
        parts.append(scores(cb, kpe_refs[i][...].astype(BF16)))
    online(jnp.concatenate(parts, axis=1), ckv_scr[...])

    @pl.when(s_idx == pl.num_programs(1) - 1)
    def _():
        pad = PAGE - nq
        cb = jnp.concatenate([cn_ref[...], jnp.zeros((pad, KV_LORA), F32)], axis=0).astype(BF16)
        kb = jnp.concatenate([kn_ref[...], jnp.zeros((pad, MLA_ROPE), F32)], axis=0).astype(BF16)
        s = scores(cb, kb)
        row = lax.broadcasted_iota(jnp.int32, (nrow, PAGE), 0)
        col = lax.broadcasted_iota(jnp.int32, (nrow, PAGE), 1)
        s = jnp.where(col <= row % nq, s, NEG_BIG)
        online(s, cb)
        o_lat = (acc_scr[...] / l_scr[...]).astype(BF16)
        out = _dot(o_lat[0:nq, :], wuv_ref[0])
        for hh in range(1, MLA_HEADS):
            out = out + _dot(o_lat[hh * nq:(hh + 1) * nq, :], wuv_ref[hh])
        o_ref[...] = out


def mla_attend_sample(qf, ckv_new, kpe_new, pool_ckv, pool_kpe, layer, page_table, wfull, wukt, wuv, pps):
    db, nq, hw = qf.shape
    n_pages = page_table.shape[1]
    assert n_pages % pps == 0
    nrow = MLA_HEADS * nq

    def fixed(arr):
        nd = arr.ndim
        return pl.BlockSpec(arr.shape, lambda b, s, pt: (0,) * nd)

    def per_seq(arr):
        return pl.BlockSpec((None,) + arr.shape[1:], lambda b, s, pt: (b, 0, 0))

    def page_spec(arr, i):
        return pl.BlockSpec((None, None) + arr.shape[2:], lambda b, s, pt: (layer, pt[b, s * pps + i], 0, 0))

    in_specs = [per_seq(qf), per_seq(ckv_new), per_seq(kpe_new), fixed(wfull), fixed(wukt), fixed(wuv)]
    in_specs += [page_spec(pool_ckv, i) for i in range(pps)] + [page_spec(pool_kpe, i) for i in range(pps)]
    out_w = MLA_HEADS * MLA_V
    grid_spec = pltpu.PrefetchScalarGridSpec(
        num_scalar_prefetch=1,
        grid=(db, n_pages // pps),
        in_specs=in_specs,
        out_specs=pl.BlockSpec((None, nq, out_w), lambda b, s, pt: (b, 0, 0)),
        scratch_shapes=[pltpu.VMEM((nrow, KV_LORA), F32), pltpu.VMEM((nrow, LANE), F32),
                        pltpu.VMEM((pps * PAGE, KV_LORA), BF16),
                        pltpu.VMEM((nrow, 1), F32), pltpu.VMEM((nrow, 1), F32), pltpu.VMEM((nrow, KV_LORA), F32)],
    )
    return pl.pallas_call(
        functools.partial(_mla_sample_kernel, pps=pps),
        grid_spec=grid_spec,
        out_shape=jax.ShapeDtypeStruct((db, nq, out_w), F32),
        compiler_params=_params("parallel", "arbitrary"),
        name="mla_sample",
    )(page_table, qf, ckv_new, kpe_new, wfull, wukt, wuv, *([pool_ckv] * pps), *([pool_kpe] * pps))


def _fox_sample_kernel(pt_ref, q_ref, kn_ref, vn_ref, fq_ref, fkt_ref, *rest, pps):
    k_refs = rest[:pps]
    v_refs = rest[pps:2 * pps]
    lf_refs = rest[2 * pps:3 * pps]
    o_ref, qbd_scr, v_scr, carry_scr, m_scr, l_scr, acc_scr = rest[3 * pps:]
    s_idx = pl.program_id(1)
    nq = q_ref.shape[0]
    nrow = nq * FOX_HEADS
    hw = FOX_HEADS * FOX_HD

    @pl.when(s_idx == 0)
    def _():
        m_scr[...] = jnp.full_like(m_scr, NEG_BIG)
        l_scr[...] = jnp.zeros_like(l_scr)
        acc_scr[...] = jnp.zeros_like(acc_scr)
        carry_scr[...] = jnp.zeros_like(carry_scr)
        q32 = q_ref[...].astype(F32)
        head_of_lane = lax.broadcasted_iota(jnp.int32, (FOX_HEADS, hw), 1) // FOX_HD
        head_of_row = lax.broadcasted_iota(jnp.int32, (FOX_HEADS, hw), 0)
        for qq in range(nq):
            rowv = jnp.broadcast_to(q32[qq:qq + 1, :], (FOX_HEADS, hw))
            qbd_scr[qq * FOX_HEADS:(qq + 1) * FOX_HEADS, :] = jnp.where(head_of_lane == head_of_row, rowv, 0.0)

    qbd = qbd_scr[...].astype(BF16)
    fq = fq_ref[...]

    def online(s, vals):
        m_prev = m_scr[...]
        m_new = jnp.maximum(m_prev, jnp.max(s, axis=-1, keepdims=True))
        alpha = jnp.exp(m_prev - m_new)
        p = jnp.exp(s - m_new)
        l_scr[...] = alpha * l_scr[...] + jnp.sum(p, axis=-1, keepdims=True)
        acc_scr[...] = alpha * acc_scr[...] + _dot(p.astype(BF16), vals)
        m_scr[...] = m_new

    later = lax.broadcasted_iota(jnp.int32, (PAGE, PAGE), 0) > lax.broadcasted_iota(jnp.int32, (PAGE, PAGE), 1)
    upper = jnp.where(later, 1.0, 0.0).astype(BF16)
    carry = carry_scr[...]
    parts = []
    for i in range(pps):
        lft = lf_refs[i][...]
        hi, mid, lo = _split3(lft)
        r_t = _dot(hi, upper) + _dot(mid, upper) + _dot(lo, upper) + carry
        carry = carry + jnp.sum(lft, axis=-1, keepdims=True)
        kb = k_refs[i][...].astype(BF16)
        v_scr[i * PAGE:(i + 1) * PAGE, :] = v_refs[i][...].astype(BF16)
        s = _dot_nt(qbd, kb) + jnp.concatenate([r_t] * nq, axis=0) + fq
        parts.append(s)
    carry_scr[...] = carry
    online(jnp.concatenate(parts, axis=1), v_scr[...])

    @pl.when(s_idx == pl.num_programs(1) - 1)
    def _():
        pad = PAGE - nq
        kb = jnp.concatenate([kn_ref[...], jnp.zeros((pad, hw), F32)], axis=0).astype(BF16)
        vb = jnp.concatenate([vn_ref[...], jnp.zeros((pad, hw), F32)], axis=0).astype(BF16)
        s = _dot_nt(qbd, kb) + fq - jnp.concatenate([fkt_ref[...]] * nq, axis=0)
        row = lax.broadcasted_iota(jnp.int32, (nrow, PAGE), 0)
        col = lax.broadcasted_iota(jnp.int32, (nrow, PAGE), 1)
        s = jnp.where(col <= row // FOX_HEADS, s, NEG_BIG)
        online(s, vb)
        acc = acc_scr[...]
        head_of_row = lax.broadcasted_iota(jnp.int32, (nrow, FOX_HD), 0) % FOX_HEADS
        out = jnp.zeros((nrow, FOX_HD), F32)
        for hh in range(FOX_HEADS):
            out = jnp.where(head_of_row == hh, acc[:, hh * FOX_HD:(hh + 1) * FOX_HD], out)
        o_ref[...] = out / l_scr[...]


def fox_attend_sample(qb, k_new, v_new, fq_col, fk_t, pool_k, pool_v, pool_lft, page_table, pps):
    db, nq, hw = qb.shape
    n_pages = page_table.shape[1]
    assert n_pages % pps == 0
    nrow = nq * FOX_HEADS

    def per_seq(arr):
        return pl.BlockSpec((None,) + arr.shape[1:], lambda b, s, pt: (b, 0, 0))

    def page_spec(arr, i):
        return pl.BlockSpec((None,) + arr.shape[1:],
                            lambda b, s, pt: (pt[b, n_pages - 1 - (s * pps + i)], 0, 0))

    in_specs = [per_seq(qb), per_seq(k_new), per_seq(v_new), per_seq(fq_col), per_seq(fk_t)]
    for pool in (pool_k, pool_v, pool_lft):
        in_specs += [page_spec(pool, i) for i in range(pps)]
    grid_spec = pltpu.PrefetchScalarGridSpec(
        num_scalar_prefetch=1,
        grid=(db, n_pages // pps),
        in_specs=in_specs,
        out_specs=pl.BlockSpec((None, nrow, FOX_HD), lambda b, s, pt: (b, 0, 0)),
        scratch_shapes=[pltpu.VMEM((nrow, hw), F32), pltpu.VMEM((pps * PAGE, hw), BF16),
                        pltpu.VMEM((FOX_HEADS, LANE), F32),
                        pltpu.VMEM((nrow, 1), F32), pltpu.VMEM((nrow, 1), F32), pltpu.VMEM((nrow, hw), F32)],
    )
    return pl.pallas_call(
        functools.partial(_fox_sample_kernel, pps=pps),
        grid_spec=grid_spec,
        out_shape=jax.ShapeDtypeStruct((db, nrow, FOX_HD), F32),
        compiler_params=_params("parallel", "arbitrary"),
        name="fox_sample",
    )(page_table, qb, k_new, v_new, fq_col, fk_t, *([pool_k] * pps), *([pool_v] * pps), *([pool_lft] * pps))


def _rope_perm():
    half = MLA_ROPE // 2
    idx = jnp.arange(MLA_ROPE)
    return jnp.where(idx < half, idx + half, idx - half), jnp.where(idx < half, -1.0, 1.0).astype(F32)


def _pad_rope_block(w):
    z = jnp.zeros(w.shape[:-1] + (MLA_NOPE,), w.dtype)
    z2 = jnp.zeros(w.shape[:-1] + (LANE - MLA_NOPE - MLA_ROPE,), w.dtype)
    return jnp.concatenate([z, w, z2], axis=-1)


def _mla_weights(a_w_in, a_w_uq, a_w_ukv):
    perm, _ = _rope_perm()
    o1 = Q_LORA
    o2 = o1 + KV_LORA
    o3 = o2 + MLA_ROPE
    w_kpe = a_w_in[:, o2:o3]
    win = jnp.concatenate([a_w_in[:, :o2], a_w_in[:, o3:], _pad_rope_block(w_kpe), _pad_rope_block(w_kpe[:, perm])],
                          axis=1).astype(BF16)
    wq3 = a_w_uq.reshape(Q_LORA, MLA_HEADS, MLA_NOPE + MLA_ROPE)
    zpad = jnp.zeros((Q_LORA, MLA_HEADS, LANE - MLA_NOPE - MLA_ROPE), F32)
    wq_a = jnp.concatenate([wq3, zpad], axis=-1).reshape(Q_LORA, MLA_HEADS * LANE)
    wq_b = _pad_rope_block(wq3[..., MLA_NOPE:][..., perm]).reshape(Q_LORA, MLA_HEADS * LANE)
    wq = jnp.concatenate([wq_a, wq_b], axis=1).astype(BF16)
    wkv3 = a_w_ukv.reshape(KV_LORA, MLA_HEADS, MLA_NOPE + MLA_V)
    wk = jnp.concatenate([wkv3[..., :MLA_NOPE], jnp.zeros((KV_LORA, MLA_HEADS, LANE - MLA_NOPE), F32)], axis=-1)
    wv = wkv3[..., MLA_NOPE:]
    wkv = jnp.concatenate([wk.reshape(KV_LORA, -1), wv.reshape(KV_LORA, -1)], axis=1).astype(BF16)
    return win, wq, wkv, wkv3


def _rope_tables(pos, qn_g, qr_g, kr_g):
    half = MLA_ROPE // 2
    perm, sign = _rope_perm()
    freq = ROPE_THETA ** (-jnp.arange(half, dtype=F32) / half)
    ang = pos.astype(F32)[:, None] * freq[None, :]
    cos = jnp.concatenate([jnp.cos(ang), jnp.cos(ang)], axis=1)
    sin = jnp.concatenate([jnp.sin(ang), jnp.sin(ang)], axis=1)
    n = pos.shape[0]
    zn = jnp.zeros((n, MLA_NOPE), F32)
    zp = jnp.zeros((n, LANE - MLA_NOPE - MLA_ROPE), F32)
    gaq = jnp.concatenate([jnp.broadcast_to(qn_g[None, :], (n, MLA_NOPE)), qr_g[None, :] * cos, zp], axis=1) * MLA_SCALE
    gbq = jnp.concatenate([zn, (sign * qr_g[perm])[None, :] * sin, zp], axis=1) * MLA_SCALE
    gak = jnp.concatenate([zn, kr_g[None, :] * cos, zp], axis=1)
    gbk = jnp.concatenate([zn, (sign * kr_g[perm])[None, :] * sin, zp], axis=1)
    return gaq, gbq, gak, gbk


def _s5_params(a_re, a_im, log_dt, b_re, b_im, c_re, c_im, d_skip, w_glu, b_glu):
    g, n = a_re.shape
    dt = jnp.exp(log_dt)[:, None]
    mag = jnp.exp(dt * a_re)
    abar_re, abar_im = mag * jnp.cos(dt * a_im), mag * jnp.sin(dt * a_im)
    den = a_re * a_re + a_im * a_im
    w_re = ((abar_re - 1) * a_re + abar_im * a_im) / den
    w_im = (abar_im * a_re - (abar_re - 1) * a_im) / den
    bbar_re = w_re[..., None] * b_re - w_im[..., None] * b_im
    bbar_im = w_re[..., None] * b_im + w_im[..., None] * b_re
    gh = g // 2
    eye = jnp.eye(gh, dtype=F32)

    def pack_b(bb):
        return jnp.einsum('gnc,gh->gchn', bb, eye).reshape(gh * S5_GROUP, gh * n)

    def pack_c(cc):
        return jnp.einsum('gcn,gh->gnhc', cc, eye).reshape(gh * n, gh * S5_GROUP)

    bd = jnp.stack([jnp.concatenate([pack_b(bbar_re[h * gh:(h + 1) * gh]), pack_b(bbar_im[h * gh:(h + 1) * gh])], axis=1)
                    for h in range(2)])
    cd = jnp.stack([jnp.concatenate([pack_c(c_re[h * gh:(h + 1) * gh]), -pack_c(c_im[h * gh:(h + 1) * gh])], axis=0)
                    for h in range(2)])
    bd_hi = bd.astype(BF16)
    bd_lo = (bd - bd_hi.astype(F32)).astype(BF16)
    return (abar_re.reshape(1, g * n), abar_im.reshape(1, g * n), bd_hi, bd_lo, cd.astype(BF16),
            d_skip.reshape(1, -1), w_glu.astype(BF16), b_glu.reshape(1, -1))


def kernel(x_prompt, x_sample, c_prompt, c_sample, cache_mla_ckv, cache_mla_kpe, state_s5_re, state_s5_im,
           cache_fox_k, cache_fox_v, cache_fox_logf, page_table, w_ada, b_ada, norm_g, ffn_w1, ffn_w3, ffn_w2,
           a_w_in, a_q_norm, a_kv_norm, a_w_uq, a_w_ukv, a_qn_norm, a_qr_norm, a_kn_norm, a_kr_norm,
           s5_a_re, s5_a_im, s5_log_dt, s5_b_re, s5_b_im, s5_c_re, s5_c_im, s5_d, s5_w_glu, s5_b_glu,
           a_w_out, c_w_in, c_b_f, c_q_norm, c_k_norm, c_w_out):
    B, L, D = x_prompt.shape
    DB, DS, _ = x_sample.shape
    RS = DB * DS
    n_pages = page_table.shape[1]
    n_past = n_pages * PAGE
    depth = w_ada.shape[0]

    tm_p = min(512, L)
    tm_ffn = min(1024, L)
    tf = 256
    tq = min(512, L)

    m_all = ada_modulation(jnp.concatenate([c_prompt, c_sample], axis=0), w_ada, b_ada)
    w1b, w3b, w2b = ffn_w1.astype(BF16), ffn_w3.astype(BF16), ffn_w2.astype(BF16)

    xp = x_prompt
    xs = x_sample.reshape(1, RS, D)
    outs_p = {}
    outs_s = {}
    for i in range(depth):
        mp = m_all[i, :B].reshape(B, 3, 3, 1, D)
        ms = jnp.repeat(m_all[i, B:].reshape(DB, 3, 3, D), DS, axis=0).reshape(1, RS, 3, 3, D)

        def mod_p(s, k):
            return mp[:, s, k]

        def mod_s(s, k):
            return ms[:, :, s, k]

        g = norm_g[i]
        xp = ffn_sublayer(xp, g[0], mod_p(0, 0), mod_p(0, 1), mod_p(0, 2), w1b, w3b, w2b, i, 0, tm_ffn, tf)
        xs = ffn_sublayer(xs, g[0], mod_s(0, 0), mod_s(0, 1), mod_s(0, 2), w1b, w3b, w2b, i, 0, RS, tf)
        j = i // 2
        g1 = g[1].reshape(1, D)
        if i % 2 == 0:
            win, wq, wkv, wkv3 = _mla_weights(a_w_in[j], a_w_uq[j], a_w_ukv[j])
            gq = a_q_norm[j].reshape(1, -1)
            gkv = a_kv_norm[j].reshape(1, -1)
            gkn = jnp.concatenate([a_kn_norm[j], jnp.zeros((LANE - MLA_NOPE,), F32)]).reshape(1, LANE)
            s5p = _s5_params(s5_a_re[j], s5_a_im[j], s5_log_dt[j], s5_b_re[j], s5_b_im[j], s5_c_re[j], s5_c_im[j],
                             s5_d[j], s5_w_glu[j], s5_b_glu[j])
            wo = a_w_out[j].astype(BF16)
            hv = MLA_HEADS * MLA_V
            wo_att, wo_ssm = wo[:hv], wo[hv:]
            cw = s5_d.shape[-1]
            ns = s5_a_re.shape[1] * s5_a_re.shape[2]

            tabs = _rope_tables(jnp.arange(L), a_qn_norm[j], a_qr_norm[j], a_kr_norm[j])
            ckv, kpe, u_tm, qf, kf, vb = mla_s5_project(xp, g1, mod_p(1, 0), mod_p(1, 1), tabs, gq, gkv, gkn,
                                                        win, wq, wkv, tm_p)
            att = flash_prompt(qf, kf, vb, None, None, 2, MLA_V, tq, tq)
            zeros = jnp.zeros((B, ns), F32)
            ssm, sr, si = s5_mix(u_tm.reshape(L, B, cw), zeros, zeros, s5p, min(64, L), False)
            ssm2 = ssm.reshape(L, B * cw)
            xp = mixer_out(xp, mod_p(1, 2), [att, ssm2],
                           [pl.BlockSpec((None, tm_p, hv), lambda b, l: (b, l, 0)),
                            pl.BlockSpec((tm_p, cw), lambda b, l: (l, b))],
                           [wo_att, wo_ssm], tm_p)
            outs_p.setdefault('ckv', []).append(ckv)
            outs_p.setdefault('kpe', []).append(kpe)
            outs_p.setdefault('s5r', []).append(sr.reshape(B, -1, S5_STATE))
            outs_p.setdefault('s5i', []).append(si.reshape(B, -1, S5_STATE))

            pos_s = n_past + jnp.tile(jnp.arange(DS), DB)
            tabs = _rope_tables(pos_s, a_qn_norm[j], a_qr_norm[j], a_kr_norm[j])
            ckv, kpe, u_s, qf, kf, vb = mla_s5_project(xs, g1, mod_s(1, 0), mod_s(1, 1), tabs, gq, gkv, gkn,
                                                       win, wq, wkv, RS)
            wuk = wkv3[..., :MLA_NOPE] * a_kn_norm[j][None, None, :]
            wabs = jnp.transpose(wuk, (1, 2, 0))
            top = jnp.concatenate([wabs, jnp.zeros((MLA_HEADS, MLA_NOPE, LANE), F32)], axis=-1)
            eye_blk = jnp.concatenate([jnp.zeros((MLA_ROPE, KV_LORA), F32), jnp.eye(MLA_ROPE, LANE, dtype=F32)], axis=-1)
            mid = jnp.broadcast_to(eye_blk[None], (MLA_HEADS, MLA_ROPE, KV_LORA + LANE))
            bot = jnp.zeros((MLA_HEADS, LANE - MLA_NOPE - MLA_ROPE, KV_LORA + LANE), F32)
            wfull = jnp.concatenate([top, mid, bot], axis=1).astype(BF16)
            wukt = jnp.transpose(wkv3[..., :MLA_NOPE], (1, 2, 0)).reshape(MLA_HEADS * MLA_NOPE, KV_LORA).astype(BF16)
            wuv = jnp.einsum('khd,hg->hkgd', wkv3[..., MLA_NOPE:], jnp.eye(MLA_HEADS, dtype=F32))
            wuv = wuv.reshape(MLA_HEADS, KV_LORA, hv).astype(BF16)
            att_s = mla_attend_sample(qf.reshape(DB, DS, -1), ckv.reshape(DB, DS, -1), kpe.reshape(DB, DS, -1),
                                      cache_mla_ckv, cache_mla_kpe, j, page_table, wfull, wukt, wuv, 8)
            u3 = jnp.transpose(u_s.reshape(DB, DS, cw), (1, 0, 2))
            ssm, sr, si = s5_mix(u3, state_s5_re[j].reshape(DB, ns), state_s5_im[j].reshape(DB, ns), s5p, DS, True)
            ssm_s = jnp.transpose(ssm, (1, 0, 2)).reshape(1, RS, cw)
            xs = mixer_out(xs, mod_s(1, 2), [att_s.reshape(1, RS, hv), ssm_s],
                           [pl.BlockSpec((None, RS, hv), lambda b, l: (b, l, 0)),
                            pl.BlockSpec((None, RS, cw), lambda b, l: (b, l, 0))],
                           [wo_att, wo_ssm], RS)
            outs_s.setdefault('ckv', []).append(ckv.reshape(DB, DS, -1))
            outs_s.setdefault('kpe', []).append(kpe.reshape(DB, DS, -1))
            outs_s.setdefault('s5r', []).append(sr.reshape(DB, -1, S5_STATE))
            outs_s.setdefault('s5i', []).append(si.reshape(DB, -1, S5_STATE))
        else:
            hw = FOX_HEADS * FOX_HD
            wf = jnp.concatenate([c_w_in[j], jnp.zeros((D, LANE - FOX_HEADS), F32)], axis=1)
            wf = wf.astype(BF16)
            gqf = (c_q_norm[j] * FOX_SCALE).reshape(1, FOX_HD)
            gkf = c_k_norm[j].reshape(1, FOX_HD)
            bf = jnp.concatenate([c_b_f[j], jnp.zeros((LANE - FOX_HEADS,), F32)]).reshape(1, LANE)
            wo = c_w_out[j].astype(BF16)

            k32, v32, lf, qb, kb, vb, fc = fox_project(xp, g1, mod_p(1, 0), mod_p(1, 1), gqf, gkf, bf, wf, L, tm_p)
            fcs = fc[:, :, :FOX_HEADS]
            fq = jnp.transpose(fcs, (0, 2, 1))[..., None]
            fk = jnp.transpose(fcs, (0, 2, 1))[:, :, None, :]
            o = flash_prompt(qb, kb, vb, fq, fk, 1, FOX_HD, tq, tq)
            xp = mixer_out(xp, mod_p(1, 2), [o], [pl.BlockSpec((None, tm_p, hw), lambda b, l: (b, l, 0))], [wo], tm_p)
            outs_p.setdefault('fk', []).append(k32.reshape(B, L, FOX_HEADS, FOX_HD))
            outs_p.setdefault('fv', []).append(v32.reshape(B, L, FOX_HEADS, FOX_HD))
            outs_p.setdefault('flf', []).append(lf)

            k32, v32, lf, qb, kb, vb, fc = fox_project(xs, g1, mod_s(1, 0), mod_s(1, 1), gqf, gkf, bf, wf, DS, RS)
            f_new = fc[0, :, :FOX_HEADS].reshape(DB, DS, FOX_HEADS)
            fq_col = f_new.reshape(DB, DS * FOX_HEADS, 1)
            fk_t = jnp.transpose(f_new, (0, 2, 1))
            fk_t = jnp.concatenate([fk_t, jnp.zeros((DB, FOX_HEADS, LANE - DS), F32)], axis=-1)
            pool_k = cache_fox_k[j].reshape(-1, PAGE, hw)
            pool_v = cache_fox_v[j].reshape(-1, PAGE, hw)
            pool_lft = jnp.transpose(cache_fox_logf[j], (0, 2, 1))
            o = fox_attend_sample(qb.reshape(DB, DS, hw), k32.reshape(DB, DS, hw), v32.reshape(DB, DS, hw),
                                  fq_col, fk_t, pool_k, pool_v, pool_lft, page_table, 8)
            xs = mixer_out(xs, mod_s(1, 2), [o.reshape(1, RS, hw)],
                           [pl.BlockSpec((None, RS, hw), lambda b, l: (b, l, 0))], [wo], RS)
            outs_s.setdefault('fk', []).append(k32.reshape(DB, DS, FOX_HEADS, FOX_HD))
            outs_s.setdefault('fv', []).append(v32.reshape(DB, DS, FOX_HEADS, FOX_HD))
            outs_s.setdefault('flf', []).append(lf.reshape(DB, DS, FOX_HEADS))
        xp = ffn_sublayer(xp, g[2], mod_p(2, 0), mod_p(2, 1), mod_p(2, 2), w1b, w3b, w2b, i, 1, tm_ffn, tf)
        xs = ffn_sublayer(xs, g[2], mod_s(2, 0), mod_s(2, 1), mod_s(2, 2), w1b, w3b, w2b, i, 1, RS, tf)

    def st(d, key):
        return jnp.stack(d[key])

    return (xp, xs.reshape(DB, DS, D),
            st(outs_p, 'ckv'), st(outs_p, 'kpe'), st(outs_p, 's5r'), st(outs_p, 's5i'),
            st(outs_p, 'fk'), st(outs_p, 'fv'), st(outs_p, 'flf'),
            st(outs_s, 'ckv'), st(outs_s, 'kpe'), st(outs_s, 's5r'), st(outs_s, 's5i'),
            st(outs_s, 'fk'), st(outs_s, 'fv'), st(outs_s, 'flf'))
```

```python
import functools
import math

import jax
import jax.numpy as jnp
from jax import lax
from jax.experimental import pallas as pl
from jax.experimental.pallas import tpu as pltpu

F32 = jnp.float32
BF16 = jnp.bfloat16

LANE = 128
VMEM_LIMIT_BYTES = 52 * 1024 * 1024

RMS_EPS = 1e-6
ROPE_THETA = 10000.0
NEG_BIG = -1e30
LOG2E = math.log2(math.e)

MLA_HEADS = 8
MLA_NOPE = 64
MLA_ROPE = 32
MLA_V = 64
MLA_SCALE = (MLA_NOPE + MLA_ROPE) ** -0.5
Q_LORA = 384
KV_LORA = 256
S5_GROUP = 16
S5_STATE = 64
FOX_HEADS = 8
FOX_HD = 128
FOX_SCALE = FOX_HD ** -0.5
PAGE = 128
MLA_GROUP = 4

NT_DIMS = (((1,), (1,)), ((), ()))


def _params(*sem):
    return pltpu.CompilerParams(dimension_semantics=sem, vmem_limit_bytes=VMEM_LIMIT_BYTES)


def _dot(a, b):
    return jnp.dot(a, b, preferred_element_type=F32)


def _dot_nt(a, b):
    return lax.dot_general(a, b, NT_DIMS, preferred_element_type=F32)


def _split3(x):
    hi = x.astype(BF16)
    r1 = x - hi.astype(F32)
    mid = r1.astype(BF16)
    lo = (r1 - mid.astype(F32)).astype(BF16)
    return hi, mid, lo


def _rms(x, n):
    return lax.rsqrt(jnp.sum(x * x, axis=-1, keepdims=True) * (1.0 / n) + RMS_EPS)


def _modulate(x, g, scale, shift):
    return (x * _rms(x, x.shape[-1]) * g) * (1.0 + scale) + shift


def _silu(x):
    return x * jax.nn.sigmoid(x)


def _ada_kernel(c_ref, w_ref, b_ref, o_ref):
    a = _silu(c_ref[...]).astype(BF16)
    o_ref[...] = _dot(a, w_ref[...].astype(BF16)) + b_ref[...]


def ada_modulation(c_all, w_ada, b_ada):
    depth, d, n = w_ada.shape
    m = c_all.shape[0]
    tn = 1024
    return pl.pallas_call(
        _ada_kernel,
        grid=(depth, n // tn),
        in_specs=[
            pl.BlockSpec((m, d), lambda i, j: (0, 0)),
            pl.BlockSpec((None, d, tn), lambda i, j: (i, 0, j)),
            pl.BlockSpec((None, 1, tn), lambda i, j: (i, 0, j)),
        ],
        out_specs=pl.BlockSpec((None, m, tn), lambda i, j: (i, 0, j)),
        out_shape=jax.ShapeDtypeStruct((depth, m, n), F32),
        compiler_params=_params("parallel", "parallel"),
        name="ada",
    )(c_all, w_ada, b_ada.reshape(depth, 1, n))


def _row_spec(arr, tm):
    return pl.BlockSpec((None, tm, arr.shape[-1]), lambda b, l: (b, l, 0))


def _mod_spec(arr, tm):
    if arr.shape[1] == 1:
        return pl.BlockSpec((None, 1, arr.shape[-1]), lambda b, l: (b, 0, 0))
    return pl.BlockSpec((None, tm, arr.shape[-1]), lambda b, l: (b, l, 0))


def _full_spec(arr):
    nd = arr.ndim
    return pl.BlockSpec(arr.shape, lambda *_: (0,) * nd)


def _ffn_kernel(x_ref, g_ref, sh_ref, sc_ref, gt_ref, w1_ref, w3_ref, w2_ref, o_ref, *, tf):
    x = x_ref[...]
    h = _modulate(x, g_ref[...], sc_ref[...], sh_ref[...]).astype(BF16)
    acc = None
    for c in range(w1_ref.shape[-1] // tf):
        cs = slice(c * tf, (c + 1) * tf)
        t = (_silu(_dot(h, w1_ref[:, cs])) * _dot(h, w3_ref[:, cs])).astype(BF16)
        part = _dot(t, w2_ref[cs, :])
        acc = part if acc is None else acc + part
    o_ref[...] = x + 0.5 * gt_ref[...] * acc


def ffn_sublayer(x, g, shift, scale, gate, w1, w3, w2, li, half, tm, tf):
    nb, r, d = x.shape
    f = w1.shape[-1]
    wspec13 = pl.BlockSpec((None, None, d, f), lambda b, l: (li, half, 0, 0), pipeline_mode=pl.Buffered(1))
    wspec2 = pl.BlockSpec((None, None, f, d), lambda b, l: (li, half, 0, 0), pipeline_mode=pl.Buffered(1))
    return pl.pallas_call(
        functools.partial(_ffn_kernel, tf=tf),
        grid=(nb, r // tm),
        in_specs=[_row_spec(x, tm), pl.BlockSpec((1, d), lambda b, l: (0, 0)),
                  _mod_spec(shift, tm), _mod_spec(scale, tm), _mod_spec(gate, tm),
                  wspec13, wspec13, wspec2],
        out_specs=_row_spec(x, tm),
        out_shape=jax.ShapeDtypeStruct(x.shape, F32),
        compiler_params=_params("parallel", "parallel"),
        name="ffn",
    )(x, g.reshape(1, d), shift, scale, gate, w1, w3, w2)


def _aproj_kernel(x_ref, g_ref, sh_ref, sc_ref, gaq_ref, gbq_ref, gak_ref, gbk_ref, gq_ref, gkv_ref, gkn_ref,
                  win_ref, wq_ref, wkv_ref, ckv_ref, kpe_ref, u_ref, qf_ref, kf_ref, v_ref):
    h = _modulate(x_ref[...], g_ref[...], sc_ref[...], sh_ref[...]).astype(BF16)
    proj = _dot(h, win_ref[...])
    o1 = Q_LORA
    o2 = o1 + KV_LORA
    o3 = o2 + 512
    cq = proj[:, :o1]
    cqn = (cq * _rms(cq, Q_LORA) * gq_ref[...]).astype(BF16)
    ckv_raw = proj[:, o1:o2]
    ckv = ckv_raw * _rms(ckv_raw, KV_LORA) * gkv_ref[...]
    ckv_ref[...] = ckv
    u_ref[...] = proj[:, o2:o3]
    ka = proj[:, o3:o3 + LANE]
    kb = proj[:, o3 + LANE:o3 + 2 * LANE]
    kpe_blk = _rms(ka, MLA_ROPE) * (ka * gak_ref[...] + kb * gbk_ref[...])
    kpe_ref[...] = kpe_blk[:, MLA_NOPE:MLA_NOPE + MLA_ROPE]

    qraw = _dot(cqn, wq_ref[...])
    kv = _dot(ckv.astype(BF16), wkv_ref[...])
    lane = lax.broadcasted_iota(jnp.int32, (1, LANE), 1)
    is_n = lane < MLA_NOPE
    is_p = jnp.logical_and(lane >= MLA_NOPE, lane < MLA_NOPE + MLA_ROPE)
    gaq = gaq_ref[...]
    gbq = gbq_ref[...]
    gkn = gkn_ref[...]
    hw = MLA_HEADS * LANE
    for hh in range(MLA_HEADS):
        sl = slice(hh * LANE, (hh + 1) * LANE)
        a = qraw[:, sl]
        b = qraw[:, hw + hh * LANE:hw + (hh + 1) * LANE]
        sq = a * a
        msn = jnp.sum(jnp.where(is_n, sq, 0.0), axis=-1, keepdims=True) * (1.0 / MLA_NOPE)
        msp = jnp.sum(jnp.where(is_p, sq, 0.0), axis=-1, keepdims=True) * (1.0 / MLA_ROPE)
        r = jnp.where(is_n, lax.rsqrt(msn + RMS_EPS), lax.rsqrt(msp + RMS_EPS))
        qf_ref[:, sl] = (r * (a * gaq + b * gbq)).astype(BF16)
        kk = kv[:, sl]
        kf_ref[:, sl] = (kk * _rms(kk, MLA_NOPE) * gkn + kpe_blk).astype(BF16)
    v_ref[...] = kv[:, hw:].astype(BF16)


def mla_s5_project(x, g, shift, scale, tabs, gq, gkv, gkn, win, wq, wkv, tm):
    nb, r, d = x.shape
    gaq, gbq, gak, gbk = tabs
    tab_spec = pl.BlockSpec((tm, LANE), lambda b, l: (l, 0))
    hw = MLA_HEADS * LANE
    hv = MLA_HEADS * MLA_V
    u_w = win.shape[1] - Q_LORA - KV_LORA - 2 * LANE
    out_shape = [
        jax.ShapeDtypeStruct((nb, r, KV_LORA), F32),
        jax.ShapeDtypeStruct((nb, r, MLA_ROPE), F32),
        jax.ShapeDtypeStruct((r, nb * u_w), F32),
        jax.ShapeDtypeStruct((nb, r, hw), BF16),
        jax.ShapeDtypeStruct((nb, r, hw), BF16),
        jax.ShapeDtypeStruct((nb, r, hv), BF16),
    ]

    def rs(c):
        return pl.BlockSpec((None, tm, c), lambda b, l: (b, l, 0))

    out_specs = [rs(KV_LORA), rs(MLA_ROPE), pl.BlockSpec((tm, u_w), lambda b, l: (l, b)), rs(hw), rs(hw), rs(hv)]
    return pl.pallas_call(
        _aproj_kernel,
        grid=(nb, r // tm),
        in_specs=[_row_spec(x, tm), _full_spec(g), _mod_spec(shift, tm), _mod_spec(scale, tm),
                  tab_spec, tab_spec, tab_spec, tab_spec, _full_spec(gq), _full_spec(gkv), _full_spec(gkn),
                  _full_spec(win), _full_spec(wq), _full_spec(wkv)],
        out_specs=out_specs,
        out_shape=out_shape,
        compiler_params=_params("parallel", "parallel"),
        name="mla_s5_project",
    )(x, g, shift, scale, gaq, gbq, gak, gbk, gq, gkv, gkn, win, wq, wkv)


def _lane_fold(x, op):
    out = x[:, :LANE]
    for c in range(1, x.shape[1] // LANE):
        out = op(out, x[:, c * LANE:(c + 1) * LANE])
    return out


def _flash_kernel(*refs, hp, dv, has_bias, tq):
    if has_bias:
        q_ref, k_ref, v_ref, fq_ref, fk_ref, o_ref, mr_scr, lp_scr, acc_scr = refs
    else:
        q_ref, k_ref, v_ref, o_ref, mr_scr, lp_scr, acc_scr = refs
    qi = pl.program_id(2)
    tk = tq
    row = lax.broadcasted_iota(jnp.int32, (tq, tk), 0)
    col = lax.broadcasted_iota(jnp.int32, (tq, tk), 1)
    heads = range(hp)
    qs = [q_ref[:, i * LANE:(i + 1) * LANE] for i in heads]
    fqb = [jnp.broadcast_to(fq_ref[i], (tq, tk)) for i in heads] if has_bias else None

    def raw(i, j):
        return _dot_nt(qs[i], k_ref[pl.ds(pl.multiple_of(j * tk, tk), tk), i * LANE:(i + 1) * LANE])

    def vals(i, j):
        g = (i * dv) // LANE
        return v_ref[pl.ds(pl.multiple_of(j * tk, tk), tk), g * LANE:(g + 1) * LANE]

    def scores(i, j):
        s = raw(i, j)
        if has_bias:
            s = s + fqb[i] - fk_ref[i, j]
        return s

    s_d = [jnp.where(col <= row, scores(i, qi), NEG_BIG) for i in heads]
    for i in heads:
        mr_scr[i] = _lane_fold(s_d[i], jnp.maximum)

    def pass1(j, c):
        for i in heads:
            mr_scr[i] = jnp.maximum(mr_scr[i], _lane_fold(scores(i, j), jnp.maximum))
        return c

    lax.fori_loop(0, qi, pass1, 0)
    shift = []
    for i in heads:
        m = jnp.max(mr_scr[i], axis=-1, keepdims=True)
        p_d = jnp.exp2(s_d[i] - m)
        acc_scr[i] = _dot(p_d.astype(BF16), vals(i, qi))
        lp_scr[i] = _lane_fold(p_d, jnp.add)
        shift.append((fqb[i] - m) if has_bias else jnp.broadcast_to(m, (tq, tk)))

    def pass2(j, c):
        for i in heads:
            if has_bias:
                p = jnp.exp2(raw(i, j) + shift[i] - fk_ref[i, j])
            else:
                p = jnp.exp2(raw(i, j) - shift[i])
            acc_scr[i] += _dot(p.astype(BF16), vals(i, j))
            lp_scr[i] += _lane_fold(p, jnp.add)
        return c

    lax.fori_loop(0, qi, pass2, 0)
    outs = [acc_scr[i] / jnp.sum(lp_scr[i], axis=-1, keepdims=True) for i in heads]
    lane = lax.broadcasted_iota(jnp.int32, (1, LANE), 1)
    per_group = LANE // dv
    groups = []
    for g in range(hp // per_group):
        out = outs[g * per_group]
        for t in range(1, per_group):
            out = jnp.where(lane >= t * dv, outs[g * per_group + t], out)
        groups.append(out)
    o_ref[...] = (groups[0] if len(groups) == 1 else jnp.concatenate(groups, axis=1)).astype(o_ref.dtype)


def flash_prompt(q, k, v, fq, fk, hp, dv, tq):
    b, l, hw = q.shape
    nh = hw // LANE
    assert (hp * dv) % LANE == 0 and LANE % dv == 0 and l % tq == 0
    has_bias = fq is not None
    in_specs = [
        pl.BlockSpec((None, tq, hp * LANE), lambda bb, h, i: (bb, i, h)),
        pl.BlockSpec((None, l, hp * LANE), lambda bb, h, i: (bb, 0, h)),
        pl.BlockSpec((None, l, hp * dv), lambda bb, h, i: (bb, 0, h)),
    ]
    args = [q, k, v]
    if has_bias:
        in_specs += [
            pl.BlockSpec((None, hp, tq, 1), lambda bb, h, i: (bb, h, i, 0)),
            pl.BlockSpec((None, hp, l // tq, 1, tq), lambda bb, h, i: (bb, h, 0, 0, 0)),
        ]
        args += [fq, fk]
    return pl.pallas_call(
        functools.partial(_flash_kernel, hp=hp, dv=dv, has_bias=has_bias, tq=tq),
        grid=(b, nh // hp, l // tq),
        in_specs=in_specs,
        out_specs=pl.BlockSpec((None, tq, hp * dv), lambda bb, h, i: (bb, i, h)),
        out_shape=jax.ShapeDtypeStruct((b, l, nh * dv), BF16),
        scratch_shapes=[pltpu.VMEM((hp, tq, LANE), F32), pltpu.VMEM((hp, tq, LANE), F32),
                        pltpu.VMEM((hp, tq, LANE), F32)],
        compiler_params=_params("parallel", "parallel", "arbitrary"),
        name="flash_bias" if has_bias else "flash",
    )(*args)


def _gelu_tanh(x):
    c = math.sqrt(2.0 / math.pi)
    return x * (0.5 * (1.0 + jnp.tanh(c * (x + 0.044715 * (x * x * x)))))


def _s5_kernel(u_ref, h0r_ref, h0i_ref, ar_ref, ai_ref, bd_ref, bdlo_ref, cd_ref, d_ref, wg_ref, bg_ref,
               y_ref, xr_ref, xi_ref, sr, si, xr_scr, xi_scr, *, t_blk, bt, precise):
    i = pl.program_id(0)
    rows = t_blk * bt
    ns = xr_scr.shape[-1]
    nh = ns // 2
    ch = u_ref.shape[-1] // 2

    @pl.when(i == 0)
    def _():
        xr_scr[...] = h0r_ref[...]
        xi_scr[...] = h0i_ref[...]

    u = u_ref[...].reshape(rows, 2 * ch)
    ub = u.astype(BF16)
    for hf in range(2):
        uh = ub[:, hf * ch:(hf + 1) * ch]
        bu = _dot(uh, bd_ref[hf])
        if precise:
            ulo = (u[:, hf * ch:(hf + 1) * ch] - uh.astype(F32)).astype(BF16)
            bu = bu + _dot(ulo, bd_ref[hf]) + _dot(uh, bdlo_ref[hf])
        sr[:, hf * nh:(hf + 1) * nh] = bu[:, :nh]
        si[:, hf * nh:(hf + 1) * nh] = bu[:, nh:]

    for hf in range(2):
        cs = slice(hf * nh, (hf + 1) * nh)
        ar = jnp.broadcast_to(ar_ref[:, cs], (bt, nh))
        ai = jnp.broadcast_to(ai_ref[:, cs], (bt, nh))

        def step(t, carry, cs=cs, ar=ar, ai=ai):
            xr, xi = carry
            o = pl.multiple_of(t * bt, bt)
            nr = ar * xr - ai * xi + sr[pl.ds(o, bt), cs]
            ni = ar * xi + ai * xr + si[pl.ds(o, bt), cs]
            sr[pl.ds(o, bt), cs] = nr
            si[pl.ds(o, bt), cs] = ni
            return nr, ni

        xr, xi = lax.fori_loop(0, t_blk, step, (xr_scr[:, cs], xi_scr[:, cs]))
        xr_scr[:, cs] = xr
        xi_scr[:, cs] = xi

    ys = []
    for hf in range(2):
        cs = slice(hf * nh, (hf + 1) * nh)
        ys.append(_dot(sr[:, cs].astype(BF16), cd_ref[hf, :nh, :]) + _dot(si[:, cs].astype(BF16), cd_ref[hf, nh:, :]))
    y = jnp.concatenate(ys, axis=1)
    y = _gelu_tanh(y + d_ref[...] * u)
    y = y * jax.nn.sigmoid(_dot(y.astype(BF16), wg_ref[...]) + bg_ref[...])
    y_ref[...] = y.reshape(t_blk, bt, 2 * ch)

    @pl.when(i == pl.num_programs(0) - 1)
    def _():
        xr_ref[...] = xr_scr[...]
        xi_ref[...] = xi_scr[...]


def s5_mix(u3, h0r, h0i, prm, t_blk, precise):
    t, bt, c = u3.shape
    ns = h0r.shape[-1]
    rows = t_blk * bt
    ar, ai, bd, bdlo, cd, d, wg, bg = prm
    blk = pl.BlockSpec((t_blk, bt, c), lambda i: (i, 0, 0))
    ins = [u3, h0r, h0i, ar, ai, bd, bdlo, cd, d, wg, bg]
    return pl.pallas_call(
        functools.partial(_s5_kernel, t_blk=t_blk, bt=bt, precise=precise),
        grid=(t // t_blk,),
        in_specs=[blk] + [_full_spec(a) for a in ins[1:]],
        out_specs=[blk, _full_spec(h0r), _full_spec(h0i)],
        out_shape=[jax.ShapeDtypeStruct((t, bt, c), F32), jax.ShapeDtypeStruct(h0r.shape, F32),
                   jax.ShapeDtypeStruct(h0i.shape, F32)],
        scratch_shapes=[pltpu.VMEM((rows, ns), F32), pltpu.VMEM((rows, ns), F32),
                        pltpu.VMEM((bt, ns), F32), pltpu.VMEM((bt, ns), F32)],
        compiler_params=_params("arbitrary"),
        name="s5",
    )(*ins)


def _outproj_kernel(x_ref, gt_ref, *rest, n_in):
    a_refs = rest[:n_in]
    w_refs = rest[n_in:2 * n_in]
    o_ref = rest[2 * n_in]
    acc = _dot(a_refs[0][...].astype(BF16), w_refs[0][...])
    for a_ref, w_ref in zip(a_refs[1:], w_refs[1:]):
        acc = acc + _dot(a_ref[...].astype(BF16), w_ref[...])
    o_ref[...] = x_ref[...] + gt_ref[...] * acc


def mixer_out(x, gate, acts, act_specs, ws, tm):
    nb, r, d = x.shape
    return pl.pallas_call(
        functools.partial(_outproj_kernel, n_in=len(acts)),
        grid=(nb, r // tm),
        in_specs=[_row_spec(x, tm), _mod_spec(gate, tm)] + list(act_specs) + [_full_spec(w) for w in ws],
        out_specs=_row_spec(x, tm),
        out_shape=jax.ShapeDtypeStruct(x.shape, F32),
        compiler_params=_params("parallel", "parallel"),
        name="mixer_out",
    )(x, gate, *acts, *ws)


def _foxproj_kernel(x_ref, g_ref, sh_ref, sc_ref, gq_ref, gk_ref, bf_ref, w_ref,
                    k_ref, v_ref, lf_ref, qb_ref, kb_ref, vb_ref, fc_ref, carry_scr, *, seg, tm):
    l_idx = pl.program_id(1)
    h = _modulate(x_ref[...], g_ref[...], sc_ref[...], sh_ref[...]).astype(BF16)
    proj = _dot(h, w_ref[...])
    hw = FOX_HEADS * FOX_HD
    gq = gq_ref[...]
    gk = gk_ref[...]
    for hh in range(FOX_HEADS):
        sl = slice(hh * FOX_HD, (hh + 1) * FOX_HD)
        qh = proj[:, sl]
        qb_ref[:, sl] = (qh * _rms(qh, FOX_HD) * gq).astype(BF16)
        kh = proj[:, hw + hh * FOX_HD:hw + (hh + 1) * FOX_HD]
        kn = kh * _rms(kh, FOX_HD) * gk
        k_ref[:, sl] = kn
        kb_ref[:, sl] = kn.astype(BF16)
    vv = proj[:, 2 * hw:3 * hw]
    v_ref[...] = vv
    vb_ref[...] = vv.astype(BF16)
    z = proj[:, 3 * hw:] + bf_ref[...]
    lf = jnp.minimum(z, 0.0) - jnp.log1p(jnp.exp(-jnp.abs(z)))
    lf_ref[...] = lf[:, :FOX_HEADS]
    row = lax.broadcasted_iota(jnp.int32, (tm, tm), 0)
    col = lax.broadcasted_iota(jnp.int32, (tm, tm), 1)
    keep = col <= row
    if seg < tm:
        keep = jnp.logical_and(keep, (col // seg) == (row // seg))
    tri = jnp.where(keep, 1.0, 0.0).astype(BF16)
    hi, mid, lo = _split3(lf)
    cs = _dot(tri, hi) + _dot(tri, mid) + _dot(tri, lo)
    if seg > tm:
        @pl.when(l_idx == 0)
        def _():
            carry_scr[...] = jnp.zeros_like(carry_scr)

        cs = cs + carry_scr[...]
        carry_scr[...] = cs[tm - 1:tm, :]
    fc_ref[...] = cs


def fox_project(x, g, shift, scale, gq, gk, bf, w, seg, tm):
    nb, r, d = x.shape
    hw = FOX_HEADS * FOX_HD

    def rs(c):
        return pl.BlockSpec((None, tm, c), lambda b, l: (b, l, 0))

    out_shape = [
        jax.ShapeDtypeStruct((nb, r, hw), F32), jax.ShapeDtypeStruct((nb, r, hw), F32),
        jax.ShapeDtypeStruct((nb, r, FOX_HEADS), F32),
        jax.ShapeDtypeStruct((nb, r, hw), BF16), jax.ShapeDtypeStruct((nb, r, hw), BF16),
        jax.ShapeDtypeStruct((nb, r, hw), BF16), jax.ShapeDtypeStruct((nb, r, LANE), F32),
    ]
    out_specs = [rs(hw), rs(hw), rs(FOX_HEADS), rs(hw), rs(hw), rs(hw), rs(LANE)]
    return pl.pallas_call(
        functools.partial(_foxproj_kernel, seg=seg, tm=tm),
        grid=(nb, r // tm),
        in_specs=[_row_spec(x, tm), _full_spec(g), _mod_spec(shift, tm), _mod_spec(scale, tm),
                  _full_spec(gq), _full_spec(gk), _full_spec(bf), _full_spec(w)],
        out_specs=out_specs,
        out_shape=out_shape,
        scratch_shapes=[pltpu.VMEM((1, LANE), F32)],
        compiler_params=_params("parallel", "arbitrary"),
        name="fox_project",
    )(x, g, shift, scale, gq, gk, bf, w)


def _mla_sample_kernel(pt_ref, q_ref, cn_ref, kn_ref, wfull_ref, wukt_ref, wuv_ref, *rest, pps):
    ckv_refs = rest[:pps]
    kpe_refs = rest[pps:2 * pps]
    o_ref, qabs_scr, qpe_scr, ckv_scr, m_scr, l_scr, acc_scr = rest[2 * pps:]
    s_idx = pl.program_id(1)
    nq = q_ref.shape[0]
    nrow = MLA_HEADS * nq

    @pl.when(s_idx == 0)
    def _():
        m_scr[...] = jnp.full_like(m_scr, NEG_BIG)
        l_scr[...] = jnp.zeros_like(l_scr)
        acc_scr[...] = jnp.zeros_like(acc_scr)
        for hh in range(MLA_HEADS):
            res = _dot(q_ref[:, hh * LANE:(hh + 1) * LANE], wfull_ref[hh])
            qabs_scr[hh * nq:(hh + 1) * nq, :] = res[:, :KV_LORA]
            qpe_scr[hh * nq:(hh + 1) * nq, :] = res[:, KV_LORA:]

    wstack = jnp.concatenate([wukt_ref[...], qabs_scr[...].astype(BF16)], axis=0)
    qpe = qpe_scr[...].astype(BF16)
    nk = MLA_HEADS * MLA_NOPE

    def scores(cb, kpe_t):
        a = _dot_nt(wstack, cb)
        spe = _dot(qpe[:, :MLA_ROPE], kpe_t)
        rows = []
        for hh in range(MLA_HEADS):
            kr = a[hh * MLA_NOPE:(hh + 1) * MLA_NOPE, :]
            ms = jnp.sum(kr * kr, axis=0, keepdims=True) * (1.0 / MLA_NOPE)
            rows.append(a[nk + hh * nq:nk + (hh + 1) * nq, :] * lax.rsqrt(ms + RMS_EPS))
        return jnp.concatenate(rows, axis=0) + spe

    def online(s, vals):
        m_prev = m_scr[...]
        m_new = jnp.maximum(m_prev, jnp.max(s, axis=-1, keepdims=True))
        alpha = jnp.exp2(m_prev - m_new)
        p = jnp.exp2(s - m_new)
        l_scr[...] = alpha * l_scr[...] + jnp.sum(p, axis=-1, keepdims=True)
        acc_scr[...] = alpha * acc_scr[...] + _dot(p.astype(BF16), vals)
        m_scr[...] = m_new

    for i in range(pps):
        ckv_scr[i * PAGE:(i + 1) * PAGE, :] = ckv_refs[i][...].astype(BF16)
    parts = []
    for g in range(pps // MLA_GROUP):
        cb = ckv_scr[g * MLA_GROUP * PAGE:(g + 1) * MLA_GROUP * PAGE, :]
        kt = jnp.concatenate([kpe_refs[g * MLA_GROUP + t][...] for t in range(MLA_GROUP)], axis=1)
        parts.append(scores(cb, kt.astype(BF16)))
    online(jnp.concatenate(parts, axis=1), ckv_scr[...])

    @pl.when(s_idx == pl.num_programs(1) - 1)
    def _():
        pad = PAGE - nq
        cb = jnp.concatenate([cn_ref[...], jnp.zeros((pad, KV_LORA), F32)], axis=0).astype(BF16)
        kb = kn_ref[...].astype(BF16)
        s = scores(cb, kb)
        row = lax.broadcasted_iota(jnp.int32, (nrow, PAGE), 0)
        col = lax.broadcasted_iota(jnp.int32, (nrow, PAGE), 1)
        s = jnp.where(col <= row % nq, s, NEG_BIG)
        online(s, cb)
        o_lat = (acc_scr[...] / l_scr[...]).astype(BF16)
        out = _dot(o_lat[0:nq, :], wuv_ref[0])
        for hh in range(1, MLA_HEADS):
            out = out + _dot(o_lat[hh * nq:(hh + 1) * nq, :], wuv_ref[hh])
        o_ref[...] = out


def mla_attend_sample(qf, ckv_new, kpe_new, pool_ckv, pool_kpe, layer, page_table, wfull, wukt, wuv, pps):
    db, nq, hw = qf.shape
    n_pages = page_table.shape[1]
    assert n_pages % pps == 0 and pps % MLA_GROUP == 0
    nrow = MLA_HEADS * nq

    def fixed(arr):
        nd = arr.ndim
        return pl.BlockSpec(arr.shape, lambda b, s, pt: (0,) * nd)

    def per_seq(arr):
        return pl.BlockSpec((None,) + arr.shape[1:], lambda b, s, pt: (b, 0, 0))

    def page_spec(arr, i):
        return pl.BlockSpec((None, None) + arr.shape[2:], lambda b, s, pt: (layer, pt[b, s * pps + i], 0, 0))

    in_specs = [per_seq(qf), per_seq(ckv_new), per_seq(kpe_new), fixed(wfull), fixed(wukt), fixed(wuv)]
    in_specs += [page_spec(pool_ckv, i) for i in range(pps)] + [page_spec(pool_kpe, i) for i in range(pps)]
    out_w = MLA_HEADS * MLA_V
    grid_spec = pltpu.PrefetchScalarGridSpec(
        num_scalar_prefetch=1,
        grid=(db, n_pages // pps),
        in_specs=in_specs,
        out_specs=pl.BlockSpec((None, nq, out_w), lambda b, s, pt: (b, 0, 0)),
        scratch_shapes=[pltpu.VMEM((nrow, KV_LORA), F32), pltpu.VMEM((nrow, LANE), F32),
                        pltpu.VMEM((pps * PAGE, KV_LORA), BF16),
                        pltpu.VMEM((nrow, 1), F32), pltpu.VMEM((nrow, 1), F32), pltpu.VMEM((nrow, KV_LORA), F32)],
    )
    return pl.pallas_call(
        functools.partial(_mla_sample_kernel, pps=pps),
        grid_spec=grid_spec,
        out_shape=jax.ShapeDtypeStruct((db, nq, out_w), F32),
        compiler_params=_params("parallel", "arbitrary"),
        name="mla_sample",
    )(page_table, qf, ckv_new, kpe_new, wfull, wukt, wuv, *([pool_ckv] * pps), *([pool_kpe] * pps))


def _suffix_flat(x):
    lane = lax.broadcasted_iota(jnp.int32, x.shape, 1)
    row = lax.broadcasted_iota(jnp.int32, x.shape, 0)
    y = x
    for s in (8, 16, 32, 64):
        y = y + jnp.where(lane + s < LANE, pltpu.roll(y, LANE - s, 1), 0.0)
    z = jnp.where(lane < FOX_HEADS, y, 0.0)
    for s in (8, 16, 32, 64):
        z = z + pltpu.roll(z, s, 1)
    v = z
    for s in (1, 2, 4):
        v = v + jnp.where(row + s < 8, pltpu.roll(v, 8 - s, 0), 0.0)
    return (y - x) + (v - z), v[0:1, :]


def _fox_sample_kernel(pt_ref, q_ref, kn_ref, vn_ref, fq_ref, fkn_ref, *rest, pps):
    k_refs = rest[:pps]
    v_refs = rest[pps:2 * pps]
    lf_refs = rest[2 * pps:3 * pps]
    o_ref, v_scr, carry_scr, m_scr, l_scr, acc_scr = rest[3 * pps:]
    s_idx = pl.program_id(1)
    nrow = q_ref.shape[0]
    prow = PAGE * FOX_HEADS

    @pl.when(s_idx == 0)
    def _():
        m_scr[...] = jnp.full_like(m_scr, NEG_BIG)
        l_scr[...] = jnp.zeros_like(l_scr)
        acc_scr[...] = jnp.zeros_like(acc_scr)
        carry_scr[...] = jnp.zeros_like(carry_scr)

    q = q_ref[...]
    row = lax.broadcasted_iota(jnp.int32, (nrow, LANE), 0)
    lane = lax.broadcasted_iota(jnp.int32, (nrow, LANE), 1)
    same_head = (row % FOX_HEADS) == (lane % FOX_HEADS)
    fqm = fq_ref[...] + jnp.where(same_head, 0.0, NEG_BIG)

    def online(s, vals):
        m_prev = m_scr[...]
        m_new = jnp.maximum(m_prev, jnp.max(s, axis=-1, keepdims=True))
        alpha = jnp.exp2(m_prev - m_new)
        p = jnp.exp2(s - m_new)
        l_scr[...] = alpha * l_scr[...] + jnp.sum(p, axis=-1, keepdims=True)
        acc_scr[...] = alpha * acc_scr[...] + _dot(p.astype(BF16), vals)
        m_scr[...] = m_new

    carry = carry_scr[...]
    parts = []
    for i in range(pps):
        r, tot = _suffix_flat(lf_refs[i][...])
        r = (r + carry) * LOG2E
        carry = carry + tot
        raw = _dot_nt(q, k_refs[i][...].astype(BF16))
        v_scr[i * prow:(i + 1) * prow, :] = v_refs[i][...].astype(BF16)
        for c in range(prow // LANE):
            parts.append(raw[:, c * LANE:(c + 1) * LANE] + (r[c:c + 1, :] + fqm))
    carry_scr[...] = carry
    online(jnp.concatenate(parts, axis=1), v_scr[...])

    @pl.when(s_idx == pl.num_programs(1) - 1)
    def _():
        pad = LANE - kn_ref.shape[0]
        kb = jnp.concatenate([kn_ref[...], jnp.zeros((pad, FOX_HD), F32)], axis=0).astype(BF16)
        vb = jnp.concatenate([vn_ref[...], jnp.zeros((pad, FOX_HD), F32)], axis=0).astype(BF16)
        s = _dot_nt(q, kb) + fqm - fkn_ref[...]
        s = jnp.where(lane // FOX_HEADS <= row // FOX_HEADS, s, NEG_BIG)
        online(s, vb)
        o_ref[...] = acc_scr[...] / l_scr[...]


def fox_attend_sample(qb, k_new, v_new, fq_col, fk_row, pool_k, pool_v, pool_lf, page_base, page_table, pps):
    db, nrow, hd = qb.shape
    n_pages = page_table.shape[1]
    assert n_pages % pps == 0

    def per_seq(arr):
        return pl.BlockSpec((None,) + arr.shape[1:], lambda b, s, pt: (b, 0, 0))

    def page_spec(arr, i):
        return pl.BlockSpec((None,) + arr.shape[1:],
                            lambda b, s, pt: (page_base + pt[b, n_pages - 1 - (s * pps + i)], 0, 0))

    in_specs = [per_seq(qb), per_seq(k_new), per_seq(v_new), per_seq(fq_col), per_seq(fk_row)]
    for pool in (pool_k, pool_v, pool_lf):
        in_specs += [page_spec(pool, i) for i in range(pps)]
    grid_spec = pltpu.PrefetchScalarGridSpec(
        num_scalar_prefetch=1,
        grid=(db, n_pages // pps),
        in_specs=in_specs,
        out_specs=pl.BlockSpec((None, nrow, hd), lambda b, s, pt: (b, 0, 0)),
        scratch_shapes=[pltpu.VMEM((pps * PAGE * FOX_HEADS, hd), BF16), pltpu.VMEM((1, LANE), F32),
                        pltpu.VMEM((nrow, 1), F32), pltpu.VMEM((nrow, 1), F32), pltpu.VMEM((nrow, hd), F32)],
    )
    return pl.pallas_call(
        functools.partial(_fox_sample_kernel, pps=pps),
        grid_spec=grid_spec,
        out_shape=jax.ShapeDtypeStruct((db, nrow, hd), F32),
        compiler_params=_params("parallel", "arbitrary"),
        name="fox_sample",
    )(page_table, qb, k_new, v_new, fq_col, fk_row, *([pool_k] * pps), *([pool_v] * pps), *([pool_lf] * pps))


def _rope_perm():
    half = MLA_ROPE // 2
    idx = jnp.arange(MLA_ROPE)
    return jnp.where(idx < half, idx + half, idx - half), jnp.where(idx < half, -1.0, 1.0).astype(F32)


def _pad_rope_block(w):
    z = jnp.zeros(w.shape[:-1] + (MLA_NOPE,), w.dtype)
    z2 = jnp.zeros(w.shape[:-1] + (LANE - MLA_NOPE - MLA_ROPE,), w.dtype)
    return jnp.concatenate([z, w, z2], axis=-1)


def _mla_weights(a_w_in, a_w_uq, a_w_ukv):
    perm, _ = _rope_perm()
    o1 = Q_LORA
    o2 = o1 + KV_LORA
    o3 = o2 + MLA_ROPE
    w_kpe = a_w_in[:, o2:o3]
    win = jnp.concatenate([a_w_in[:, :o2], a_w_in[:, o3:], _pad_rope_block(w_kpe), _pad_rope_block(w_kpe[:, perm])],
                          axis=1).astype(BF16)
    wq3 = a_w_uq.reshape(Q_LORA, MLA_HEADS, MLA_NOPE + MLA_ROPE)
    zpad = jnp.zeros((Q_LORA, MLA_HEADS, LANE - MLA_NOPE - MLA_ROPE), F32)
    wq_a = jnp.concatenate([wq3, zpad], axis=-1).reshape(Q_LORA, MLA_HEADS * LANE)
    wq_b = _pad_rope_block(wq3[..., MLA_NOPE:][..., perm]).reshape(Q_LORA, MLA_HEADS * LANE)
    wq = jnp.concatenate([wq_a, wq_b], axis=1).astype(BF16)
    wkv3 = a_w_ukv.reshape(KV_LORA, MLA_HEADS, MLA_NOPE + MLA_V)
    wk = jnp.concatenate([wkv3[..., :MLA_NOPE], jnp.zeros((KV_LORA, MLA_HEADS, LANE - MLA_NOPE), F32)], axis=-1)
    wv = wkv3[..., MLA_NOPE:]
    wkv = jnp.concatenate([wk.reshape(KV_LORA, -1), wv.reshape(KV_LORA, -1)], axis=1).astype(BF16)
    return win, wq, wkv, wkv3


def _rope_tables(pos, qn_g, qr_g, kr_g):
    half = MLA_ROPE // 2
    perm, sign = _rope_perm()
    freq = ROPE_THETA ** (-jnp.arange(half, dtype=F32) / half)
    ang = pos.astype(F32)[:, None] * freq[None, :]
    cos = jnp.concatenate([jnp.cos(ang), jnp.cos(ang)], axis=1)
    sin = jnp.concatenate([jnp.sin(ang), jnp.sin(ang)], axis=1)
    n = pos.shape[0]
    zn = jnp.zeros((n, MLA_NOPE), F32)
    zp = jnp.zeros((n, LANE - MLA_NOPE - MLA_ROPE), F32)
    qs = MLA_SCALE * LOG2E
    gaq = jnp.concatenate([jnp.broadcast_to(qn_g[None, :], (n, MLA_NOPE)), qr_g[None, :] * cos, zp], axis=1) * qs
    gbq = jnp.concatenate([zn, (sign * qr_g[perm])[None, :] * sin, zp], axis=1) * qs
    gak = jnp.concatenate([zn, kr_g[None, :] * cos, zp], axis=1)
    gbk = jnp.concatenate([zn, (sign * kr_g[perm])[None, :] * sin, zp], axis=1)
    return gaq, gbq, gak, gbk


def _s5_params(a_re, a_im, log_dt, b_re, b_im, c_re, c_im, d_skip, w_glu, b_glu):
    g, n = a_re.shape
    dt = jnp.exp(log_dt)[:, None]
    mag = jnp.exp(dt * a_re)
    abar_re, abar_im = mag * jnp.cos(dt * a_im), mag * jnp.sin(dt * a_im)
    den = a_re * a_re + a_im * a_im
    w_re = ((abar_re - 1) * a_re + abar_im * a_im) / den
    w_im = (abar_im * a_re - (abar_re - 1) * a_im) / den
    bbar_re = w_re[..., None] * b_re - w_im[..., None] * b_im
    bbar_im = w_re[..., None] * b_im + w_im[..., None] * b_re
    gh = g // 2
    eye = jnp.eye(gh, dtype=F32)

    def pack_b(bb):
        return jnp.einsum('gnc,gh->gchn', bb, eye).reshape(gh * S5_GROUP, gh * n)

    def pack_c(cc):
        return jnp.einsum('gcn,gh->gnhc', cc, eye).reshape(gh * n, gh * S5_GROUP)

    bd = jnp.stack([jnp.concatenate([pack_b(bbar_re[h * gh:(h + 1) * gh]), pack_b(bbar_im[h * gh:(h + 1) * gh])], axis=1)
                    for h in range(2)])
    cd = jnp.stack([jnp.concatenate([pack_c(c_re[h * gh:(h + 1) * gh]), -pack_c(c_im[h * gh:(h + 1) * gh])], axis=0)
                    for h in range(2)])
    bd_hi = bd.astype(BF16)
    bd_lo = (bd - bd_hi.astype(F32)).astype(BF16)
    return (abar_re.reshape(1, g * n), abar_im.reshape(1, g * n), bd_hi, bd_lo, cd.astype(BF16),
            d_skip.reshape(1, -1), w_glu.astype(BF16), b_glu.reshape(1, -1))


def kernel(x_prompt, x_sample, c_prompt, c_sample, cache_mla_ckv, cache_mla_kpe, state_s5_re, state_s5_im,
           cache_fox_k, cache_fox_v, cache_fox_logf, page_table, w_ada, b_ada, norm_g, ffn_w1, ffn_w3, ffn_w2,
           a_w_in, a_q_norm, a_kv_norm, a_w_uq, a_w_ukv, a_qn_norm, a_qr_norm, a_kn_norm, a_kr_norm,
           s5_a_re, s5_a_im, s5_log_dt, s5_b_re, s5_b_im, s5_c_re, s5_c_im, s5_d, s5_w_glu, s5_b_glu,
           a_w_out, c_w_in, c_b_f, c_q_norm, c_k_norm, c_w_out):
    B, L, D = x_prompt.shape
    DB, DS, _ = x_sample.shape
    RS = DB * DS
    n_pages = page_table.shape[1]
    n_past = n_pages * PAGE
    depth = w_ada.shape[0]

    tm_p = min(512, L)
    tm_ffn = min(1024, L)
    tf = 256
    tq = min(512, L)

    m_all = ada_modulation(jnp.concatenate([c_prompt, c_sample], axis=0), w_ada, b_ada)
    w1b, w3b, w2b = ffn_w1.astype(BF16), ffn_w3.astype(BF16), ffn_w2.astype(BF16)

    xp = x_prompt
    xs = x_sample.reshape(1, RS, D)
    outs_p = {}
    outs_s = {}
    for i in range(depth):
        mp = m_all[i, :B].reshape(B, 3, 3, 1, D)
        ms = jnp.repeat(m_all[i, B:].reshape(DB, 3, 3, D), DS, axis=0).reshape(1, RS, 3, 3, D)

        def mod_p(s, k):
            return mp[:, s, k]

        def mod_s(s, k):
            return ms[:, :, s, k]

        g = norm_g[i]
        xp = ffn_sublayer(xp, g[0], mod_p(0, 0), mod_p(0, 1), mod_p(0, 2), w1b, w3b, w2b, i, 0, tm_ffn, tf)
        xs = ffn_sublayer(xs, g[0], mod_s(0, 0), mod_s(0, 1), mod_s(0, 2), w1b, w3b, w2b, i, 0, RS, tf)
        j = i // 2
        g1 = g[1].reshape(1, D)
        if i % 2 == 0:
            win, wq, wkv, wkv3 = _mla_weights(a_w_in[j], a_w_uq[j], a_w_ukv[j])
            gq = a_q_norm[j].reshape(1, -1)
            gkv = a_kv_norm[j].reshape(1, -1)
            gkn = jnp.concatenate([a_kn_norm[j], jnp.zeros((LANE - MLA_NOPE,), F32)]).reshape(1, LANE)
            s5p = _s5_params(s5_a_re[j], s5_a_im[j], s5_log_dt[j], s5_b_re[j], s5_b_im[j], s5_c_re[j], s5_c_im[j],
                             s5_d[j], s5_w_glu[j], s5_b_glu[j])
            wo = a_w_out[j].astype(BF16)
            hv = MLA_HEADS * MLA_V
            wo_att, wo_ssm = wo[:hv], wo[hv:]
            cw = s5_d.shape[-1]
            ns = s5_a_re.shape[1] * s5_a_re.shape[2]

            tabs = _rope_tables(jnp.arange(L), a_qn_norm[j], a_qr_norm[j], a_kr_norm[j])
            ckv, kpe, u_tm, qf, kf, vb = mla_s5_project(xp, g1, mod_p(1, 0), mod_p(1, 1), tabs, gq, gkv, gkn,
                                                        win, wq, wkv, tm_p)
            att = flash_prompt(qf, kf, vb, None, None, 2, MLA_V, tq)
            zeros = jnp.zeros((B, ns), F32)
            ssm, sr, si = s5_mix(u_tm.reshape(L, B, cw), zeros, zeros, s5p, min(64, L), False)
            ssm2 = ssm.reshape(L, B * cw)
            xp = mixer_out(xp, mod_p(1, 2), [att, ssm2],
                           [pl.BlockSpec((None, tm_p, hv), lambda b, l: (b, l, 0)),
                            pl.BlockSpec((tm_p, cw), lambda b, l: (l, b))],
                           [wo_att, wo_ssm], tm_p)
            outs_p.setdefault('ckv', []).append(ckv)
            outs_p.setdefault('kpe', []).append(kpe)
            outs_p.setdefault('s5r', []).append(sr.reshape(B, -1, S5_STATE))
            outs_p.setdefault('s5i', []).append(si.reshape(B, -1, S5_STATE))

            pos_s = n_past + jnp.tile(jnp.arange(DS), DB)
            tabs = _rope_tables(pos_s, a_qn_norm[j], a_qr_norm[j], a_kr_norm[j])
            ckv, kpe, u_s, qf, kf, vb = mla_s5_project(xs, g1, mod_s(1, 0), mod_s(1, 1), tabs, gq, gkv, gkn,
                                                       win, wq, wkv, RS)
            wuk = wkv3[..., :MLA_NOPE] * a_kn_norm[j][None, None, :]
            wabs = jnp.transpose(wuk, (1, 2, 0))
            top = jnp.concatenate([wabs, jnp.zeros((MLA_HEADS, MLA_NOPE, LANE), F32)], axis=-1)
            eye_blk = jnp.concatenate([jnp.zeros((MLA_ROPE, KV_LORA), F32), jnp.eye(MLA_ROPE, LANE, dtype=F32)], axis=-1)
            mid = jnp.broadcast_to(eye_blk[None], (MLA_HEADS, MLA_ROPE, KV_LORA + LANE))
            bot = jnp.zeros((MLA_HEADS, LANE - MLA_NOPE - MLA_ROPE, KV_LORA + LANE), F32)
            wfull = jnp.concatenate([top, mid, bot], axis=1).astype(BF16)
            wukt = jnp.transpose(wkv3[..., :MLA_NOPE], (1, 2, 0)).reshape(MLA_HEADS * MLA_NOPE, KV_LORA).astype(BF16)
            wuv = jnp.einsum('khd,hg->hkgd', wkv3[..., MLA_NOPE:], jnp.eye(MLA_HEADS, dtype=F32))
            wuv = wuv.reshape(MLA_HEADS, KV_LORA, hv).astype(BF16)
            kpe_t = jnp.transpose(kpe.reshape(DB, DS, MLA_ROPE), (0, 2, 1))
            kpe_t = jnp.concatenate([kpe_t, jnp.zeros((DB, MLA_ROPE, PAGE - DS), F32)], axis=-1)
            att_s = mla_attend_sample(qf.reshape(DB, DS, -1), ckv.reshape(DB, DS, -1), kpe_t,
                                      cache_mla_ckv, jnp.swapaxes(cache_mla_kpe, 2, 3), j, page_table,
                                      wfull, wukt, wuv, 16)
            u3 = jnp.transpose(u_s.reshape(DB, DS, cw), (1, 0, 2))
            ssm, sr, si = s5_mix(u3, state_s5_re[j].reshape(DB, ns), state_s5_im[j].reshape(DB, ns), s5p, DS, True)
            ssm_s = jnp.transpose(ssm, (1, 0, 2)).reshape(1, RS, cw)
            xs = mixer_out(xs, mod_s(1, 2), [att_s.reshape(1, RS, hv), ssm_s],
                           [pl.BlockSpec((None, RS, hv), lambda b, l: (b, l, 0)),
                            pl.BlockSpec((None, RS, cw), lambda b, l: (b, l, 0))],
                           [wo_att, wo_ssm], RS)
            outs_s.setdefault('ckv', []).append(ckv.reshape(DB, DS, -1))
            outs_s.setdefault('kpe', []).append(kpe.reshape(DB, DS, -1))
            outs_s.setdefault('s5r', []).append(sr.reshape(DB, -1, S5_STATE))
            outs_s.setdefault('s5i', []).append(si.reshape(DB, -1, S5_STATE))
        else:
            hw = FOX_HEADS * FOX_HD
            wf = jnp.concatenate([c_w_in[j], jnp.zeros((D, LANE - FOX_HEADS), F32)], axis=1)
            wf = wf.astype(BF16)
            gqf = (c_q_norm[j] * (FOX_SCALE * LOG2E)).reshape(1, FOX_HD)
            gkf = c_k_norm[j].reshape(1, FOX_HD)
            bf = jnp.concatenate([c_b_f[j], jnp.zeros((LANE - FOX_HEADS,), F32)]).reshape(1, LANE)
            wo = c_w_out[j].astype(BF16)

            k32, v32, lf, qb, kb, vb, fc = fox_project(xp, g1, mod_p(1, 0), mod_p(1, 1), gqf, gkf, bf, wf, L, tm_p)
            fcs = jnp.transpose(fc[:, :, :FOX_HEADS], (0, 2, 1)) * LOG2E
            fq = fcs[..., None]
            fk = fcs.reshape(B, FOX_HEADS, L // tq, 1, tq)
            o = flash_prompt(qb, kb, vb, fq, fk, 2, FOX_HD, tq)
            xp = mixer_out(xp, mod_p(1, 2), [o], [pl.BlockSpec((None, tm_p, hw), lambda b, l: (b, l, 0))], [wo], tm_p)
            outs_p.setdefault('fk', []).append(k32.reshape(B, L, FOX_HEADS, FOX_HD))
            outs_p.setdefault('fv', []).append(v32.reshape(B, L, FOX_HEADS, FOX_HD))
            outs_p.setdefault('flf', []).append(lf)

            k32, v32, lf, qb, kb, vb, fc = fox_project(xs, g1, mod_s(1, 0), mod_s(1, 1), gqf, gkf, bf, wf, DS, RS)
            nr = DS * FOX_HEADS
            f_new = fc[0, :, :FOX_HEADS].reshape(DB, nr) * LOG2E
            fq_col = f_new[..., None]
            fk_row = jnp.concatenate([f_new, jnp.zeros((DB, LANE - nr), F32)], axis=-1)[:, None, :]
            n_pool = cache_fox_k.shape[1]
            pool_k = cache_fox_k.reshape(-1, PAGE * FOX_HEADS, FOX_HD)
            pool_v = cache_fox_v.reshape(-1, PAGE * FOX_HEADS, FOX_HD)
            pool_lf = cache_fox_logf.reshape(-1, PAGE * FOX_HEADS // LANE, LANE)
            o = fox_attend_sample(qb.reshape(DB, nr, FOX_HD), k32.reshape(DB, nr, FOX_HD), v32.reshape(DB, nr, FOX_HD),
                                  fq_col, fk_row, pool_k, pool_v, pool_lf, j * n_pool, page_table, 8)
            xs = mixer_out(xs, mod_s(1, 2), [o.reshape(1, RS, hw)],
                           [pl.BlockSpec((None, RS, hw), lambda b, l: (b, l, 0))], [wo], RS)
            outs_s.setdefault('fk', []).append(k32.reshape(DB, DS, FOX_HEADS, FOX_HD))
            outs_s.setdefault('fv', []).append(v32.reshape(DB, DS, FOX_HEADS, FOX_HD))
            outs_s.setdefault('flf', []).append(lf.reshape(DB, DS, FOX_HEADS))
        xp = ffn_sublayer(xp, g[2], mod_p(2, 0), mod_p(2, 1), mod_p(2, 2), w1b, w3b, w2b, i, 1, tm_ffn, tf)
        xs = ffn_sublayer(xs, g[2], mod_s(2, 0), mod_s(2, 1), mod_s(2, 2), w1b, w3b, w2b, i, 1, RS, tf)

    def st(d, key):
        return jnp.stack(d[key])

    return (xp, xs.reshape(DB, DS, D),
            st(outs_p, 'ckv'), st(outs_p, 'kpe'), st(outs_p, 's5r'), st(outs_p, 's5i'),
            st(outs_p, 'fk'), st(outs_p, 'fv'), st(outs_p, 'flf'),
            st(outs_s, 'ckv'), st(outs_s, 'kpe'), st(outs_s, 's5r'), st(outs_s, 's5i'),
            st(outs_s, 'fk'), st(outs_s, 'fv'), st(outs_s, 'flf'))
```

```python
import functools
import math

import jax
import jax.numpy as jnp
from jax import lax
from jax.experimental import pallas as pl
from jax.experimental.pallas import tpu as pltpu

F32 = jnp.float32
BF16 = jnp.bfloat16

LANE = 128
VMEM_LIMIT_BYTES = 56 * 1024 * 1024

RMS_EPS = 1e-6
ROPE_THETA = 10000.0
NEG_BIG = -1e30
LOG2E = math.log2(math.e)

MLA_HEADS = 8
MLA_NOPE = 64
MLA_ROPE = 32
MLA_V = 64
MLA_SCALE = (MLA_NOPE + MLA_ROPE) ** -0.5
Q_LORA = 384
KV_LORA = 256
S5_GROUP = 16
S5_STATE = 64
FOX_HEADS = 8
FOX_HD = 128
FOX_SCALE = FOX_HD ** -0.5
PAGE = 128
MLA_GROUP = 16
FOX_SUB = 8

NT_DIMS = (((1,), (1,)), ((), ()))


def _params(*sem):
    return pltpu.CompilerParams(dimension_semantics=sem, vmem_limit_bytes=VMEM_LIMIT_BYTES)


def _dot(a, b):
    return jnp.dot(a, b, preferred_element_type=F32)


def _dot_nt(a, b):
    return lax.dot_general(a, b, NT_DIMS, preferred_element_type=F32)


def _split3(x):
    hi = x.astype(BF16)
    r1 = x - hi.astype(F32)
    mid = r1.astype(BF16)
    lo = (r1 - mid.astype(F32)).astype(BF16)
    return hi, mid, lo


def _rms(x, n):
    return lax.rsqrt(jnp.sum(x * x, axis=-1, keepdims=True) * (1.0 / n) + RMS_EPS)


def _modulate(x, g, scale, shift):
    return (x * _rms(x, x.shape[-1]) * g) * (1.0 + scale) + shift


def _silu(x):
    return x * jax.nn.sigmoid(x)


def _ada_kernel(c_ref, w_ref, b_ref, o_ref):
    a = _silu(c_ref[...]).astype(BF16)
    o_ref[...] = _dot(a, w_ref[...].astype(BF16)) + b_ref[...]


def ada_modulation(c_all, w_ada, b_ada):
    depth, d, n = w_ada.shape
    m = c_all.shape[0]
    tn = 1024
    return pl.pallas_call(
        _ada_kernel,
        grid=(depth, n // tn),
        in_specs=[
            pl.BlockSpec((m, d), lambda i, j: (0, 0)),
            pl.BlockSpec((None, d, tn), lambda i, j: (i, 0, j)),
            pl.BlockSpec((None, 1, tn), lambda i, j: (i, 0, j)),
        ],
        out_specs=pl.BlockSpec((None, m, tn), lambda i, j: (i, 0, j)),
        out_shape=jax.ShapeDtypeStruct((depth, m, n), F32),
        compiler_params=_params("parallel", "parallel"),
        name="ada",
    )(c_all, w_ada, b_ada.reshape(depth, 1, n))


def _row_spec(arr, tm):
    return pl.BlockSpec((None, tm, arr.shape[-1]), lambda b, l: (b, l, 0))


def _mod_spec(arr, tm):
    if arr.shape[1] == 1:
        return pl.BlockSpec((None, 1, arr.shape[-1]), lambda b, l: (b, 0, 0))
    return pl.BlockSpec((None, tm, arr.shape[-1]), lambda b, l: (b, l, 0))


def _full_spec(arr):
    nd = arr.ndim
    return pl.BlockSpec(arr.shape, lambda *_: (0,) * nd)


def _ffn_kernel(x_ref, g_ref, sh_ref, sc_ref, gt_ref, w1_ref, w3_ref, w2_ref, *rest, tf, n_mix):
    o_ref = rest[-1]
    x = x_ref[...]
    if n_mix:
        gm_ref = rest[0]
        a_refs = rest[1:1 + n_mix]
        wo_refs = rest[1 + n_mix:1 + 2 * n_mix]
        mix = _dot(a_refs[0][...].astype(BF16), wo_refs[0][...])
        for a_ref, wo_ref in zip(a_refs[1:], wo_refs[1:]):
            mix = mix + _dot(a_ref[...].astype(BF16), wo_ref[...])
        x = x + gm_ref[...] * mix
    h = _modulate(x, g_ref[...], sc_ref[...], sh_ref[...]).astype(BF16)
    acc = None
    for c in range(w1_ref.shape[-1] // tf):
        cs = slice(c * tf, (c + 1) * tf)
        t = (_silu(_dot(h, w1_ref[:, cs])) * _dot(h, w3_ref[:, cs])).astype(BF16)
        part = _dot(t, w2_ref[cs, :])
        acc = part if acc is None else acc + part
    o_ref[...] = x + 0.5 * gt_ref[...] * acc


def ffn_sublayer(x, g, shift, scale, gate, w1, w3, w2, li, half, tm, tf, mix=None):
    nb, r, d = x.shape
    f = w1.shape[-1]

    def resident(shape, idx):
        return pl.BlockSpec(shape, lambda b, l: idx, pipeline_mode=pl.Buffered(1))

    in_specs = [_row_spec(x, tm), pl.BlockSpec((1, d), lambda b, l: (0, 0)),
                _mod_spec(shift, tm), _mod_spec(scale, tm), _mod_spec(gate, tm),
                resident((None, None, d, f), (li, half, 0, 0)), resident((None, None, d, f), (li, half, 0, 0)),
                resident((None, None, f, d), (li, half, 0, 0))]
    args = [x, g.reshape(1, d), shift, scale, gate, w1, w3, w2]
    n_mix = 0
    if mix is not None:
        gate_mix, acts, spec_fns, wos = mix
        n_mix = len(acts)
        in_specs += [_mod_spec(gate_mix, tm)] + [fn(tm) for fn in spec_fns]
        in_specs += [resident(w.shape, (0,) * w.ndim) for w in wos]
        args += [gate_mix] + list(acts) + list(wos)
    return pl.pallas_call(
        functools.partial(_ffn_kernel, tf=tf, n_mix=n_mix),
        grid=(nb, r // tm),
        in_specs=in_specs,
        out_specs=_row_spec(x, tm),
        out_shape=jax.ShapeDtypeStruct(x.shape, F32),
        compiler_params=_params("parallel", "parallel"),
        name="ffn_mix" if n_mix else "ffn",
    )(*args)


def _aproj_kernel(x_ref, g_ref, sh_ref, sc_ref, gaq_ref, gbq_ref, gak_ref, gbk_ref, gq_ref, gkv_ref, gkn_ref,
                  win_ref, wq_ref, wkv_ref, ckv_ref, kpe_ref, u_ref, qf_ref, kf_ref, v_ref):
    h = _modulate(x_ref[...], g_ref[...], sc_ref[...], sh_ref[...]).astype(BF16)
    proj = _dot(h, win_ref[...])
    o1 = Q_LORA
    o2 = o1 + KV_LORA
    o3 = o2 + 512
    cq = proj[:, :o1]
    cqn = (cq * _rms(cq, Q_LORA) * gq_ref[...]).astype(BF16)
    ckv_raw = proj[:, o1:o2]
    ckv = ckv_raw * _rms(ckv_raw, KV_LORA) * gkv_ref[...]
    ckv_ref[...] = ckv
    u_ref[...] = proj[:, o2:o3]
    ka = proj[:, o3:o3 + LANE]
    kb = proj[:, o3 + LANE:o3 + 2 * LANE]
    kpe_blk = _rms(ka, MLA_ROPE) * (ka * gak_ref[...] + kb * gbk_ref[...])
    kpe_ref[...] = kpe_blk[:, MLA_NOPE:MLA_NOPE + MLA_ROPE]

    qraw = _dot(cqn, wq_ref[...])
    kv = _dot(ckv.astype(BF16), wkv_ref[...])
    lane = lax.broadcasted_iota(jnp.int32, (1, LANE), 1)
    is_n = lane < MLA_NOPE
    is_p = jnp.logical_and(lane >= MLA_NOPE, lane < MLA_NOPE + MLA_ROPE)
    gaq = gaq_ref[...]
    gbq = gbq_ref[...]
    gkn = gkn_ref[...]
    hw = MLA_HEADS * LANE
    for hh in range(MLA_HEADS):
        sl = slice(hh * LANE, (hh + 1) * LANE)
        a = qraw[:, sl]
        b = qraw[:, hw + hh * LANE:hw + (hh + 1) * LANE]
        sq = a * a
        msn = jnp.sum(jnp.where(is_n, sq, 0.0), axis=-1, keepdims=True) * (1.0 / MLA_NOPE)
        msp = jnp.sum(jnp.where(is_p, sq, 0.0), axis=-1, keepdims=True) * (1.0 / MLA_ROPE)
        r = jnp.where(is_n, lax.rsqrt(msn + RMS_EPS), lax.rsqrt(msp + RMS_EPS))
        qf_ref[:, sl] = (r * (a * gaq + b * gbq)).astype(BF16)
        kk = kv[:, sl]
        kf_ref[:, sl] = (kk * _rms(kk, MLA_NOPE) * gkn + kpe_blk).astype(BF16)
    v_ref[...] = kv[:, hw:].astype(BF16)


def mla_s5_project(x, g, shift, scale, tabs, gq, gkv, gkn, win, wq, wkv, tm):
    nb, r, d = x.shape
    gaq, gbq, gak, gbk = tabs
    tab_spec = pl.BlockSpec((tm, LANE), lambda b, l: (l, 0))
    hw = MLA_HEADS * LANE
    hv = MLA_HEADS * MLA_V
    u_w = win.shape[1] - Q_LORA - KV_LORA - 2 * LANE
    out_shape = [
        jax.ShapeDtypeStruct((nb, r, KV_LORA), F32),
        jax.ShapeDtypeStruct((nb, r, MLA_ROPE), F32),
        jax.ShapeDtypeStruct((r, nb * u_w), F32),
        jax.ShapeDtypeStruct((nb, r, hw), BF16),
        jax.ShapeDtypeStruct((nb, r, hw), BF16),
        jax.ShapeDtypeStruct((nb, r, hv), BF16),
    ]

    def rs(c):
        return pl.BlockSpec((None, tm, c), lambda b, l: (b, l, 0))

    out_specs = [rs(KV_LORA), rs(MLA_ROPE), pl.BlockSpec((tm, u_w), lambda b, l: (l, b)), rs(hw), rs(hw), rs(hv)]
    return pl.pallas_call(
        _aproj_kernel,
        grid=(nb, r // tm),
        in_specs=[_row_spec(x, tm), _full_spec(g), _mod_spec(shift, tm), _mod_spec(scale, tm),
                  tab_spec, tab_spec, tab_spec, tab_spec, _full_spec(gq), _full_spec(gkv), _full_spec(gkn),
                  _full_spec(win), _full_spec(wq), _full_spec(wkv)],
        out_specs=out_specs,
        out_shape=out_shape,
        compiler_params=_params("parallel", "parallel"),
        name="mla_s5_project",
    )(x, g, shift, scale, gaq, gbq, gak, gbk, gq, gkv, gkn, win, wq, wkv)


def _lane_fold(x, op):
    out = x[:, :LANE]
    for c in range(1, x.shape[1] // LANE):
        out = op(out, x[:, c * LANE:(c + 1) * LANE])
    return out


def _flash_kernel(*refs, hp, dv, has_bias, tq, tk):
    if has_bias:
        q_ref, k_ref, v_ref, fq_ref, fk_ref, o_ref, mr_scr, lp_scr, acc_scr = refs
    else:
        q_ref, k_ref, v_ref, o_ref, mr_scr, lp_scr, acc_scr = refs
    qi = pl.program_id(2)
    nd = tq // tk
    row = lax.broadcasted_iota(jnp.int32, (tq, tk), 0)
    col = lax.broadcasted_iota(jnp.int32, (tq, tk), 1)
    heads = range(hp)
    qs = [q_ref[:, i * LANE:(i + 1) * LANE] for i in heads]
    fqb = [jnp.broadcast_to(fq_ref[i], (tq, tk)) for i in heads] if has_bias else None

    def raw(i, j):
        return _dot_nt(qs[i], k_ref[pl.ds(pl.multiple_of(j * tk, tk), tk), i * LANE:(i + 1) * LANE])

    def vals(i, j):
        g = (i * dv) // LANE
        return v_ref[pl.ds(pl.multiple_of(j * tk, tk), tk), g * LANE:(g + 1) * LANE]

    def scores(i, j):
        s = raw(i, j)
        if has_bias:
            s = s + fqb[i] - fk_ref[i, j]
        return s

    s_d = [[jnp.where(d * tk + col <= row, scores(i, qi * nd + d), NEG_BIG) for d in range(nd)] for i in heads]
    for i in heads:
        mr = _lane_fold(s_d[i][0], jnp.maximum)
        for d in range(1, nd):
            mr = jnp.maximum(mr, _lane_fold(s_d[i][d], jnp.maximum))
        mr_scr[i] = mr

    def pass1(j, c):
        for i in heads:
            mr_scr[i] = jnp.maximum(mr_scr[i], _lane_fold(scores(i, j), jnp.maximum))
        return c

    lax.fori_loop(0, qi * nd, pass1, 0)
    shift = []
    for i in heads:
        m = jnp.max(mr_scr[i], axis=-1, keepdims=True)
        acc = None
        lp = None
        for d in range(nd):
            p_d = jnp.exp2(s_d[i][d] - m)
            part = _dot(p_d.astype(BF16), vals(i, qi * nd + d))
            acc = part if acc is None else acc + part
            fold = _lane_fold(p_d, jnp.add)
            lp = fold if lp is None else lp + fold
        acc_scr[i] = acc
        lp_scr[i] = lp
        shift.append((fqb[i] - m) if has_bias else jnp.broadcast_to(m, (tq, tk)))

    def pass2(j, c):
        for i in heads:
            if has_bias:
                p = jnp.exp2(raw(i, j) + shift[i] - fk_ref[i, j])
            else:
                p = jnp.exp2(raw(i, j) - shift[i])
            acc_scr[i] += _dot(p.astype(BF16), vals(i, j))
            lp_scr[i] += _lane_fold(p, jnp.add)
        return c

    lax.fori_loop(0, qi * nd, pass2, 0)
    outs = [acc_scr[i] / jnp.sum(lp_scr[i], axis=-1, keepdims=True) for i in heads]
    lane = lax.broadcasted_iota(jnp.int32, (1, LANE), 1)
    per_group = LANE // dv
    groups = []
    for g in range(hp // per_group):
        out = outs[g * per_group]
        for t in range(1, per_group):
            out = jnp.where(lane >= t * dv, outs[g * per_group + t], out)
        groups.append(out)
    o_ref[...] = (groups[0] if len(groups) == 1 else jnp.concatenate(groups, axis=1)).astype(o_ref.dtype)


def flash_prompt(q, k, v, fq, fk, hp, dv, tq, tk):
    b, l, hw = q.shape
    nh = hw // LANE
    assert (hp * dv) % LANE == 0 and LANE % dv == 0 and l % tq == 0 and tq % tk == 0
    has_bias = fq is not None
    in_specs = [
        pl.BlockSpec((None, tq, hp * LANE), lambda bb, h, i: (bb, i, h)),
        pl.BlockSpec((None, l, hp * LANE), lambda bb, h, i: (bb, 0, h)),
        pl.BlockSpec((None, l, hp * dv), lambda bb, h, i: (bb, 0, h)),
    ]
    args = [q, k, v]
    if has_bias:
        in_specs += [
            pl.BlockSpec((None, hp, tq, 1), lambda bb, h, i: (bb, h, i, 0)),
            pl.BlockSpec((None, hp, l // tk, 1, tk), lambda bb, h, i: (bb, h, 0, 0, 0)),
        ]
        args += [fq, fk]
    return pl.pallas_call(
        functools.partial(_flash_kernel, hp=hp, dv=dv, has_bias=has_bias, tq=tq, tk=tk),
        grid=(b, nh // hp, l // tq),
        in_specs=in_specs,
        out_specs=pl.BlockSpec((None, tq, hp * dv), lambda bb, h, i: (bb, i, h)),
        out_shape=jax.ShapeDtypeStruct((b, l, nh * dv), BF16),
        scratch_shapes=[pltpu.VMEM((hp, tq, LANE), F32), pltpu.VMEM((hp, tq, LANE), F32),
                        pltpu.VMEM((hp, tq, LANE), F32)],
        compiler_params=_params("parallel", "parallel", "arbitrary"),
        name="flash_bias" if has_bias else "flash",
    )(*args)


def _gelu_tanh(x):
    c = math.sqrt(2.0 / math.pi)
    return x * (0.5 * (1.0 + jnp.tanh(c * (x + 0.044715 * (x * x * x)))))


def _s5_kernel(u_ref, h0r_ref, h0i_ref, ar_ref, ai_ref, bd_ref, bdlo_ref, cd_ref, d_ref, wg_ref, bg_ref,
               y_ref, xr_ref, xi_ref, sr, si, xr_scr, xi_scr, *, t_blk, bt, precise):
    i = pl.program_id(0)
    rows = t_blk * bt
    ns = xr_scr.shape[-1]
    nh = ns // 2
    ch = u_ref.shape[-1] // 2

    @pl.when(i == 0)
    def _():
        xr_scr[...] = h0r_ref[...]
        xi_scr[...] = h0i_ref[...]

    u = u_ref[...].reshape(rows, 2 * ch)
    ub = u.astype(BF16)
    for hf in range(2):
        uh = ub[:, hf * ch:(hf + 1) * ch]
        bu = _dot(uh, bd_ref[hf])
        if precise:
            ulo = (u[:, hf * ch:(hf + 1) * ch] - uh.astype(F32)).astype(BF16)
            bu = bu + _dot(ulo, bd_ref[hf]) + _dot(uh, bdlo_ref[hf])
        sr[:, hf * nh:(hf + 1) * nh] = bu[:, :nh]
        si[:, hf * nh:(hf + 1) * nh] = bu[:, nh:]

    for hf in range(2):
        cs = slice(hf * nh, (hf + 1) * nh)
        ar = jnp.broadcast_to(ar_ref[:, cs], (bt, nh))
        ai = jnp.broadcast_to(ai_ref[:, cs], (bt, nh))

        def step(t, carry, cs=cs, ar=ar, ai=ai):
            xr, xi = carry
            o = pl.multiple_of(t * bt, bt)
            nr = ar * xr - ai * xi + sr[pl.ds(o, bt), cs]
            ni = ar * xi + ai * xr + si[pl.ds(o, bt), cs]
            sr[pl.ds(o, bt), cs] = nr
            si[pl.ds(o, bt), cs] = ni
            return nr, ni

        xr, xi = lax.fori_loop(0, t_blk, step, (xr_scr[:, cs], xi_scr[:, cs]))
        xr_scr[:, cs] = xr
        xi_scr[:, cs] = xi

    ys = []
    for hf in range(2):
        cs = slice(hf * nh, (hf + 1) * nh)
        ys.append(_dot(sr[:, cs].astype(BF16), cd_ref[hf, :nh, :]) + _dot(si[:, cs].astype(BF16), cd_ref[hf, nh:, :]))
    y = jnp.concatenate(ys, axis=1)
    y = _gelu_tanh(y + d_ref[...] * u)
    y = y * jax.nn.sigmoid(_dot(y.astype(BF16), wg_ref[...]) + bg_ref[...])
    y_ref[...] = y.reshape(t_blk, bt, 2 * ch)

    @pl.when(i == pl.num_programs(0) - 1)
    def _():
        xr_ref[...] = xr_scr[...]
        xi_ref[...] = xi_scr[...]


def s5_mix(u3, h0r, h0i, prm, t_blk, precise):
    t, bt, c = u3.shape
    ns = h0r.shape[-1]
    rows = t_blk * bt
    ar, ai, bd, bdlo, cd, d, wg, bg = prm
    blk = pl.BlockSpec((t_blk, bt, c), lambda i: (i, 0, 0))
    ins = [u3, h0r, h0i, ar, ai, bd, bdlo, cd, d, wg, bg]
    return pl.pallas_call(
        functools.partial(_s5_kernel, t_blk=t_blk, bt=bt, precise=precise),
        grid=(t // t_blk,),
        in_specs=[blk] + [_full_spec(a) for a in ins[1:]],
        out_specs=[blk, _full_spec(h0r), _full_spec(h0i)],
        out_shape=[jax.ShapeDtypeStruct((t, bt, c), F32), jax.ShapeDtypeStruct(h0r.shape, F32),
                   jax.ShapeDtypeStruct(h0i.shape, F32)],
        scratch_shapes=[pltpu.VMEM((rows, ns), F32), pltpu.VMEM((rows, ns), F32),
                        pltpu.VMEM((bt, ns), F32), pltpu.VMEM((bt, ns), F32)],
        compiler_params=_params("arbitrary"),
        name="s5",
    )(*ins)


def _foxproj_kernel(x_ref, g_ref, sh_ref, sc_ref, gq_ref, gk_ref, bf_ref, w_ref,
                    k_ref, v_ref, lf_ref, qb_ref, kb_ref, vb_ref, fc_ref, carry_scr, *, seg, tm):
    l_idx = pl.program_id(1)
    h = _modulate(x_ref[...], g_ref[...], sc_ref[...], sh_ref[...]).astype(BF16)
    proj = _dot(h, w_ref[...])
    hw = FOX_HEADS * FOX_HD
    gq = gq_ref[...]
    gk = gk_ref[...]
    for hh in range(FOX_HEADS):
        sl = slice(hh * FOX_HD, (hh + 1) * FOX_HD)
        qh = proj[:, sl]
        qb_ref[:, sl] = (qh * _rms(qh, FOX_HD) * gq).astype(BF16)
        kh = proj[:, hw + hh * FOX_HD:hw + (hh + 1) * FOX_HD]
        kn = kh * _rms(kh, FOX_HD) * gk
        k_ref[:, sl] = kn
        kb_ref[:, sl] = kn.astype(BF16)
    vv = proj[:, 2 * hw:3 * hw]
    v_ref[...] = vv
    vb_ref[...] = vv.astype(BF16)
    z = proj[:, 3 * hw:] + bf_ref[...]
    lf = jnp.minimum(z, 0.0) - jnp.log1p(jnp.exp(-jnp.abs(z)))
    lf_ref[...] = lf[:, :FOX_HEADS]
    row = lax.broadcasted_iota(jnp.int32, (tm, tm), 0)
    col = lax.broadcasted_iota(jnp.int32, (tm, tm), 1)
    keep = col <= row
    if seg < tm:
        keep = jnp.logical_and(keep, (col // seg) == (row // seg))
    tri = jnp.where(keep, 1.0, 0.0).astype(BF16)
    hi, mid, lo = _split3(lf)
    cs = _dot(tri, hi) + _dot(tri, mid) + _dot(tri, lo)
    if seg > tm:
        @pl.when(l_idx == 0)
        def _():
            carry_scr[...] = jnp.zeros_like(carry_scr)

        cs = cs + carry_scr[...]
        carry_scr[...] = cs[tm - 1:tm, :]
    fc_ref[...] = cs


def fox_project(x, g, shift, scale, gq, gk, bf, w, seg, tm):
    nb, r, d = x.shape
    hw = FOX_HEADS * FOX_HD

    def rs(c):
        return pl.BlockSpec((None, tm, c), lambda b, l: (b, l, 0))

    out_shape = [
        jax.ShapeDtypeStruct((nb, r, hw), F32), jax.ShapeDtypeStruct((nb, r, hw), F32),
        jax.ShapeDtypeStruct((nb, r, FOX_HEADS), F32),
        jax.ShapeDtypeStruct((nb, r, hw), BF16), jax.ShapeDtypeStruct((nb, r, hw), BF16),
        jax.ShapeDtypeStruct((nb, r, hw), BF16), jax.ShapeDtypeStruct((nb, r, LANE), F32),
    ]
    out_specs = [rs(hw), rs(hw), rs(FOX_HEADS), rs(hw), rs(hw), rs(hw), rs(LANE)]
    return pl.pallas_call(
        functools.partial(_foxproj_kernel, seg=seg, tm=tm),
        grid=(nb, r // tm),
        in_specs=[_row_spec(x, tm), _full_spec(g), _mod_spec(shift, tm), _mod_spec(scale, tm),
                  _full_spec(gq), _full_spec(gk), _full_spec(bf), _full_spec(w)],
        out_specs=out_specs,
        out_shape=out_shape,
        scratch_shapes=[pltpu.VMEM((1, LANE), F32)],
        compiler_params=_params("parallel", "arbitrary"),
        name="fox_project",
    )(x, g, shift, scale, gq, gk, bf, w)


def _mla_sample_kernel(pt_ref, q_ref, cn_ref, kn_ref, wfull_ref, wukt_ref, wuv_ref, *rest, pps):
    ckv_refs = rest[:pps]
    kpe_refs = rest[pps:2 * pps]
    o_ref, qabs_scr, qpe_scr, ckv_scr, m_scr, l_scr, acc_scr = rest[2 * pps:]
    s_idx = pl.program_id(1)
    nq = q_ref.shape[0]
    nrow = MLA_HEADS * nq

    @pl.when(s_idx == 0)
    def _():
        m_scr[...] = jnp.full_like(m_scr, NEG_BIG)
        l_scr[...] = jnp.zeros_like(l_scr)
        acc_scr[...] = jnp.zeros_like(acc_scr)
        for hh in range(MLA_HEADS):
            res = _dot(q_ref[:, hh * LANE:(hh + 1) * LANE], wfull_ref[hh])
            qabs_scr[hh * nq:(hh + 1) * nq, :] = res[:, :KV_LORA]
            qpe_scr[hh * nq:(hh + 1) * nq, :] = res[:, KV_LORA:]

    wstack = jnp.concatenate([wukt_ref[...], qabs_scr[...].astype(BF16)], axis=0)
    qpe = qpe_scr[...].astype(BF16)
    nk = MLA_HEADS * MLA_NOPE

    def scores(cb, kpe_t):
        a = _dot_nt(wstack, cb)
        spe = _dot(qpe[:, :MLA_ROPE], kpe_t)
        rows = []
        for hh in range(MLA_HEADS):
            kr = a[hh * MLA_NOPE:(hh + 1) * MLA_NOPE, :]
            ms = jnp.sum(kr * kr, axis=0, keepdims=True) * (1.0 / MLA_NOPE)
            rows.append(a[nk + hh * nq:nk + (hh + 1) * nq, :] * lax.rsqrt(ms + RMS_EPS))
        return jnp.concatenate(rows, axis=0) + spe

    def online(s, vals):
        m_prev = m_scr[...]
        m_new = jnp.maximum(m_prev, jnp.max(s, axis=-1, keepdims=True))
        alpha = jnp.exp2(m_prev - m_new)
        p = jnp.exp2(s - m_new)
        l_scr[...] = alpha * l_scr[...] + jnp.sum(p, axis=-1, keepdims=True)
        acc_scr[...] = alpha * acc_scr[...] + _dot(p.astype(BF16), vals)
        m_scr[...] = m_new

    for i in range(pps):
        ckv_scr[i * PAGE:(i + 1) * PAGE, :] = ckv_refs[i][...].astype(BF16)
    pending = None
    for g in range(pps // MLA_GROUP):
        cb = ckv_scr[g * MLA_GROUP * PAGE:(g + 1) * MLA_GROUP * PAGE, :]
        kt = jnp.concatenate([kpe_refs[g * MLA_GROUP + t][...] for t in range(MLA_GROUP)], axis=1)
        s = scores(cb, kt.astype(BF16))
        if pending is not None:
            online(*pending)
        pending = (s, cb)
    online(*pending)

    @pl.when(s_idx == pl.num_programs(1) - 1)
    def _():
        pad = PAGE - nq
        cb = jnp.concatenate([cn_ref[...], jnp.zeros((pad, KV_LORA), F32)], axis=0).astype(BF16)
        kb = kn_ref[...].astype(BF16)
        s = scores(cb, kb)
        row = lax.broadcasted_iota(jnp.int32, (nrow, PAGE), 0)
        col = lax.broadcasted_iota(jnp.int32, (nrow, PAGE), 1)
        s = jnp.where(col <= row % nq, s, NEG_BIG)
        online(s, cb)
        o_lat = (acc_scr[...] / l_scr[...]).astype(BF16)
        out = _dot(o_lat[0:nq, :], wuv_ref[0])
        for hh in range(1, MLA_HEADS):
            out = out + _dot(o_lat[hh * nq:(hh + 1) * nq, :], wuv_ref[hh])
        o_ref[...] = out


def mla_attend_sample(qf, ckv_new, kpe_new, pool_ckv, pool_kpe, layer, page_table, wfull, wukt, wuv, pps):
    db, nq, hw = qf.shape
    n_pages = page_table.shape[1]
    assert n_pages % pps == 0 and pps % MLA_GROUP == 0
    nrow = MLA_HEADS * nq

    def fixed(arr):
        nd = arr.ndim
        return pl.BlockSpec(arr.shape, lambda b, s, pt: (0,) * nd)

    def per_seq(arr):
        return pl.BlockSpec((None,) + arr.shape[1:], lambda b, s, pt: (b, 0, 0))

    def page_spec(arr, i):
        return pl.BlockSpec((None, None) + arr.shape[2:], lambda b, s, pt: (layer, pt[b, s * pps + i], 0, 0))

    in_specs = [per_seq(qf), per_seq(ckv_new), per_seq(kpe_new), fixed(wfull), fixed(wukt), fixed(wuv)]
    in_specs += [page_spec(pool_ckv, i) for i in range(pps)] + [page_spec(pool_kpe, i) for i in range(pps)]
    out_w = MLA_HEADS * MLA_V
    grid_spec = pltpu.PrefetchScalarGridSpec(
        num_scalar_prefetch=1,
        grid=(db, n_pages // pps),
        in_specs=in_specs,
        out_specs=pl.BlockSpec((None, nq, out_w), lambda b, s, pt: (b, 0, 0)),
        scratch_shapes=[pltpu.VMEM((nrow, KV_LORA), F32), pltpu.VMEM((nrow, LANE), F32),
                        pltpu.VMEM((pps * PAGE, KV_LORA), BF16),
                        pltpu.VMEM((nrow, 1), F32), pltpu.VMEM((nrow, 1), F32), pltpu.VMEM((nrow, KV_LORA), F32)],
    )
    return pl.pallas_call(
        functools.partial(_mla_sample_kernel, pps=pps),
        grid_spec=grid_spec,
        out_shape=jax.ShapeDtypeStruct((db, nq, out_w), F32),
        compiler_params=_params("parallel", "arbitrary"),
        name="mla_sample",
    )(page_table, qf, ckv_new, kpe_new, wfull, wukt, wuv, *([pool_ckv] * pps), *([pool_kpe] * pps))


def _suffix_flat(x):
    lane = lax.broadcasted_iota(jnp.int32, x.shape, 1)
    row = lax.broadcasted_iota(jnp.int32, x.shape, 0)
    y = x
    for s in (8, 16, 32, 64):
        y = y + jnp.where(lane + s < LANE, pltpu.roll(y, LANE - s, 1), 0.0)
    z = jnp.where(lane < FOX_HEADS, y, 0.0)
    for s in (8, 16, 32, 64):
        z = z + pltpu.roll(z, s, 1)
    v = z
    for s in (1, 2, 4):
        v = v + jnp.where(row + s < 8, pltpu.roll(v, 8 - s, 0), 0.0)
    return (y - x) + (v - z), v[0:1, :]


def _fox_sample_kernel(pt_ref, q_ref, kn_ref, vn_ref, fq_ref, fkn_ref, *rest, pps):
    k_refs = rest[:pps]
    v_refs = rest[pps:2 * pps]
    lf_refs = rest[2 * pps:3 * pps]
    o_ref, carry_scr, m_scr, l_scr, acc_scr = rest[3 * pps:]
    s_idx = pl.program_id(1)
    nrow = q_ref.shape[0]
    prow = PAGE * FOX_HEADS

    @pl.when(s_idx == 0)
    def _():
        m_scr[...] = jnp.full_like(m_scr, NEG_BIG)
        l_scr[...] = jnp.zeros_like(l_scr)
        acc_scr[...] = jnp.zeros_like(acc_scr)
        carry_scr[...] = jnp.zeros_like(carry_scr)

    q = q_ref[...]
    row = lax.broadcasted_iota(jnp.int32, (nrow, LANE), 0)
    lane = lax.broadcasted_iota(jnp.int32, (nrow, LANE), 1)
    same_head = (row % FOX_HEADS) == (lane % FOX_HEADS)
    fqm = fq_ref[...] + jnp.where(same_head, 0.0, NEG_BIG)

    def online(state, s, vals):
        m_prev, l_prev, acc = state
        m_new = jnp.maximum(m_prev, jnp.max(s, axis=-1, keepdims=True))
        alpha = jnp.exp2(m_prev - m_new)
        p = jnp.exp2(s - m_new)
        l_new = alpha * l_prev + jnp.sum(p, axis=-1, keepdims=True)
        pb = p.astype(BF16)
        acc = alpha * acc
        o = 0
        for v in vals:
            acc = acc + _dot(pb[:, o:o + v.shape[0]], v)
            o += v.shape[0]
        return m_new, l_new, acc

    carry = carry_scr[...]
    state = (m_scr[...], l_scr[...], acc_scr[...])
    pending = None
    for g in range(pps // FOX_SUB):
        parts = []
        vals = []
        for i in range(g * FOX_SUB, (g + 1) * FOX_SUB):
            r, tot = _suffix_flat(lf_refs[i][...])
            r = (r + carry) * LOG2E
            carry = carry + tot
            raw = _dot_nt(q, k_refs[i][...].astype(BF16))
            vals.append(v_refs[i][...].astype(BF16))
            for c in range(prow // LANE):
                parts.append(raw[:, c * LANE:(c + 1) * LANE] + (r[c:c + 1, :] + fqm))
        if pending is not None:
            state = online(state, *pending)
        pending = (jnp.concatenate(parts, axis=1), vals)
    state = online(state, *pending)
    carry_scr[...] = carry
    m_scr[...], l_scr[...], acc_scr[...] = state

    @pl.when(s_idx == pl.num_programs(1) - 1)
    def _():
        pad = LANE - kn_ref.shape[0]
        kb = jnp.concatenate([kn_ref[...], jnp.zeros((pad, FOX_HD), F32)], axis=0).astype(BF16)
        vb = jnp.concatenate([vn_ref[...], jnp.zeros((pad, FOX_HD), F32)], axis=0).astype(BF16)
        s = _dot_nt(q, kb) + fqm - fkn_ref[...]
        s = jnp.where(lane // FOX_HEADS <= row // FOX_HEADS, s, NEG_BIG)
        _, l_fin, acc_fin = online((m_scr[...], l_scr[...], acc_scr[...]), s, [vb])
        o_ref[...] = acc_fin / l_fin


def fox_attend_sample(qb, k_new, v_new, fq_col, fk_row, pool_k, pool_v, pool_lf, page_base, page_table, pps):
    db, nrow, hd = qb.shape
    n_pages = page_table.shape[1]
    assert n_pages % pps == 0

    def per_seq(arr):
        return pl.BlockSpec((None,) + arr.shape[1:], lambda b, s, pt: (b, 0, 0))

    def page_spec(arr, i):
        return pl.BlockSpec((None,) + arr.shape[1:],
                            lambda b, s, pt: (page_base + pt[b, n_pages - 1 - (s * pps + i)], 0, 0))

    in_specs = [per_seq(qb), per_seq(k_new), per_seq(v_new), per_seq(fq_col), per_seq(fk_row)]
    for pool in (pool_k, pool_v, pool_lf):
        in_specs += [page_spec(pool, i) for i in range(pps)]
    grid_spec = pltpu.PrefetchScalarGridSpec(
        num_scalar_prefetch=1,
        grid=(db, n_pages // pps),
        in_specs=in_specs,
        out_specs=pl.BlockSpec((None, nrow, hd), lambda b, s, pt: (b, 0, 0)),
        scratch_shapes=[pltpu.VMEM((1, LANE), F32),
                        pltpu.VMEM((nrow, 1), F32), pltpu.VMEM((nrow, 1), F32), pltpu.VMEM((nrow, hd), F32)],
    )
    return pl.pallas_call(
        functools.partial(_fox_sample_kernel, pps=pps),
        grid_spec=grid_spec,
        out_shape=jax.ShapeDtypeStruct((db, nrow, hd), F32),
        compiler_params=_params("parallel", "arbitrary"),
        name="fox_sample",
    )(page_table, qb, k_new, v_new, fq_col, fk_row, *([pool_k] * pps), *([pool_v] * pps), *([pool_lf] * pps))


def _rope_perm():
    half = MLA_ROPE // 2
    idx = jnp.arange(MLA_ROPE)
    return jnp.where(idx < half, idx + half, idx - half), jnp.where(idx < half, -1.0, 1.0).astype(F32)


def _pad_rope_block(w):
    z = jnp.zeros(w.shape[:-1] + (MLA_NOPE,), w.dtype)
    z2 = jnp.zeros(w.shape[:-1] + (LANE - MLA_NOPE - MLA_ROPE,), w.dtype)
    return jnp.concatenate([z, w, z2], axis=-1)


def _mla_weights(a_w_in, a_w_uq, a_w_ukv):
    perm, _ = _rope_perm()
    o1 = Q_LORA
    o2 = o1 + KV_LORA
    o3 = o2 + MLA_ROPE
    w_kpe = a_w_in[:, o2:o3]
    win = jnp.concatenate([a_w_in[:, :o2], a_w_in[:, o3:], _pad_rope_block(w_kpe), _pad_rope_block(w_kpe[:, perm])],
                          axis=1).astype(BF16)
    wq3 = a_w_uq.reshape(Q_LORA, MLA_HEADS, MLA_NOPE + MLA_ROPE)
    zpad = jnp.zeros((Q_LORA, MLA_HEADS, LANE - MLA_NOPE - MLA_ROPE), F32)
    wq_a = jnp.concatenate([wq3, zpad], axis=-1).reshape(Q_LORA, MLA_HEADS * LANE)
    wq_b = _pad_rope_block(wq3[..., MLA_NOPE:][..., perm]).reshape(Q_LORA, MLA_HEADS * LANE)
    wq = jnp.concatenate([wq_a, wq_b], axis=1).astype(BF16)
    wkv3 = a_w_ukv.reshape(KV_LORA, MLA_HEADS, MLA_NOPE + MLA_V)
    wk = jnp.concatenate([wkv3[..., :MLA_NOPE], jnp.zeros((KV_LORA, MLA_HEADS, LANE - MLA_NOPE), F32)], axis=-1)
    wv = wkv3[..., MLA_NOPE:]
    wkv = jnp.concatenate([wk.reshape(KV_LORA, -1), wv.reshape(KV_LORA, -1)], axis=1).astype(BF16)
    return win, wq, wkv, wkv3


def _rope_tables(pos, qn_g, qr_g, kr_g):
    half = MLA_ROPE // 2
    perm, sign = _rope_perm()
    freq = ROPE_THETA ** (-jnp.arange(half, dtype=F32) / half)
    ang = pos.astype(F32)[:, None] * freq[None, :]
    cos = jnp.concatenate([jnp.cos(ang), jnp.cos(ang)], axis=1)
    sin = jnp.concatenate([jnp.sin(ang), jnp.sin(ang)], axis=1)
    n = pos.shape[0]
    zn = jnp.zeros((n, MLA_NOPE), F32)
    zp = jnp.zeros((n, LANE - MLA_NOPE - MLA_ROPE), F32)
    qs = MLA_SCALE * LOG2E
    gaq = jnp.concatenate([jnp.broadcast_to(qn_g[None, :], (n, MLA_NOPE)), qr_g[None, :] * cos, zp], axis=1) * qs
    gbq = jnp.concatenate([zn, (sign * qr_g[perm])[None, :] * sin, zp], axis=1) * qs
    gak = jnp.concatenate([zn, kr_g[None, :] * cos, zp], axis=1)
    gbk = jnp.concatenate([zn, (sign * kr_g[perm])[None, :] * sin, zp], axis=1)
    return gaq, gbq, gak, gbk


def _s5_params(a_re, a_im, log_dt, b_re, b_im, c_re, c_im, d_skip, w_glu, b_glu):
    g, n = a_re.shape
    dt = jnp.exp(log_dt)[:, None]
    mag = jnp.exp(dt * a_re)
    abar_re, abar_im = mag * jnp.cos(dt * a_im), mag * jnp.sin(dt * a_im)
    den = a_re * a_re + a_im * a_im
    w_re = ((abar_re - 1) * a_re + abar_im * a_im) / den
    w_im = (abar_im * a_re - (abar_re - 1) * a_im) / den
    bbar_re = w_re[..., None] * b_re - w_im[..., None] * b_im
    bbar_im = w_re[..., None] * b_im + w_im[..., None] * b_re
    gh = g // 2
    eye = jnp.eye(gh, dtype=F32)

    def pack_b(bb):
        return jnp.einsum('gnc,gh->gchn', bb, eye).reshape(gh * S5_GROUP, gh * n)

    def pack_c(cc):
        return jnp.einsum('gcn,gh->gnhc', cc, eye).reshape(gh * n, gh * S5_GROUP)

    bd = jnp.stack([jnp.concatenate([pack_b(bbar_re[h * gh:(h + 1) * gh]), pack_b(bbar_im[h * gh:(h + 1) * gh])], axis=1)
                    for h in range(2)])
    cd = jnp.stack([jnp.concatenate([pack_c(c_re[h * gh:(h + 1) * gh]), -pack_c(c_im[h * gh:(h + 1) * gh])], axis=0)
                    for h in range(2)])
    bd_hi = bd.astype(BF16)
    bd_lo = (bd - bd_hi.astype(F32)).astype(BF16)
    return (abar_re.reshape(1, g * n), abar_im.reshape(1, g * n), bd_hi, bd_lo, cd.astype(BF16),
            d_skip.reshape(1, -1), w_glu.astype(BF16), b_glu.reshape(1, -1))


def kernel(x_prompt, x_sample, c_prompt, c_sample, cache_mla_ckv, cache_mla_kpe, state_s5_re, state_s5_im,
           cache_fox_k, cache_fox_v, cache_fox_logf, page_table, w_ada, b_ada, norm_g, ffn_w1, ffn_w3, ffn_w2,
           a_w_in, a_q_norm, a_kv_norm, a_w_uq, a_w_ukv, a_qn_norm, a_qr_norm, a_kn_norm, a_kr_norm,
           s5_a_re, s5_a_im, s5_log_dt, s5_b_re, s5_b_im, s5_c_re, s5_c_im, s5_d, s5_w_glu, s5_b_glu,
           a_w_out, c_w_in, c_b_f, c_q_norm, c_k_norm, c_w_out):
    B, L, D = x_prompt.shape
    DB, DS, _ = x_sample.shape
    RS = DB * DS
    n_pages = page_table.shape[1]
    n_past = n_pages * PAGE
    depth = w_ada.shape[0]

    tm_p = min(512, L)
    tm_ffn = min(1024, L)
    tf = 256
    tq = min(1024, L)
    tk = min(512, L)

    m_all = ada_modulation(jnp.concatenate([c_prompt, c_sample], axis=0), w_ada, b_ada)
    w1b, w3b, w2b = ffn_w1.astype(BF16), ffn_w3.astype(BF16), ffn_w2.astype(BF16)

    xp = x_prompt
    xs = x_sample.reshape(1, RS, D)
    outs_p = {}
    outs_s = {}
    for i in range(depth):
        mp = m_all[i, :B].reshape(B, 3, 3, 1, D)
        ms = jnp.repeat(m_all[i, B:].reshape(DB, 3, 3, D), DS, axis=0).reshape(1, RS, 3, 3, D)

        def mod_p(s, k):
            return mp[:, s, k]

        def mod_s(s, k):
            return ms[:, :, s, k]

        g = norm_g[i]
        xp = ffn_sublayer(xp, g[0], mod_p(0, 0), mod_p(0, 1), mod_p(0, 2), w1b, w3b, w2b, i, 0, tm_ffn, tf)
        xs = ffn_sublayer(xs, g[0], mod_s(0, 0), mod_s(0, 1), mod_s(0, 2), w1b, w3b, w2b, i, 0, RS, tf)
        j = i // 2
        g1 = g[1].reshape(1, D)
        if i % 2 == 0:
            win, wq, wkv, wkv3 = _mla_weights(a_w_in[j], a_w_uq[j], a_w_ukv[j])
            gq = a_q_norm[j].reshape(1, -1)
            gkv = a_kv_norm[j].reshape(1, -1)
            gkn = jnp.concatenate([a_kn_norm[j], jnp.zeros((LANE - MLA_NOPE,), F32)]).reshape(1, LANE)
            s5p = _s5_params(s5_a_re[j], s5_a_im[j], s5_log_dt[j], s5_b_re[j], s5_b_im[j], s5_c_re[j], s5_c_im[j],
                             s5_d[j], s5_w_glu[j], s5_b_glu[j])
            wo = a_w_out[j].astype(BF16)
            hv = MLA_HEADS * MLA_V
            wo_att, wo_ssm = wo[:hv], wo[hv:]
            cw = s5_d.shape[-1]
            ns = s5_a_re.shape[1] * s5_a_re.shape[2]

            tabs = _rope_tables(jnp.arange(L), a_qn_norm[j], a_qr_norm[j], a_kr_norm[j])
            ckv, kpe, u_tm, qf, kf, vb = mla_s5_project(xp, g1, mod_p(1, 0), mod_p(1, 1), tabs, gq, gkv, gkn,
                                                        win, wq, wkv, tm_p)
            att = flash_prompt(qf, kf, vb, None, None, 2, MLA_V, tq, tk)
            zeros = jnp.zeros((B, ns), F32)
            ssm, sr, si = s5_mix(u_tm.reshape(L, B, cw), zeros, zeros, s5p, min(64, L), False)
            ssm2 = ssm.reshape(L, B * cw)
            mix_p = (mod_p(1, 2), [att, ssm2],
                     [lambda tm, w=hv: pl.BlockSpec((None, tm, w), lambda b, l: (b, l, 0)),
                      lambda tm, w=cw: pl.BlockSpec((tm, w), lambda b, l: (l, b))],
                     [wo_att, wo_ssm])
            outs_p.setdefault('ckv', []).append(ckv)
            outs_p.setdefault('kpe', []).append(kpe)
            outs_p.setdefault('s5r', []).append(sr.reshape(B, -1, S5_STATE))
            outs_p.setdefault('s5i', []).append(si.reshape(B, -1, S5_STATE))

            pos_s = n_past + jnp.tile(jnp.arange(DS), DB)
            tabs = _rope_tables(pos_s, a_qn_norm[j], a_qr_norm[j], a_kr_norm[j])
            ckv, kpe, u_s, qf, kf, vb = mla_s5_project(xs, g1, mod_s(1, 0), mod_s(1, 1), tabs, gq, gkv, gkn,
                                                       win, wq, wkv, RS)
            wuk = wkv3[..., :MLA_NOPE] * a_kn_norm[j][None, None, :]
            wabs = jnp.transpose(wuk, (1, 2, 0))
            top = jnp.concatenate([wabs, jnp.zeros((MLA_HEADS, MLA_NOPE, LANE), F32)], axis=-1)
            eye_blk = jnp.concatenate([jnp.zeros((MLA_ROPE, KV_LORA), F32), jnp.eye(MLA_ROPE, LANE, dtype=F32)], axis=-1)
            mid = jnp.broadcast_to(eye_blk[None], (MLA_HEADS, MLA_ROPE, KV_LORA + LANE))
            bot = jnp.zeros((MLA_HEADS, LANE - MLA_NOPE - MLA_ROPE, KV_LORA + LANE), F32)
            wfull = jnp.concatenate([top, mid, bot], axis=1).astype(BF16)
            wukt = jnp.transpose(wkv3[..., :MLA_NOPE], (1, 2, 0)).reshape(MLA_HEADS * MLA_NOPE, KV_LORA).astype(BF16)
            wuv = jnp.einsum('khd,hg->hkgd', wkv3[..., MLA_NOPE:], jnp.eye(MLA_HEADS, dtype=F32))
            wuv = wuv.reshape(MLA_HEADS, KV_LORA, hv).astype(BF16)
            kpe_t = jnp.transpose(kpe.reshape(DB, DS, MLA_ROPE), (0, 2, 1))
            kpe_t = jnp.concatenate([kpe_t, jnp.zeros((DB, MLA_ROPE, PAGE - DS), F32)], axis=-1)
            att_s = mla_attend_sample(qf.reshape(DB, DS, -1), ckv.reshape(DB, DS, -1), kpe_t,
                                      cache_mla_ckv, jnp.swapaxes(cache_mla_kpe, 2, 3), j, page_table,
                                      wfull, wukt, wuv, 16)
            u3 = jnp.transpose(u_s.reshape(DB, DS, cw), (1, 0, 2))
            ssm, sr, si = s5_mix(u3, state_s5_re[j].reshape(DB, ns), state_s5_im[j].reshape(DB, ns), s5p, DS, True)
            ssm_s = jnp.transpose(ssm, (1, 0, 2)).reshape(1, RS, cw)
            mix_s = (mod_s(1, 2), [att_s.reshape(1, RS, hv), ssm_s],
                     [lambda tm, w=hv: pl.BlockSpec((None, tm, w), lambda b, l: (b, l, 0)),
                      lambda tm, w=cw: pl.BlockSpec((None, tm, w), lambda b, l: (b, l, 0))],
                     [wo_att, wo_ssm])
            outs_s.setdefault('ckv', []).append(ckv.reshape(DB, DS, -1))
            outs_s.setdefault('kpe', []).append(kpe.reshape(DB, DS, -1))
            outs_s.setdefault('s5r', []).append(sr.reshape(DB, -1, S5_STATE))
            outs_s.setdefault('s5i', []).append(si.reshape(DB, -1, S5_STATE))
        else:
            hw = FOX_HEADS * FOX_HD
            wf = jnp.concatenate([c_w_in[j], jnp.zeros((D, LANE - FOX_HEADS), F32)], axis=1)
            wf = wf.astype(BF16)
            gqf = (c_q_norm[j] * (FOX_SCALE * LOG2E)).reshape(1, FOX_HD)
            gkf = c_k_norm[j].reshape(1, FOX_HD)
            bf = jnp.concatenate([c_b_f[j], jnp.zeros((LANE - FOX_HEADS,), F32)]).reshape(1, LANE)
            wo = c_w_out[j].astype(BF16)

            k32, v32, lf, qb, kb, vb, fc = fox_project(xp, g1, mod_p(1, 0), mod_p(1, 1), gqf, gkf, bf, wf, L, tm_p)
            fcs = jnp.transpose(fc[:, :, :FOX_HEADS], (0, 2, 1)) * LOG2E
            fq = fcs[..., None]
            fk = fcs.reshape(B, FOX_HEADS, L // tk, 1, tk)
            o = flash_prompt(qb, kb, vb, fq, fk, 2, FOX_HD, tq, tk)
            mix_p = (mod_p(1, 2), [o], [lambda tm, w=hw: pl.BlockSpec((None, tm, w), lambda b, l: (b, l, 0))], [wo])
            outs_p.setdefault('fk', []).append(k32.reshape(B, L, FOX_HEADS, FOX_HD))
            outs_p.setdefault('fv', []).append(v32.reshape(B, L, FOX_HEADS, FOX_HD))
            outs_p.setdefault('flf', []).append(lf)

            k32, v32, lf, qb, kb, vb, fc = fox_project(xs, g1, mod_s(1, 0), mod_s(1, 1), gqf, gkf, bf, wf, DS, RS)
            nr = DS * FOX_HEADS
            f_new = fc[0, :, :FOX_HEADS].reshape(DB, nr) * LOG2E
            fq_col = f_new[..., None]
            fk_row = jnp.concatenate([f_new, jnp.zeros((DB, LANE - nr), F32)], axis=-1)[:, None, :]
            n_pool = cache_fox_k.shape[1]
            pool_k = cache_fox_k.reshape(-1, PAGE * FOX_HEADS, FOX_HD)
            pool_v = cache_fox_v.reshape(-1, PAGE * FOX_HEADS, FOX_HD)
            pool_lf = cache_fox_logf.reshape(-1, PAGE * FOX_HEADS // LANE, LANE)
            o = fox_attend_sample(qb.reshape(DB, nr, FOX_HD), k32.reshape(DB, nr, FOX_HD), v32.reshape(DB, nr, FOX_HD),
                                  fq_col, fk_row, pool_k, pool_v, pool_lf, j * n_pool, page_table, 16)
            mix_s = (mod_s(1, 2), [o.reshape(1, RS, hw)],
                     [lambda tm, w=hw: pl.BlockSpec((None, tm, w), lambda b, l: (b, l, 0))], [wo])
            outs_s.setdefault('fk', []).append(k32.reshape(DB, DS, FOX_HEADS, FOX_HD))
            outs_s.setdefault('fv', []).append(v32.reshape(DB, DS, FOX_HEADS, FOX_HD))
            outs_s.setdefault('flf', []).append(lf.reshape(DB, DS, FOX_HEADS))
        xp = ffn_sublayer(xp, g[2], mod_p(2, 0), mod_p(2, 1), mod_p(2, 2), w1b, w3b, w2b, i, 1, tm_ffn, tf, mix_p)
        xs = ffn_sublayer(xs, g[2], mod_s(2, 0), mod_s(2, 1), mod_s(2, 2), w1b, w3b, w2b, i, 1, RS, tf, mix_s)

    def st(d, key):
        return jnp.stack(d[key])

    return (xp, xs.reshape(DB, DS, D),
            st(outs_p, 'ckv'), st(outs_p, 'kpe'), st(outs_p, 's5r'), st(outs_p, 's5i'),
            st(outs_p, 'fk'), st(outs_p, 'fv'), st(outs_p, 'flf'),
            st(outs_s, 'ckv'), st(outs_s, 'kpe'), st(outs_s, 's5r'), st(outs_s, 's5i'),
            st(outs_s, 'fk'), st(outs_s, 'fv'), st(outs_s, 'flf'))
```

```python
import functools
import math

import jax
import jax.numpy as jnp
from jax import lax
from jax.experimental import pallas as pl
from jax.experimental.pallas import tpu as pltpu

F32 = jnp.float32
BF16 = jnp.bfloat16

LANE = 128
VMEM_LIMIT_BYTES = 56 * 1024 * 1024

RMS_EPS = 1e-6
ROPE_THETA = 10000.0
NEG_BIG = -1e30
LOG2E = math.log2(math.e)

MLA_HEADS = 8
MLA_NOPE = 64
MLA_ROPE = 32
MLA_V = 64
MLA_SCALE = (MLA_NOPE + MLA_ROPE) ** -0.5
Q_LORA = 384
KV_LORA = 256
S5_GROUP = 16
S5_STATE = 64
FOX_HEADS = 8
FOX_HD = 128
FOX_SCALE = FOX_HD ** -0.5
PAGE = 128
FOX_SUB = 8

NT_DIMS = (((1,), (1,)), ((), ()))


def _params(*sem):
    return pltpu.CompilerParams(dimension_semantics=sem, vmem_limit_bytes=VMEM_LIMIT_BYTES)


def _dot(a, b):
    return jnp.dot(a, b, preferred_element_type=F32)


def _dot_nt(a, b):
    return lax.dot_general(a, b, NT_DIMS, preferred_element_type=F32)


def _split3(x):
    hi = x.astype(BF16)
    r1 = x - hi.astype(F32)
    mid = r1.astype(BF16)
    lo = (r1 - mid.astype(F32)).astype(BF16)
    return hi, mid, lo


def _rms(x, n):
    return lax.rsqrt(jnp.sum(x * x, axis=-1, keepdims=True) * (1.0 / n) + RMS_EPS)


def _modulate(x, g, scale, shift):
    return (x * _rms(x, x.shape[-1]) * g) * (1.0 + scale) + shift


def _silu(x):
    return x * jax.nn.sigmoid(x)


def _ada_kernel(c_ref, w_ref, b_ref, o_ref):
    a = _silu(c_ref[...]).astype(BF16)
    o_ref[...] = _dot(a, w_ref[...].astype(BF16)) + b_ref[...]


def ada_modulation(c_all, w_ada, b_ada):
    depth, d, n = w_ada.shape
    m = c_all.shape[0]
    tn = 1024
    return pl.pallas_call(
        _ada_kernel,
        grid=(depth, n // tn),
        in_specs=[
            pl.BlockSpec((m, d), lambda i, j: (0, 0)),
            pl.BlockSpec((None, d, tn), lambda i, j: (i, 0, j)),
            pl.BlockSpec((None, 1, tn), lambda i, j: (i, 0, j)),
        ],
        out_specs=pl.BlockSpec((None, m, tn), lambda i, j: (i, 0, j)),
        out_shape=jax.ShapeDtypeStruct((depth, m, n), F32),
        compiler_params=_params("parallel", "parallel"),
        name="ada",
    )(c_all, w_ada, b_ada.reshape(depth, 1, n))


def _row_spec(arr, tm):
    return pl.BlockSpec((None, tm, arr.shape[-1]), lambda b, l: (b, l, 0))


def _mod_spec(arr, tm):
    if arr.shape[1] == 1:
        return pl.BlockSpec((None, 1, arr.shape[-1]), lambda b, l: (b, 0, 0))
    return pl.BlockSpec((None, tm, arr.shape[-1]), lambda b, l: (b, l, 0))


def _full_spec(arr):
    nd = arr.ndim
    return pl.BlockSpec(arr.shape, lambda *_: (0,) * nd)


def _ffn_kernel(x_ref, g_ref, sh_ref, sc_ref, gt_ref, w1_ref, w3_ref, w2_ref, *rest, tf, n_mix):
    o_ref = rest[-1]
    x = x_ref[...]
    if n_mix:
        gm_ref = rest[0]
        a_refs = rest[1:1 + n_mix]
        wo_refs = rest[1 + n_mix:1 + 2 * n_mix]
        mix = _dot(a_refs[0][...].astype(BF16), wo_refs[0][...])
        for a_ref, wo_ref in zip(a_refs[1:], wo_refs[1:]):
            mix = mix + _dot(a_ref[...].astype(BF16), wo_ref[...])
        x = x + gm_ref[...] * mix
    h = _modulate(x, g_ref[...], sc_ref[...], sh_ref[...]).astype(BF16)
    acc = None
    for c in range(w1_ref.shape[-1] // tf):
        cs = slice(c * tf, (c + 1) * tf)
        t = (_silu(_dot(h, w1_ref[:, cs])) * _dot(h, w3_ref[:, cs])).astype(BF16)
        part = _dot(t, w2_ref[cs, :])
        acc = part if acc is None else acc + part
    o_ref[...] = x + 0.5 * gt_ref[...] * acc


def ffn_sublayer(x, g, shift, scale, gate, w1, w3, w2, li, half, tm, tf, mix=None):
    nb, r, d = x.shape
    f = w1.shape[-1]

    def resident(shape, idx):
        return pl.BlockSpec(shape, lambda b, l: idx, pipeline_mode=pl.Buffered(1))

    in_specs = [_row_spec(x, tm), pl.BlockSpec((1, d), lambda b, l: (0, 0)),
                _mod_spec(shift, tm), _mod_spec(scale, tm), _mod_spec(gate, tm),
                resident((None, None, d, f), (li, half, 0, 0)), resident((None, None, d, f), (li, half, 0, 0)),
                resident((None, None, f, d), (li, half, 0, 0))]
    args = [x, g.reshape(1, d), shift, scale, gate, w1, w3, w2]
    n_mix = 0
    if mix is not None:
        gate_mix, acts, spec_fns, wos = mix
        n_mix = len(acts)
        in_specs += [_mod_spec(gate_mix, tm)] + [fn(tm) for fn in spec_fns]
        in_specs += [resident(w.shape, (0,) * w.ndim) for w in wos]
        args += [gate_mix] + list(acts) + list(wos)
    return pl.pallas_call(
        functools.partial(_ffn_kernel, tf=tf, n_mix=n_mix),
        grid=(nb, r // tm),
        in_specs=in_specs,
        out_specs=_row_spec(x, tm),
        out_shape=jax.ShapeDtypeStruct(x.shape, F32),
        compiler_params=_params("parallel", "parallel"),
        name="ffn_mix" if n_mix else "ffn",
    )(*args)


def _aproj_kernel(x_ref, g_ref, sh_ref, sc_ref, gaq_ref, gbq_ref, gak_ref, gbk_ref, gq_ref, gkv_ref, gkn_ref,
                  win_ref, wq_ref, wkv_ref, ckv_ref, kpe_ref, u_ref, qf_ref, kf_ref, v_ref):
    h = _modulate(x_ref[...], g_ref[...], sc_ref[...], sh_ref[...]).astype(BF16)
    proj = _dot(h, win_ref[...])
    o1 = Q_LORA
    o2 = o1 + KV_LORA
    o3 = o2 + 512
    cq = proj[:, :o1]
    cqn = (cq * _rms(cq, Q_LORA) * gq_ref[...]).astype(BF16)
    ckv_raw = proj[:, o1:o2]
    ckv = ckv_raw * _rms(ckv_raw, KV_LORA) * gkv_ref[...]
    ckv_ref[...] = ckv
    u_ref[...] = proj[:, o2:o3]
    ka = proj[:, o3:o3 + LANE]
    kb = proj[:, o3 + LANE:o3 + 2 * LANE]
    kpe_blk = _rms(ka, MLA_ROPE) * (ka * gak_ref[...] + kb * gbk_ref[...])
    kpe_ref[...] = kpe_blk[:, MLA_NOPE:MLA_NOPE + MLA_ROPE]

    qraw = _dot(cqn, wq_ref[...])
    kv = _dot(ckv.astype(BF16), wkv_ref[...])
    lane = lax.broadcasted_iota(jnp.int32, (1, LANE), 1)
    is_n = lane < MLA_NOPE
    is_p = jnp.logical_and(lane >= MLA_NOPE, lane < MLA_NOPE + MLA_ROPE)
    gaq = gaq_ref[...]
    gbq = gbq_ref[...]
    gkn = gkn_ref[...]
    hw = MLA_HEADS * LANE
    for hh in range(MLA_HEADS):
        sl = slice(hh * LANE, (hh + 1) * LANE)
        a = qraw[:, sl]
        b = qraw[:, hw + hh * LANE:hw + (hh + 1) * LANE]
        sq = a * a
        msn = jnp.sum(jnp.where(is_n, sq, 0.0), axis=-1, keepdims=True) * (1.0 / MLA_NOPE)
        msp = jnp.sum(jnp.where(is_p, sq, 0.0), axis=-1, keepdims=True) * (1.0 / MLA_ROPE)
        r = jnp.where(is_n, lax.rsqrt(msn + RMS_EPS), lax.rsqrt(msp + RMS_EPS))
        qf_ref[:, sl] = (r * (a * gaq + b * gbq)).astype(BF16)
        kk = kv[:, sl]
        kf_ref[:, sl] = (kk * _rms(kk, MLA_NOPE) * gkn + kpe_blk).astype(BF16)
    v_ref[...] = kv[:, hw:].astype(BF16)


def mla_s5_project(x, g, shift, scale, tabs, gq, gkv, gkn, win, wq, wkv, tm):
    nb, r, d = x.shape
    gaq, gbq, gak, gbk = tabs
    tab_spec = pl.BlockSpec((tm, LANE), lambda b, l: (l, 0))
    hw = MLA_HEADS * LANE
    hv = MLA_HEADS * MLA_V
    u_w = win.shape[1] - Q_LORA - KV_LORA - 2 * LANE
    out_shape = [
        jax.ShapeDtypeStruct((nb, r, KV_LORA), F32),
        jax.ShapeDtypeStruct((nb, r, MLA_ROPE), F32),
        jax.ShapeDtypeStruct((r, nb * u_w), F32),
        jax.ShapeDtypeStruct((nb, r, hw), BF16),
        jax.ShapeDtypeStruct((nb, r, hw), BF16),
        jax.ShapeDtypeStruct((nb, r, hv), BF16),
    ]

    def rs(c):
        return pl.BlockSpec((None, tm, c), lambda b, l: (b, l, 0))

    out_specs = [rs(KV_LORA), rs(MLA_ROPE), pl.BlockSpec((tm, u_w), lambda b, l: (l, b)), rs(hw), rs(hw), rs(hv)]
    return pl.pallas_call(
        _aproj_kernel,
        grid=(nb, r // tm),
        in_specs=[_row_spec(x, tm), _full_spec(g), _mod_spec(shift, tm), _mod_spec(scale, tm),
                  tab_spec, tab_spec, tab_spec, tab_spec, _full_spec(gq), _full_spec(gkv), _full_spec(gkn),
                  _full_spec(win), _full_spec(wq), _full_spec(wkv)],
        out_specs=out_specs,
        out_shape=out_shape,
        compiler_params=_params("parallel", "parallel"),
        name="mla_s5_project",
    )(x, g, shift, scale, gaq, gbq, gak, gbk, gq, gkv, gkn, win, wq, wkv)


def _lane_fold(x, op):
    out = x[:, :LANE]
    for c in range(1, x.shape[1] // LANE):
        out = op(out, x[:, c * LANE:(c + 1) * LANE])
    return out


def _flash_kernel(*refs, hp, dv, has_bias, tq, tk):
    if has_bias:
        q_ref, k_ref, v_ref, fq_ref, fk_ref, o_ref, mr_scr, lp_scr, acc_scr = refs
    else:
        q_ref, k_ref, v_ref, o_ref, mr_scr, lp_scr, acc_scr = refs
    qi = pl.program_id(2)
    nd = tq // tk
    row = lax.broadcasted_iota(jnp.int32, (tq, tk), 0)
    col = lax.broadcasted_iota(jnp.int32, (tq, tk), 1)
    heads = range(hp)
    qs = [q_ref[:, i * LANE:(i + 1) * LANE] for i in heads]
    fqb = [jnp.broadcast_to(fq_ref[i], (tq, tk)) for i in heads] if has_bias else None

    def raw(i, j):
        return _dot_nt(qs[i], k_ref[pl.ds(pl.multiple_of(j * tk, tk), tk), i * LANE:(i + 1) * LANE])

    def vals(i, j):
        g = (i * dv) // LANE
        return v_ref[pl.ds(pl.multiple_of(j * tk, tk), tk), g * LANE:(g + 1) * LANE]

    def scores(i, j):
        s = raw(i, j)
        if has_bias:
            s = s + fqb[i] - fk_ref[i, j]
        return s

    s_d = [[jnp.where(d * tk + col <= row, scores(i, qi * nd + d), NEG_BIG) for d in range(nd)] for i in heads]
    for i in heads:
        mr = _lane_fold(s_d[i][0], jnp.maximum)
        for d in range(1, nd):
            mr = jnp.maximum(mr, _lane_fold(s_d[i][d], jnp.maximum))
        mr_scr[i] = mr

    def pass1(j, c):
        for i in heads:
            mr_scr[i] = jnp.maximum(mr_scr[i], _lane_fold(scores(i, j), jnp.maximum))
        return c

    lax.fori_loop(0, qi * nd, pass1, 0)
    shift = []
    for i in heads:
        m = jnp.max(mr_scr[i], axis=-1, keepdims=True)
        acc = None
        lp = None
        for d in range(nd):
            p_d = jnp.exp2(s_d[i][d] - m)
            part = _dot(p_d.astype(BF16), vals(i, qi * nd + d))
            acc = part if acc is None else acc + part
            fold = _lane_fold(p_d, jnp.add)
            lp = fold if lp is None else lp + fold
        acc_scr[i] = acc
        lp_scr[i] = lp
        shift.append((fqb[i] - m) if has_bias else jnp.broadcast_to(m, (tq, tk)))

    def pass2(j, c):
        for i in heads:
            if has_bias:
                p = jnp.exp2(raw(i, j) + shift[i] - fk_ref[i, j])
            else:
                p = jnp.exp2(raw(i, j) - shift[i])
            acc_scr[i] += _dot(p.astype(BF16), vals(i, j))
            lp_scr[i] += _lane_fold(p, jnp.add)
        return c

    lax.fori_loop(0, qi * nd, pass2, 0)
    outs = [acc_scr[i] / jnp.sum(lp_scr[i], axis=-1, keepdims=True) for i in heads]
    lane = lax.broadcasted_iota(jnp.int32, (1, LANE), 1)
    per_group = LANE // dv
    groups = []
    for g in range(hp // per_group):
        out = outs[g * per_group]
        for t in range(1, per_group):
            out = jnp.where(lane >= t * dv, outs[g * per_group + t], out)
        groups.append(out)
    o_ref[...] = (groups[0] if len(groups) == 1 else jnp.concatenate(groups, axis=1)).astype(o_ref.dtype)


def flash_prompt(q, k, v, fq, fk, hp, dv, tq, tk):
    b, l, hw = q.shape
    nh = hw // LANE
    assert (hp * dv) % LANE == 0 and LANE % dv == 0 and l % tq == 0 and tq % tk == 0
    has_bias = fq is not None
    in_specs = [
        pl.BlockSpec((None, tq, hp * LANE), lambda bb, h, i: (bb, i, h)),
        pl.BlockSpec((None, l, hp * LANE), lambda bb, h, i: (bb, 0, h)),
        pl.BlockSpec((None, l, hp * dv), lambda bb, h, i: (bb, 0, h)),
    ]
    args = [q, k, v]
    if has_bias:
        in_specs += [
            pl.BlockSpec((None, hp, tq, 1), lambda bb, h, i: (bb, h, i, 0)),
            pl.BlockSpec((None, hp, l // tk, 1, tk), lambda bb, h, i: (bb, h, 0, 0, 0)),
        ]
        args += [fq, fk]
    return pl.pallas_call(
        functools.partial(_flash_kernel, hp=hp, dv=dv, has_bias=has_bias, tq=tq, tk=tk),
        grid=(b, nh // hp, l // tq),
        in_specs=in_specs,
        out_specs=pl.BlockSpec((None, tq, hp * dv), lambda bb, h, i: (bb, i, h)),
        out_shape=jax.ShapeDtypeStruct((b, l, nh * dv), BF16),
        scratch_shapes=[pltpu.VMEM((hp, tq, LANE), F32), pltpu.VMEM((hp, tq, LANE), F32),
                        pltpu.VMEM((hp, tq, LANE), F32)],
        compiler_params=_params("parallel", "parallel", "arbitrary"),
        name="flash_bias" if has_bias else "flash",
    )(*args)


def _gelu_tanh(x):
    c = math.sqrt(2.0 / math.pi)
    return x * (0.5 * (1.0 + jnp.tanh(c * (x + 0.044715 * (x * x * x)))))


def _s5_kernel(u_ref, h0r_ref, h0i_ref, ar_ref, ai_ref, bd_ref, bdlo_ref, cd_ref, d_ref, wg_ref, bg_ref,
               y_ref, xr_ref, xi_ref, sr, si, xr_scr, xi_scr, *, t_blk, bt, precise):
    i = pl.program_id(0)
    rows = t_blk * bt
    ns = xr_scr.shape[-1]
    nh = ns // 2
    ch = u_ref.shape[-1] // 2

    @pl.when(i == 0)
    def _():
        xr_scr[...] = h0r_ref[...]
        xi_scr[...] = h0i_ref[...]

    u = u_ref[...].reshape(rows, 2 * ch)
    ub = u.astype(BF16)
    for hf in range(2):
        uh = ub[:, hf * ch:(hf + 1) * ch]
        bu = _dot(uh, bd_ref[hf])
        if precise:
            ulo = (u[:, hf * ch:(hf + 1) * ch] - uh.astype(F32)).astype(BF16)
            bu = bu + _dot(ulo, bd_ref[hf]) + _dot(uh, bdlo_ref[hf])
        sr[:, hf * nh:(hf + 1) * nh] = bu[:, :nh]
        si[:, hf * nh:(hf + 1) * nh] = bu[:, nh:]

    for hf in range(2):
        cs = slice(hf * nh, (hf + 1) * nh)
        ar = jnp.broadcast_to(ar_ref[:, cs], (bt, nh))
        ai = jnp.broadcast_to(ai_ref[:, cs], (bt, nh))

        def step(t, carry, cs=cs, ar=ar, ai=ai):
            xr, xi = carry
            o = pl.multiple_of(t * bt, bt)
            nr = ar * xr - ai * xi + sr[pl.ds(o, bt), cs]
            ni = ar * xi + ai * xr + si[pl.ds(o, bt), cs]
            sr[pl.ds(o, bt), cs] = nr
            si[pl.ds(o, bt), cs] = ni
            return nr, ni

        xr, xi = lax.fori_loop(0, t_blk, step, (xr_scr[:, cs], xi_scr[:, cs]))
        xr_scr[:, cs] = xr
        xi_scr[:, cs] = xi

    ys = []
    for hf in range(2):
        cs = slice(hf * nh, (hf + 1) * nh)
        ys.append(_dot(sr[:, cs].astype(BF16), cd_ref[hf, :nh, :]) + _dot(si[:, cs].astype(BF16), cd_ref[hf, nh:, :]))
    y = jnp.concatenate(ys, axis=1)
    y = _gelu_tanh(y + d_ref[...] * u)
    y = y * jax.nn.sigmoid(_dot(y.astype(BF16), wg_ref[...]) + bg_ref[...])
    y_ref[...] = y.reshape(t_blk, bt, 2 * ch)

    @pl.when(i == pl.num_programs(0) - 1)
    def _():
        xr_ref[...] = xr_scr[...]
        xi_ref[...] = xi_scr[...]


def s5_mix(u3, h0r, h0i, prm, t_blk, precise):
    t, bt, c = u3.shape
    ns = h0r.shape[-1]
    rows = t_blk * bt
    ar, ai, bd, bdlo, cd, d, wg, bg = prm
    blk = pl.BlockSpec((t_blk, bt, c), lambda i: (i, 0, 0))
    ins = [u3, h0r, h0i, ar, ai, bd, bdlo, cd, d, wg, bg]
    return pl.pallas_call(
        functools.partial(_s5_kernel, t_blk=t_blk, bt=bt, precise=precise),
        grid=(t // t_blk,),
        in_specs=[blk] + [_full_spec(a) for a in ins[1:]],
        out_specs=[blk, _full_spec(h0r), _full_spec(h0i)],
        out_shape=[jax.ShapeDtypeStruct((t, bt, c), F32), jax.ShapeDtypeStruct(h0r.shape, F32),
                   jax.ShapeDtypeStruct(h0i.shape, F32)],
        scratch_shapes=[pltpu.VMEM((rows, ns), F32), pltpu.VMEM((rows, ns), F32),
                        pltpu.VMEM((bt, ns), F32), pltpu.VMEM((bt, ns), F32)],
        compiler_params=_params("arbitrary"),
        name="s5",
    )(*ins)


def _foxproj_kernel(x_ref, g_ref, sh_ref, sc_ref, gq_ref, gk_ref, bf_ref, w_ref,
                    k_ref, v_ref, lf_ref, qb_ref, kb_ref, vb_ref, fc_ref, carry_scr, *, seg, tm):
    l_idx = pl.program_id(1)
    h = _modulate(x_ref[...], g_ref[...], sc_ref[...], sh_ref[...]).astype(BF16)
    proj = _dot(h, w_ref[...])
    hw = FOX_HEADS * FOX_HD
    gq = gq_ref[...]
    gk = gk_ref[...]
    for hh in range(FOX_HEADS):
        sl = slice(hh * FOX_HD, (hh + 1) * FOX_HD)
        qh = proj[:, sl]
        qb_ref[:, sl] = (qh * _rms(qh, FOX_HD) * gq).astype(BF16)
        kh = proj[:, hw + hh * FOX_HD:hw + (hh + 1) * FOX_HD]
        kn = kh * _rms(kh, FOX_HD) * gk
        k_ref[:, sl] = kn
        kb_ref[:, sl] = kn.astype(BF16)
    vv = proj[:, 2 * hw:3 * hw]
    v_ref[...] = vv
    vb_ref[...] = vv.astype(BF16)
    z = proj[:, 3 * hw:] + bf_ref[...]
    lf = jnp.minimum(z, 0.0) - jnp.log1p(jnp.exp(-jnp.abs(z)))
    lf_ref[...] = lf[:, :FOX_HEADS]
    row = lax.broadcasted_iota(jnp.int32, (tm, tm), 0)
    col = lax.broadcasted_iota(jnp.int32, (tm, tm), 1)
    keep = col <= row
    if seg < tm:
        keep = jnp.logical_and(keep, (col // seg) == (row // seg))
    tri = jnp.where(keep, 1.0, 0.0).astype(BF16)
    hi, mid, lo = _split3(lf)
    cs = _dot(tri, hi) + _dot(tri, mid) + _dot(tri, lo)
    if seg > tm:
        @pl.when(l_idx == 0)
        def _():
            carry_scr[...] = jnp.zeros_like(carry_scr)

        cs = cs + carry_scr[...]
        carry_scr[...] = cs[tm - 1:tm, :]
    fc_ref[...] = cs


def fox_project(x, g, shift, scale, gq, gk, bf, w, seg, tm):
    nb, r, d = x.shape
    hw = FOX_HEADS * FOX_HD

    def rs(c):
        return pl.BlockSpec((None, tm, c), lambda b, l: (b, l, 0))

    out_shape = [
        jax.ShapeDtypeStruct((nb, r, hw), F32), jax.ShapeDtypeStruct((nb, r, hw), F32),
        jax.ShapeDtypeStruct((nb, r, FOX_HEADS), F32),
        jax.ShapeDtypeStruct((nb, r, hw), BF16), jax.ShapeDtypeStruct((nb, r, hw), BF16),
        jax.ShapeDtypeStruct((nb, r, hw), BF16), jax.ShapeDtypeStruct((nb, r, LANE), F32),
    ]
    out_specs = [rs(hw), rs(hw), rs(FOX_HEADS), rs(hw), rs(hw), rs(hw), rs(LANE)]
    return pl.pallas_call(
        functools.partial(_foxproj_kernel, seg=seg, tm=tm),
        grid=(nb, r // tm),
        in_specs=[_row_spec(x, tm), _full_spec(g), _mod_spec(shift, tm), _mod_spec(scale, tm),
                  _full_spec(gq), _full_spec(gk), _full_spec(bf), _full_spec(w)],
        out_specs=out_specs,
        out_shape=out_shape,
        scratch_shapes=[pltpu.VMEM((1, LANE), F32)],
        compiler_params=_params("parallel", "arbitrary"),
        name="fox_project",
    )(x, g, shift, scale, gq, gk, bf, w)


def _page_dma(pools, bufs, sems, pps, page_of):
    def each(b, s, slot, act):
        def body(i, c):
            pg = page_of(b, s, i)
            for n, (pool, buf) in enumerate(zip(pools, bufs)):
                act(pltpu.make_async_copy(pool.at[pg], buf.at[slot, i], sems.at[slot, n]))
            return c

        lax.fori_loop(0, pps, body, 0)

    return (lambda b, s, slot: each(b, s, slot, lambda cp: cp.start()),
            lambda b, s, slot: each(b, s, slot, lambda cp: cp.wait()))


def _resident_slot(start, wait):
    b, s = pl.program_id(0), pl.program_id(1)
    n_steps = pl.num_programs(1)
    t = b * n_steps + s
    slot = t % 2

    @pl.when(t == 0)
    def _():
        start(b, s, slot)

    last = s + 1 == n_steps

    @pl.when(t + 1 < pl.num_programs(0) * n_steps)
    def _():
        start(jnp.where(last, b + 1, b), jnp.where(last, 0, s + 1), 1 - slot)

    wait(b, s, slot)
    return slot


def _mla_sample_kernel(pt_ref, q_ref, cn_ref, kn_ref, wfull_ref, wukt_ref, wuv_ref, ckv_hbm, kpe_hbm,
                       o_ref, ckv_buf, kpe_buf, sems, qabs_scr, qpe_scr, ckv_scr, m_scr, l_scr, acc_scr,
                       *, pps, page_base):
    s_idx = pl.program_id(1)
    nq = q_ref.shape[0]
    nrow = MLA_HEADS * nq
    start, wait = _page_dma((ckv_hbm, kpe_hbm), (ckv_buf, kpe_buf), sems, pps,
                            lambda b, s, i: page_base + pt_ref[b, s * pps + i])
    slot = _resident_slot(start, wait)

    @pl.when(s_idx == 0)
    def _():
        m_scr[...] = jnp.full_like(m_scr, NEG_BIG)
        l_scr[...] = jnp.zeros_like(l_scr)
        acc_scr[...] = jnp.zeros_like(acc_scr)
        for hh in range(MLA_HEADS):
            res = _dot(q_ref[:, hh * LANE:(hh + 1) * LANE], wfull_ref[hh])
            qabs_scr[hh * nq:(hh + 1) * nq, :] = res[:, :KV_LORA]
            qpe_scr[hh * nq:(hh + 1) * nq, :] = res[:, KV_LORA:]

    wstack = jnp.concatenate([wukt_ref[...], qabs_scr[...].astype(BF16)], axis=0)
    qpe = qpe_scr[...].astype(BF16)
    nk = MLA_HEADS * MLA_NOPE

    def scores(cb, kpe_t):
        a = _dot_nt(wstack, cb)
        spe = _dot(qpe[:, :MLA_ROPE], kpe_t)
        rows = []
        for hh in range(MLA_HEADS):
            kr = a[hh * MLA_NOPE:(hh + 1) * MLA_NOPE, :]
            ms = jnp.sum(kr * kr, axis=0, keepdims=True) * (1.0 / MLA_NOPE)
            rows.append(a[nk + hh * nq:nk + (hh + 1) * nq, :] * lax.rsqrt(ms + RMS_EPS))
        return jnp.concatenate(rows, axis=0) + spe

    def online(s, vals):
        m_prev = m_scr[...]
        m_new = jnp.maximum(m_prev, jnp.max(s, axis=-1, keepdims=True))
        alpha = jnp.exp2(m_prev - m_new)
        p = jnp.exp2(s - m_new)
        l_scr[...] = alpha * l_scr[...] + jnp.sum(p, axis=-1, keepdims=True)
        acc_scr[...] = alpha * acc_scr[...] + _dot(p.astype(BF16), vals)
        m_scr[...] = m_new

    for i in range(pps):
        ckv_scr[i * PAGE:(i + 1) * PAGE, :] = ckv_buf[slot, i].astype(BF16)
    cb = ckv_scr[...]
    kt = jnp.concatenate([kpe_buf[slot, i] for i in range(pps)], axis=1)
    online(scores(cb, kt.astype(BF16)), cb)

    @pl.when(s_idx == pl.num_programs(1) - 1)
    def _():
        pad = PAGE - nq
        cb = jnp.concatenate([cn_ref[...], jnp.zeros((pad, KV_LORA), F32)], axis=0).astype(BF16)
        kb = kn_ref[...].astype(BF16)
        s = scores(cb, kb)
        row = lax.broadcasted_iota(jnp.int32, (nrow, PAGE), 0)
        col = lax.broadcasted_iota(jnp.int32, (nrow, PAGE), 1)
        s = jnp.where(col <= row % nq, s, NEG_BIG)
        online(s, cb)
        o_lat = (acc_scr[...] / l_scr[...]).astype(BF16)
        out = _dot(o_lat[0:nq, :], wuv_ref[0])
        for hh in range(1, MLA_HEADS):
            out = out + _dot(o_lat[hh * nq:(hh + 1) * nq, :], wuv_ref[hh])
        o_ref[...] = out


def mla_attend_sample(qf, ckv_new, kpe_new, pool_ckv, pool_kpe, page_base, page_table, wfull, wukt, wuv, pps):
    db, nq, hw = qf.shape
    n_pages = page_table.shape[1]
    assert n_pages % pps == 0
    nrow = MLA_HEADS * nq

    def fixed(arr):
        nd = arr.ndim
        return pl.BlockSpec(arr.shape, lambda b, s, pt: (0,) * nd)

    def per_seq(arr):
        return pl.BlockSpec((None,) + arr.shape[1:], lambda b, s, pt: (b, 0, 0))

    hbm = pl.BlockSpec(memory_space=pl.ANY)
    in_specs = [per_seq(qf), per_seq(ckv_new), per_seq(kpe_new), fixed(wfull), fixed(wukt), fixed(wuv), hbm, hbm]
    out_w = MLA_HEADS * MLA_V
    grid_spec = pltpu.PrefetchScalarGridSpec(
        num_scalar_prefetch=1,
        grid=(db, n_pages // pps),
        in_specs=in_specs,
        out_specs=pl.BlockSpec((None, nq, out_w), lambda b, s, pt: (b, 0, 0)),
        scratch_shapes=[pltpu.VMEM((2, pps) + pool_ckv.shape[1:], F32), pltpu.VMEM((2, pps) + pool_kpe.shape[1:], F32),
                        pltpu.SemaphoreType.DMA((2, 2)),
                        pltpu.VMEM((nrow, KV_LORA), F32), pltpu.VMEM((nrow, LANE), F32),
                        pltpu.VMEM((pps * PAGE, KV_LORA), BF16),
                        pltpu.VMEM((nrow, 1), F32), pltpu.VMEM((nrow, 1), F32), pltpu.VMEM((nrow, KV_LORA), F32)],
    )
    return pl.pallas_call(
        functools.partial(_mla_sample_kernel, pps=pps, page_base=page_base),
        grid_spec=grid_spec,
        out_shape=jax.ShapeDtypeStruct((db, nq, out_w), F32),
        compiler_params=_params("arbitrary", "arbitrary"),
        name="mla_sample",
    )(page_table, qf, ckv_new, kpe_new, wfull, wukt, wuv, pool_ckv, pool_kpe)


def _suffix_flat(x):
    lane = lax.broadcasted_iota(jnp.int32, x.shape, 1)
    row = lax.broadcasted_iota(jnp.int32, x.shape, 0)
    y = x
    for s in (8, 16, 32, 64):
        y = y + jnp.where(lane + s < LANE, pltpu.roll(y, LANE - s, 1), 0.0)
    z = jnp.where(lane < FOX_HEADS, y, 0.0)
    for s in (8, 16, 32, 64):
        z = z + pltpu.roll(z, s, 1)
    v = z
    for s in (1, 2, 4):
        v = v + jnp.where(row + s < 8, pltpu.roll(v, 8 - s, 0), 0.0)
    return (y - x) + (v - z), v[0:1, :]


def _fox_sample_kernel(pt_ref, q_ref, kn_ref, vn_ref, fq_ref, fkn_ref, k_hbm, v_hbm, lf_hbm,
                       o_ref, k_buf, v_buf, lf_buf, sems, carry_scr, m_scr, l_scr, acc_scr, *, pps, page_base):
    s_idx = pl.program_id(1)
    nrow = q_ref.shape[0]
    prow = PAGE * FOX_HEADS
    n_pages = pt_ref.shape[1]
    start, wait = _page_dma((k_hbm, v_hbm, lf_hbm), (k_buf, v_buf, lf_buf), sems, pps,
                            lambda b, s, i: page_base + pt_ref[b, n_pages - 1 - (s * pps + i)])
    slot = _resident_slot(start, wait)

    @pl.when(s_idx == 0)
    def _():
        m_scr[...] = jnp.full_like(m_scr, NEG_BIG)
        l_scr[...] = jnp.zeros_like(l_scr)
        acc_scr[...] = jnp.zeros_like(acc_scr)
        carry_scr[...] = jnp.zeros_like(carry_scr)

    q = q_ref[...]
    row = lax.broadcasted_iota(jnp.int32, (nrow, LANE), 0)
    lane = lax.broadcasted_iota(jnp.int32, (nrow, LANE), 1)
    same_head = (row % FOX_HEADS) == (lane % FOX_HEADS)
    fqm = fq_ref[...] + jnp.where(same_head, 0.0, NEG_BIG)

    def online(state, s, vals):
        m_prev, l_prev, acc = state
        m_new = jnp.maximum(m_prev, jnp.max(s, axis=-1, keepdims=True))
        alpha = jnp.exp2(m_prev - m_new)
        p = jnp.exp2(s - m_new)
        l_new = alpha * l_prev + jnp.sum(p, axis=-1, keepdims=True)
        pb = p.astype(BF16)
        acc = alpha * acc
        o = 0
        for v in vals:
            acc = acc + _dot(pb[:, o:o + v.shape[0]], v)
            o += v.shape[0]
        return m_new, l_new, acc

    carry = carry_scr[...]
    state = (m_scr[...], l_scr[...], acc_scr[...])
    pending = None
    for g in range(pps // FOX_SUB):
        parts = []
        vals = []
        for i in range(g * FOX_SUB, (g + 1) * FOX_SUB):
            r, tot = _suffix_flat(lf_buf[slot, i])
            r = (r + carry) * LOG2E
            carry = carry + tot
            raw = _dot_nt(q, k_buf[slot, i].astype(BF16))
            vals.append(v_buf[slot, i].astype(BF16))
            for c in range(prow // LANE):
                parts.append(raw[:, c * LANE:(c + 1) * LANE] + (r[c:c + 1, :] + fqm))
        if pending is not None:
            state = online(state, *pending)
        pending = (jnp.concatenate(parts, axis=1), vals)
    state = online(state, *pending)
    carry_scr[...] = carry
    m_scr[...], l_scr[...], acc_scr[...] = state

    @pl.when(s_idx == pl.num_programs(1) - 1)
    def _():
        pad = LANE - kn_ref.shape[0]
        kb = jnp.concatenate([kn_ref[...], jnp.zeros((pad, FOX_HD), F32)], axis=0).astype(BF16)
        vb = jnp.concatenate([vn_ref[...], jnp.zeros((pad, FOX_HD), F32)], axis=0).astype(BF16)
        s = _dot_nt(q, kb) + fqm - fkn_ref[...]
        s = jnp.where(lane // FOX_HEADS <= row // FOX_HEADS, s, NEG_BIG)
        _, l_fin, acc_fin = online((m_scr[...], l_scr[...], acc_scr[...]), s, [vb])
        o_ref[...] = acc_fin / l_fin


def fox_attend_sample(qb, k_new, v_new, fq_col, fk_row, pool_k, pool_v, pool_lf, page_base, page_table, pps):
    db, nrow, hd = qb.shape
    n_pages = page_table.shape[1]
    assert n_pages % pps == 0

    def per_seq(arr):
        return pl.BlockSpec((None,) + arr.shape[1:], lambda b, s, pt: (b, 0, 0))

    hbm = pl.BlockSpec(memory_space=pl.ANY)
    in_specs = [per_seq(qb), per_seq(k_new), per_seq(v_new), per_seq(fq_col), per_seq(fk_row), hbm, hbm, hbm]
    grid_spec = pltpu.PrefetchScalarGridSpec(
        num_scalar_prefetch=1,
        grid=(db, n_pages // pps),
        in_specs=in_specs,
        out_specs=pl.BlockSpec((None, nrow, hd), lambda b, s, pt: (b, 0, 0)),
        scratch_shapes=[pltpu.VMEM((2, pps) + pool_k.shape[1:], F32), pltpu.VMEM((2, pps) + pool_v.shape[1:], F32),
                        pltpu.VMEM((2, pps) + pool_lf.shape[1:], F32), pltpu.SemaphoreType.DMA((2, 3)),
                        pltpu.VMEM((1, LANE), F32),
                        pltpu.VMEM((nrow, 1), F32), pltpu.VMEM((nrow, 1), F32), pltpu.VMEM((nrow, hd), F32)],
    )
    return pl.pallas_call(
        functools.partial(_fox_sample_kernel, pps=pps, page_base=page_base),
        grid_spec=grid_spec,
        out_shape=jax.ShapeDtypeStruct((db, nrow, hd), F32),
        compiler_params=_params("arbitrary", "arbitrary"),
        name="fox_sample",
    )(page_table, qb, k_new, v_new, fq_col, fk_row, pool_k, pool_v, pool_lf)


def _rope_perm():
    half = MLA_ROPE // 2
    idx = jnp.arange(MLA_ROPE)
    return jnp.where(idx < half, idx + half, idx - half), jnp.where(idx < half, -1.0, 1.0).astype(F32)


def _pad_rope_block(w):
    z = jnp.zeros(w.shape[:-1] + (MLA_NOPE,), w.dtype)
    z2 = jnp.zeros(w.shape[:-1] + (LANE - MLA_NOPE - MLA_ROPE,), w.dtype)
    return jnp.concatenate([z, w, z2], axis=-1)


def _mla_weights(a_w_in, a_w_uq, a_w_ukv):
    perm, _ = _rope_perm()
    o1 = Q_LORA
    o2 = o1 + KV_LORA
    o3 = o2 + MLA_ROPE
    w_kpe = a_w_in[:, o2:o3]
    win = jnp.concatenate([a_w_in[:, :o2], a_w_in[:, o3:], _pad_rope_block(w_kpe), _pad_rope_block(w_kpe[:, perm])],
                          axis=1).astype(BF16)
    wq3 = a_w_uq.reshape(Q_LORA, MLA_HEADS, MLA_NOPE + MLA_ROPE)
    zpad = jnp.zeros((Q_LORA, MLA_HEADS, LANE - MLA_NOPE - MLA_ROPE), F32)
    wq_a = jnp.concatenate([wq3, zpad], axis=-1).reshape(Q_LORA, MLA_HEADS * LANE)
    wq_b = _pad_rope_block(wq3[..., MLA_NOPE:][..., perm]).reshape(Q_LORA, MLA_HEADS * LANE)
    wq = jnp.concatenate([wq_a, wq_b], axis=1).astype(BF16)
    wkv3 = a_w_ukv.reshape(KV_LORA, MLA_HEADS, MLA_NOPE + MLA_V)
    wk = jnp.concatenate([wkv3[..., :MLA_NOPE], jnp.zeros((KV_LORA, MLA_HEADS, LANE - MLA_NOPE), F32)], axis=-1)
    wv = wkv3[..., MLA_NOPE:]
    wkv = jnp.concatenate([wk.reshape(KV_LORA, -1), wv.reshape(KV_LORA, -1)], axis=1).astype(BF16)
    return win, wq, wkv, wkv3


def _rope_tables(pos, qn_g, qr_g, kr_g):
    half = MLA_ROPE // 2
    perm, sign = _rope_perm()
    freq = ROPE_THETA ** (-jnp.arange(half, dtype=F32) / half)
    ang = pos.astype(F32)[:, None] * freq[None, :]
    cos = jnp.concatenate([jnp.cos(ang), jnp.cos(ang)], axis=1)
    sin = jnp.concatenate([jnp.sin(ang), jnp.sin(ang)], axis=1)
    n = pos.shape[0]
    zn = jnp.zeros((n, MLA_NOPE), F32)
    zp = jnp.zeros((n, LANE - MLA_NOPE - MLA_ROPE), F32)
    qs = MLA_SCALE * LOG2E
    gaq = jnp.concatenate([jnp.broadcast_to(qn_g[None, :], (n, MLA_NOPE)), qr_g[None, :] * cos, zp], axis=1) * qs
    gbq = jnp.concatenate([zn, (sign * qr_g[perm])[None, :] * sin, zp], axis=1) * qs
    gak = jnp.concatenate([zn, kr_g[None, :] * cos, zp], axis=1)
    gbk = jnp.concatenate([zn, (sign * kr_g[perm])[None, :] * sin, zp], axis=1)
    return gaq, gbq, gak, gbk


def _s5_params(a_re, a_im, log_dt, b_re, b_im, c_re, c_im, d_skip, w_glu, b_glu):
    g, n = a_re.shape
    dt = jnp.exp(log_dt)[:, None]
    mag = jnp.exp(dt * a_re)
    abar_re, abar_im = mag * jnp.cos(dt * a_im), mag * jnp.sin(dt * a_im)
    den = a_re * a_re + a_im * a_im
    w_re = ((abar_re - 1) * a_re + abar_im * a_im) / den
    w_im = (abar_im * a_re - (abar_re - 1) * a_im) / den
    bbar_re = w_re[..., None] * b_re - w_im[..., None] * b_im
    bbar_im = w_re[..., None] * b_im + w_im[..., None] * b_re
    gh = g // 2
    eye = jnp.eye(gh, dtype=F32)

    def pack_b(bb):
        return jnp.einsum('gnc,gh->gchn', bb, eye).reshape(gh * S5_GROUP, gh * n)

    def pack_c(cc):
        return jnp.einsum('gcn,gh->gnhc', cc, eye).reshape(gh * n, gh * S5_GROUP)

    bd = jnp.stack([jnp.concatenate([pack_b(bbar_re[h * gh:(h + 1) * gh]), pack_b(bbar_im[h * gh:(h + 1) * gh])], axis=1)
                    for h in range(2)])
    cd = jnp.stack([jnp.concatenate([pack_c(c_re[h * gh:(h + 1) * gh]), -pack_c(c_im[h * gh:(h + 1) * gh])], axis=0)
                    for h in range(2)])
    bd_hi = bd.astype(BF16)
    bd_lo = (bd - bd_hi.astype(F32)).astype(BF16)
    return (abar_re.reshape(1, g * n), abar_im.reshape(1, g * n), bd_hi, bd_lo, cd.astype(BF16),
            d_skip.reshape(1, -1), w_glu.astype(BF16), b_glu.reshape(1, -1))


def kernel(x_prompt, x_sample, c_prompt, c_sample, cache_mla_ckv, cache_mla_kpe, state_s5_re, state_s5_im,
           cache_fox_k, cache_fox_v, cache_fox_logf, page_table, w_ada, b_ada, norm_g, ffn_w1, ffn_w3, ffn_w2,
           a_w_in, a_q_norm, a_kv_norm, a_w_uq, a_w_ukv, a_qn_norm, a_qr_norm, a_kn_norm, a_kr_norm,
           s5_a_re, s5_a_im, s5_log_dt, s5_b_re, s5_b_im, s5_c_re, s5_c_im, s5_d, s5_w_glu, s5_b_glu,
           a_w_out, c_w_in, c_b_f, c_q_norm, c_k_norm, c_w_out):
    B, L, D = x_prompt.shape
    DB, DS, _ = x_sample.shape
    RS = DB * DS
    n_pages = page_table.shape[1]
    n_past = n_pages * PAGE
    depth = w_ada.shape[0]

    tm_p = min(512, L)
    tm_ffn = min(1024, L)
    tf = 256
    tq = min(1024, L)
    tk = min(512, L)

    m_all = ada_modulation(jnp.concatenate([c_prompt, c_sample], axis=0), w_ada, b_ada)
    w1b, w3b, w2b = ffn_w1.astype(BF16), ffn_w3.astype(BF16), ffn_w2.astype(BF16)

    xp = x_prompt
    xs = x_sample.reshape(1, RS, D)
    outs_p = {}
    outs_s = {}
    for i in range(depth):
        mp = m_all[i, :B].reshape(B, 3, 3, 1, D)
        ms = jnp.repeat(m_all[i, B:].reshape(DB, 3, 3, D), DS, axis=0).reshape(1, RS, 3, 3, D)

        def mod_p(s, k):
            return mp[:, s, k]

        def mod_s(s, k):
            return ms[:, :, s, k]

        g = norm_g[i]
        xp = ffn_sublayer(xp, g[0], mod_p(0, 0), mod_p(0, 1), mod_p(0, 2), w1b, w3b, w2b, i, 0, tm_ffn, tf)
        xs = ffn_sublayer(xs, g[0], mod_s(0, 0), mod_s(0, 1), mod_s(0, 2), w1b, w3b, w2b, i, 0, RS, tf)
        j = i // 2
        g1 = g[1].reshape(1, D)
        if i % 2 == 0:
            win, wq, wkv, wkv3 = _mla_weights(a_w_in[j], a_w_uq[j], a_w_ukv[j])
            gq = a_q_norm[j].reshape(1, -1)
            gkv = a_kv_norm[j].reshape(1, -1)
            gkn = jnp.concatenate([a_kn_norm[j], jnp.zeros((LANE - MLA_NOPE,), F32)]).reshape(1, LANE)
            s5p = _s5_params(s5_a_re[j], s5_a_im[j], s5_log_dt[j], s5_b_re[j], s5_b_im[j], s5_c_re[j], s5_c_im[j],
                             s5_d[j], s5_w_glu[j], s5_b_glu[j])
            wo = a_w_out[j].astype(BF16)
            hv = MLA_HEADS * MLA_V
            wo_att, wo_ssm = wo[:hv], wo[hv:]
            cw = s5_d.shape[-1]
            ns = s5_a_re.shape[1] * s5_a_re.shape[2]

            tabs = _rope_tables(jnp.arange(L), a_qn_norm[j], a_qr_norm[j], a_kr_norm[j])
            ckv, kpe, u_tm, qf, kf, vb = mla_s5_project(xp, g1, mod_p(1, 0), mod_p(1, 1), tabs, gq, gkv, gkn,
                                                        win, wq, wkv, tm_p)
            att = flash_prompt(qf, kf, vb, None, None, 2, MLA_V, tq, tk)
            zeros = jnp.zeros((B, ns), F32)
            ssm, sr, si = s5_mix(u_tm.reshape(L, B, cw), zeros, zeros, s5p, min(64, L), False)
            ssm2 = ssm.reshape(L, B * cw)
            mix_p = (mod_p(1, 2), [att, ssm2],
                     [lambda tm, w=hv: pl.BlockSpec((None, tm, w), lambda b, l: (b, l, 0)),
                      lambda tm, w=cw: pl.BlockSpec((tm, w), lambda b, l: (l, b))],
                     [wo_att, wo_ssm])
            outs_p.setdefault('ckv', []).append(ckv)
            outs_p.setdefault('kpe', []).append(kpe)
            outs_p.setdefault('s5r', []).append(sr.reshape(B, -1, S5_STATE))
            outs_p.setdefault('s5i', []).append(si.reshape(B, -1, S5_STATE))

            pos_s = n_past + jnp.tile(jnp.arange(DS), DB)
            tabs = _rope_tables(pos_s, a_qn_norm[j], a_qr_norm[j], a_kr_norm[j])
            ckv, kpe, u_s, qf, kf, vb = mla_s5_project(xs, g1, mod_s(1, 0), mod_s(1, 1), tabs, gq, gkv, gkn,
                                                       win, wq, wkv, RS)
            wuk = wkv3[..., :MLA_NOPE] * a_kn_norm[j][None, None, :]
            wabs = jnp.transpose(wuk, (1, 2, 0))
            top = jnp.concatenate([wabs, jnp.zeros((MLA_HEADS, MLA_NOPE, LANE), F32)], axis=-1)
            eye_blk = jnp.concatenate([jnp.zeros((MLA_ROPE, KV_LORA), F32), jnp.eye(MLA_ROPE, LANE, dtype=F32)], axis=-1)
            mid = jnp.broadcast_to(eye_blk[None], (MLA_HEADS, MLA_ROPE, KV_LORA + LANE))
            bot = jnp.zeros((MLA_HEADS, LANE - MLA_NOPE - MLA_ROPE, KV_LORA + LANE), F32)
            wfull = jnp.concatenate([top, mid, bot], axis=1).astype(BF16)
            wukt = jnp.transpose(wkv3[..., :MLA_NOPE], (1, 2, 0)).reshape(MLA_HEADS * MLA_NOPE, KV_LORA).astype(BF16)
            wuv = jnp.einsum('khd,hg->hkgd', wkv3[..., MLA_NOPE:], jnp.eye(MLA_HEADS, dtype=F32))
            wuv = wuv.reshape(MLA_HEADS, KV_LORA, hv).astype(BF16)
            kpe_t = jnp.transpose(kpe.reshape(DB, DS, MLA_ROPE), (0, 2, 1))
            kpe_t = jnp.concatenate([kpe_t, jnp.zeros((DB, MLA_ROPE, PAGE - DS), F32)], axis=-1)
            pool_ckv = cache_mla_ckv.reshape(-1, PAGE, KV_LORA)
            pool_kpe = jnp.swapaxes(cache_mla_kpe, 2, 3).reshape(-1, MLA_ROPE, PAGE)
            att_s = mla_attend_sample(qf.reshape(DB, DS, -1), ckv.reshape(DB, DS, -1), kpe_t,
                                      pool_ckv, pool_kpe, j * cache_mla_ckv.shape[1], page_table,
                                      wfull, wukt, wuv, 16)
            u3 = jnp.transpose(u_s.reshape(DB, DS, cw), (1, 0, 2))
            ssm, sr, si = s5_mix(u3, state_s5_re[j].reshape(DB, ns), state_s5_im[j].reshape(DB, ns), s5p, DS, True)
            ssm_s = jnp.transpose(ssm, (1, 0, 2)).reshape(1, RS, cw)
            mix_s = (mod_s(1, 2), [att_s.reshape(1, RS, hv), ssm_s],
                     [lambda tm, w=hv: pl.BlockSpec((None, tm, w), lambda b, l: (b, l, 0)),
                      lambda tm, w=cw: pl.BlockSpec((None, tm, w), lambda b, l: (b, l, 0))],
                     [wo_att, wo_ssm])
            outs_s.setdefault('ckv', []).append(ckv.reshape(DB, DS, -1))
            outs_s.setdefault('kpe', []).append(kpe.reshape(DB, DS, -1))
            outs_s.setdefault('s5r', []).append(sr.reshape(DB, -1, S5_STATE))
            outs_s.setdefault('s5i', []).append(si.reshape(DB, -1, S5_STATE))
        else:
            hw = FOX_HEADS * FOX_HD
            wf = jnp.concatenate([c_w_in[j], jnp.zeros((D, LANE - FOX_HEADS), F32)], axis=1)
            wf = wf.astype(BF16)
            gqf = (c_q_norm[j] * (FOX_SCALE * LOG2E)).reshape(1, FOX_HD)
            gkf = c_k_norm[j].reshape(1, FOX_HD)
            bf = jnp.concatenate([c_b_f[j], jnp.zeros((LANE - FOX_HEADS,), F32)]).reshape(1, LANE)
            wo = c_w_out[j].astype(BF16)

            k32, v32, lf, qb, kb, vb, fc = fox_project(xp, g1, mod_p(1, 0), mod_p(1, 1), gqf, gkf, bf, wf, L, tm_p)
            fcs = jnp.transpose(fc[:, :, :FOX_HEADS], (0, 2, 1)) * LOG2E
            fq = fcs[..., None]
            fk = fcs.reshape(B, FOX_HEADS, L // tk, 1, tk)
            o = flash_prompt(qb, kb, vb, fq, fk, 2, FOX_HD, tq, tk)
            mix_p = (mod_p(1, 2), [o], [lambda tm, w=hw: pl.BlockSpec((None, tm, w), lambda b, l: (b, l, 0))], [wo])
            outs_p.setdefault('fk', []).append(k32.reshape(B, L, FOX_HEADS, FOX_HD))
            outs_p.setdefault('fv', []).append(v32.reshape(B, L, FOX_HEADS, FOX_HD))
            outs_p.setdefault('flf', []).append(lf)

            k32, v32, lf, qb, kb, vb, fc = fox_project(xs, g1, mod_s(1, 0), mod_s(1, 1), gqf, gkf, bf, wf, DS, RS)
            nr = DS * FOX_HEADS
            f_new = fc[0, :, :FOX_HEADS].reshape(DB, nr) * LOG2E
            fq_col = f_new[..., None]
            fk_row = jnp.concatenate([f_new, jnp.zeros((DB, LANE - nr), F32)], axis=-1)[:, None, :]
            n_pool = cache_fox_k.shape[1]
            pool_k = cache_fox_k.reshape(-1, PAGE * FOX_HEADS, FOX_HD)
            pool_v = cache_fox_v.reshape(-1, PAGE * FOX_HEADS, FOX_HD)
            pool_lf = cache_fox_logf.reshape(-1, PAGE * FOX_HEADS // LANE, LANE)
            o = fox_attend_sample(qb.reshape(DB, nr, FOX_HD), k32.reshape(DB, nr, FOX_HD), v32.reshape(DB, nr, FOX_HD),
                                  fq_col, fk_row, pool_k, pool_v, pool_lf, j * n_pool, page_table, 16)
            mix_s = (mod_s(1, 2), [o.reshape(1, RS, hw)],
                     [lambda tm, w=hw: pl.BlockSpec((None, tm, w), lambda b, l: (b, l, 0))], [wo])
            outs_s.setdefault('fk', []).append(k32.reshape(DB, DS, FOX_HEADS, FOX_HD))
            outs_s.setdefault('fv', []).append(v32.reshape(DB, DS, FOX_HEADS, FOX_HD))
            outs_s.setdefault('flf', []).append(lf.reshape(DB, DS, FOX_HEADS))
        xp = ffn_sublayer(xp, g[2], mod_p(2, 0), mod_p(2, 1), mod_p(2, 2), w1b, w3b, w2b, i, 1, tm_ffn, tf, mix_p)
        xs = ffn_sublayer(xs, g[2], mod_s(2, 0), mod_s(2, 1), mod_s(2, 2), w1b, w3b, w2b, i, 1, RS, tf, mix_s)

    def st(d, key):
        return jnp.stack(d[key])

    return (xp, xs.reshape(DB, DS, D),
            st(outs_p, 'ckv'), st(outs_p, 'kpe'), st(outs_p, 's5r'), st(outs_p, 's5i'),
            st(outs_p, 'fk'), st(outs_p, 'fv'), st(outs_p, 'flf'),
            st(outs_s, 'ckv'), st(outs_s, 'kpe'), st(outs_s, 's5r'), st(outs_s, 's5i'),
            st(outs_s, 'fk'), st(outs_s, 'fv'), st(outs_s, 'flf'))
```

```python
import functools
import math

import jax
import jax.numpy as jnp
from jax import lax
from jax.experimental import pallas as pl
from jax.experimental.pallas import tpu as pltpu

F32 = jnp.float32
BF16 = jnp.bfloat16

LANE = 128
VMEM_LIMIT_BYTES = 56 * 1024 * 1024

RMS_EPS = 1e-6
ROPE_THETA = 10000.0
NEG_BIG = -1e30
LOG2E = math.log2(math.e)

MLA_HEADS = 8
MLA_NOPE = 64
MLA_ROPE = 32
MLA_V = 64
MLA_SCALE = (MLA_NOPE + MLA_ROPE) ** -0.5
Q_LORA = 384
KV_LORA = 256
S5_GROUP = 16
S5_STATE = 64
FOX_HEADS = 8
FOX_HD = 128
FOX_SCALE = FOX_HD ** -0.5
PAGE = 128
FOX_SUB = 8

NT_DIMS = (((1,), (1,)), ((), ()))


def _params(*sem):
    return pltpu.CompilerParams(dimension_semantics=sem, vmem_limit_bytes=VMEM_LIMIT_BYTES)


def _dot(a, b):
    return jnp.dot(a, b, preferred_element_type=F32)


def _dot_nt(a, b):
    return lax.dot_general(a, b, NT_DIMS, preferred_element_type=F32)


def _split3(x):
    hi = x.astype(BF16)
    r1 = x - hi.astype(F32)
    mid = r1.astype(BF16)
    lo = (r1 - mid.astype(F32)).astype(BF16)
    return hi, mid, lo


def _rms(x, n):
    return lax.rsqrt(jnp.sum(x * x, axis=-1, keepdims=True) * (1.0 / n) + RMS_EPS)


def _modulate(x, g, scale, shift):
    return (x * _rms(x, x.shape[-1]) * g) * (1.0 + scale) + shift


def _silu(x):
    return x * jax.nn.sigmoid(x)


def _ada_kernel(c_ref, w_ref, b_ref, o_ref):
    a = _silu(c_ref[...]).astype(BF16)
    o_ref[...] = _dot(a, w_ref[...].astype(BF16)) + b_ref[...]


def ada_modulation(c_all, w_ada, b_ada):
    depth, d, n = w_ada.shape
    m = c_all.shape[0]
    tn = 1024
    return pl.pallas_call(
        _ada_kernel,
        grid=(depth, n // tn),
        in_specs=[
            pl.BlockSpec((m, d), lambda i, j: (0, 0)),
            pl.BlockSpec((None, d, tn), lambda i, j: (i, 0, j)),
            pl.BlockSpec((None, 1, tn), lambda i, j: (i, 0, j)),
        ],
        out_specs=pl.BlockSpec((None, m, tn), lambda i, j: (i, 0, j)),
        out_shape=jax.ShapeDtypeStruct((depth, m, n), F32),
        compiler_params=_params("parallel", "parallel"),
        name="ada",
    )(c_all, w_ada, b_ada.reshape(depth, 1, n))


def _row_spec(arr, tm):
    return pl.BlockSpec((None, tm, arr.shape[-1]), lambda b, l: (b, l, 0))


def _mod_spec(arr, tm):
    if arr.shape[1] == 1:
        return pl.BlockSpec((None, 1, arr.shape[-1]), lambda b, l: (b, 0, 0))
    return pl.BlockSpec((None, tm, arr.shape[-1]), lambda b, l: (b, l, 0))


def _full_spec(arr):
    nd = arr.ndim
    return pl.BlockSpec(arr.shape, lambda *_: (0,) * nd)


def _ffn_kernel(x_ref, g_ref, sh_ref, sc_ref, gt_ref, w1_ref, w3_ref, w2_ref, *rest, tf, n_mix):
    o_ref = rest[-1]
    x = x_ref[...]
    if n_mix:
        gm_ref = rest[0]
        a_refs = rest[1:1 + n_mix]
        wo_refs = rest[1 + n_mix:1 + 2 * n_mix]
        mix = _dot(a_refs[0][...].astype(BF16), wo_refs[0][...])
        for a_ref, wo_ref in zip(a_refs[1:], wo_refs[1:]):
            mix = mix + _dot(a_ref[...].astype(BF16), wo_ref[...])
        x = x + gm_ref[...] * mix
    h = _modulate(x, g_ref[...], sc_ref[...], sh_ref[...]).astype(BF16)
    acc = None
    for c in range(w1_ref.shape[-1] // tf):
        cs = slice(c * tf, (c + 1) * tf)
        t = (_silu(_dot(h, w1_ref[:, cs])) * _dot(h, w3_ref[:, cs])).astype(BF16)
        part = _dot(t, w2_ref[cs, :])
        acc = part if acc is None else acc + part
    o_ref[...] = x + 0.5 * gt_ref[...] * acc


def ffn_sublayer(x, g, shift, scale, gate, w1, w3, w2, li, half, tm, tf, mix=None):
    nb, r, d = x.shape
    f = w1.shape[-1]

    def resident(shape, idx):
        return pl.BlockSpec(shape, lambda b, l: idx, pipeline_mode=pl.Buffered(1))

    in_specs = [_row_spec(x, tm), pl.BlockSpec((1, d), lambda b, l: (0, 0)),
                _mod_spec(shift, tm), _mod_spec(scale, tm), _mod_spec(gate, tm),
                resident((None, None, d, f), (li, half, 0, 0)), resident((None, None, d, f), (li, half, 0, 0)),
                resident((None, None, f, d), (li, half, 0, 0))]
    args = [x, g.reshape(1, d), shift, scale, gate, w1, w3, w2]
    n_mix = 0
    if mix is not None:
        gate_mix, acts, spec_fns, wos = mix
        n_mix = len(acts)
        in_specs += [_mod_spec(gate_mix, tm)] + [fn(tm) for fn in spec_fns]
        in_specs += [resident(w.shape, (0,) * w.ndim) for w in wos]
        args += [gate_mix] + list(acts) + list(wos)
    return pl.pallas_call(
        functools.partial(_ffn_kernel, tf=tf, n_mix=n_mix),
        grid=(nb, r // tm),
        in_specs=in_specs,
        out_specs=_row_spec(x, tm),
        out_shape=jax.ShapeDtypeStruct(x.shape, F32),
        compiler_params=_params("parallel", "parallel"),
        name="ffn_mix" if n_mix else "ffn",
    )(*args)


def _aproj_kernel(x_ref, g_ref, sh_ref, sc_ref, gaq_ref, gbq_ref, gak_ref, gbk_ref, gq_ref, gkv_ref, gkn_ref,
                  win_ref, wq_ref, wkv_ref, ckv_ref, kpe_ref, u_ref, qf_ref, kf_ref, v_ref):
    h = _modulate(x_ref[...], g_ref[...], sc_ref[...], sh_ref[...]).astype(BF16)
    proj = _dot(h, win_ref[...])
    o1 = Q_LORA
    o2 = o1 + KV_LORA
    o3 = o2 + 512
    cq = proj[:, :o1]
    cqn = (cq * _rms(cq, Q_LORA) * gq_ref[...]).astype(BF16)
    ckv_raw = proj[:, o1:o2]
    ckv = ckv_raw * _rms(ckv_raw, KV_LORA) * gkv_ref[...]
    ckv_ref[...] = ckv
    u_ref[...] = proj[:, o2:o3]
    ka = proj[:, o3:o3 + LANE]
    kb = proj[:, o3 + LANE:o3 + 2 * LANE]
    kpe_blk = _rms(ka, MLA_ROPE) * (ka * gak_ref[...] + kb * gbk_ref[...])
    kpe_ref[...] = kpe_blk[:, MLA_NOPE:MLA_NOPE + MLA_ROPE]

    qraw = _dot(cqn, wq_ref[...])
    kv = _dot(ckv.astype(BF16), wkv_ref[...])
    lane = lax.broadcasted_iota(jnp.int32, (1, LANE), 1)
    is_n = lane < MLA_NOPE
    is_p = jnp.logical_and(lane >= MLA_NOPE, lane < MLA_NOPE + MLA_ROPE)
    gaq = gaq_ref[...]
    gbq = gbq_ref[...]
    gkn = gkn_ref[...]
    hw = MLA_HEADS * LANE
    for hh in range(MLA_HEADS):
        sl = slice(hh * LANE, (hh + 1) * LANE)
        a = qraw[:, sl]
        b = qraw[:, hw + hh * LANE:hw + (hh + 1) * LANE]
        sq = a * a
        msn = jnp.sum(jnp.where(is_n, sq, 0.0), axis=-1, keepdims=True) * (1.0 / MLA_NOPE)
        msp = jnp.sum(jnp.where(is_p, sq, 0.0), axis=-1, keepdims=True) * (1.0 / MLA_ROPE)
        r = jnp.where(is_n, lax.rsqrt(msn + RMS_EPS), lax.rsqrt(msp + RMS_EPS))
        qf_ref[:, sl] = (r * (a * gaq + b * gbq)).astype(BF16)
        kk = kv[:, sl]
        kf_ref[:, sl] = (kk * _rms(kk, MLA_NOPE) * gkn + kpe_blk).astype(BF16)
    v_ref[...] = kv[:, hw:].astype(BF16)


def mla_s5_project(x, g, shift, scale, tabs, gq, gkv, gkn, win, wq, wkv, tm):
    nb, r, d = x.shape
    gaq, gbq, gak, gbk = tabs
    tab_spec = pl.BlockSpec((tm, LANE), lambda b, l: (l, 0))
    hw = MLA_HEADS * LANE
    hv = MLA_HEADS * MLA_V
    u_w = win.shape[1] - Q_LORA - KV_LORA - 2 * LANE
    out_shape = [
        jax.ShapeDtypeStruct((nb, r, KV_LORA), F32),
        jax.ShapeDtypeStruct((nb, r, MLA_ROPE), F32),
        jax.ShapeDtypeStruct((r, nb * u_w), F32),
        jax.ShapeDtypeStruct((nb, r, hw), BF16),
        jax.ShapeDtypeStruct((nb, r, hw), BF16),
        jax.ShapeDtypeStruct((nb, r, hv), BF16),
    ]

    def rs(c):
        return pl.BlockSpec((None, tm, c), lambda b, l: (b, l, 0))

    out_specs = [rs(KV_LORA), rs(MLA_ROPE), pl.BlockSpec((tm, u_w), lambda b, l: (l, b)), rs(hw), rs(hw), rs(hv)]
    return pl.pallas_call(
        _aproj_kernel,
        grid=(nb, r // tm),
        in_specs=[_row_spec(x, tm), _full_spec(g), _mod_spec(shift, tm), _mod_spec(scale, tm),
                  tab_spec, tab_spec, tab_spec, tab_spec, _full_spec(gq), _full_spec(gkv), _full_spec(gkn),
                  _full_spec(win), _full_spec(wq), _full_spec(wkv)],
        out_specs=out_specs,
        out_shape=out_shape,
        compiler_params=_params("parallel", "parallel"),
        name="mla_s5_project",
    )(x, g, shift, scale, gaq, gbq, gak, gbk, gq, gkv, gkn, win, wq, wkv)


def _lane_fold(x, op):
    out = x[:, :LANE]
    for c in range(1, x.shape[1] // LANE):
        out = op(out, x[:, c * LANE:(c + 1) * LANE])
    return out


def _flash_kernel(*refs, hp, dv, has_bias, tq, tk):
    if has_bias:
        q_ref, k_ref, v_ref, fq_ref, fk_ref, o_ref, mr_scr, lp_scr, acc_scr = refs
    else:
        q_ref, k_ref, v_ref, o_ref, mr_scr, lp_scr, acc_scr = refs
    qi = pl.program_id(2)
    nd = tq // tk
    heads = range(hp)
    qs = [q_ref[:, i * LANE:(i + 1) * LANE] for i in heads]
    fqb = None
    if has_bias:
        parts = _split3(fq_ref[...] * LOG2E)
        sel_r = lax.broadcasted_iota(jnp.int32, (LANE, LANE), 0)
        fqb = []
        for i in heads:
            sel = jnp.where(sel_r == pl.program_id(1) * hp + i, 1.0, 0.0).astype(BF16)
            col128 = _dot(parts[0], sel) + _dot(parts[1], sel) + _dot(parts[2], sel)
            fqb.append(jnp.concatenate([col128] * (tk // LANE), axis=1))

    def keys(i, j):
        return k_ref[pl.ds(pl.multiple_of(j * tk, tk), tk), i * LANE:(i + 1) * LANE]

    def raw(i, j):
        return _dot_nt(qs[i], keys(i, j))

    def vals(i, j):
        g = (i * dv) // LANE
        return v_ref[pl.ds(pl.multiple_of(j * tk, tk), tk), g * LANE:(g + 1) * LANE]

    def scores(i, j):
        s = raw(i, j)
        if has_bias:
            s = s + fqb[i] - fk_ref[i, j]
        return s

    def diag_chunk(i, d):
        r0 = d * tk
        j = qi * nd + d
        s = _dot_nt(qs[i][r0:, :], keys(i, j))
        if has_bias:
            s = s + fqb[i][r0:, :] - fk_ref[i, j]
        rows = tq - r0
        keep = lax.broadcasted_iota(jnp.int32, (rows, tk), 1) <= lax.broadcasted_iota(jnp.int32, (rows, tk), 0)
        return jnp.where(keep, s, NEG_BIG)

    def merge_rows(full, part, r0, op):
        if full is None:
            return part
        if r0 == 0:
            return op(full, part)
        return jnp.concatenate([full[:r0], op(full[r0:], part)], axis=0)

    s_d = [[diag_chunk(i, d) for d in range(nd)] for i in heads]
    for i in heads:
        mr = None
        for d in range(nd):
            mr = merge_rows(mr, _lane_fold(s_d[i][d], jnp.maximum), d * tk, jnp.maximum)
        mr_scr[i] = mr

    def pass1(j, c):
        for i in heads:
            mr_scr[i] = jnp.maximum(mr_scr[i], _lane_fold(scores(i, j), jnp.maximum))
        return c

    lax.fori_loop(0, qi * nd, pass1, 0)
    shift = []
    for i in heads:
        mr = mr_scr[i]
        m = jnp.max(mr, axis=-1, keepdims=True)
        acc = None
        lp = None
        for d in range(nd):
            m_d = m if d == 0 else jnp.max(mr[d * tk:, :], axis=-1, keepdims=True)
            p_d = jnp.exp2(s_d[i][d] - m_d)
            acc = merge_rows(acc, _dot(p_d.astype(BF16), vals(i, qi * nd + d)), d * tk, jnp.add)
            lp = merge_rows(lp, _lane_fold(p_d, jnp.add), d * tk, jnp.add)
        acc_scr[i] = acc
        lp_scr[i] = lp
        shift.append((fqb[i] - m) if has_bias else jnp.broadcast_to(m, (tq, tk)))

    def pass2(j, c):
        for i in heads:
            if has_bias:
                p = jnp.exp2(raw(i, j) + shift[i] - fk_ref[i, j])
            else:
                p = jnp.exp2(raw(i, j) - shift[i])
            acc_scr[i] += _dot(p.astype(BF16), vals(i, j))
            lp_scr[i] += _lane_fold(p, jnp.add)
        return c

    lax.fori_loop(0, qi * nd, pass2, 0)
    outs = [acc_scr[i] / jnp.sum(lp_scr[i], axis=-1, keepdims=True) for i in heads]
    lane = lax.broadcasted_iota(jnp.int32, (1, LANE), 1)
    per_group = LANE // dv
    groups = []
    for g in range(hp // per_group):
        out = outs[g * per_group]
        for t in range(1, per_group):
            out = jnp.where(lane >= t * dv, outs[g * per_group + t], out)
        groups.append(out)
    o_ref[...] = (groups[0] if len(groups) == 1 else jnp.concatenate(groups, axis=1)).astype(o_ref.dtype)


def flash_prompt(q, k, v, fq, fk, hp, dv, tq, tk):
    b, l, hw = q.shape
    nh = hw // LANE
    assert (hp * dv) % LANE == 0 and LANE % dv == 0 and l % tq == 0 and tq % tk == 0
    has_bias = fq is not None
    in_specs = [
        pl.BlockSpec((None, tq, hp * LANE), lambda bb, h, i: (bb, i, h)),
        pl.BlockSpec((None, l, hp * LANE), lambda bb, h, i: (bb, 0, h)),
        pl.BlockSpec((None, l, hp * dv), lambda bb, h, i: (bb, 0, h)),
    ]
    args = [q, k, v]
    if has_bias:
        in_specs += [
            pl.BlockSpec((None, tq, LANE), lambda bb, h, i: (bb, i, 0)),
            pl.BlockSpec((None, hp, l // tk, 1, tk), lambda bb, h, i: (bb, h, 0, 0, 0)),
        ]
        args += [fq, fk]
    return pl.pallas_call(
        functools.partial(_flash_kernel, hp=hp, dv=dv, has_bias=has_bias, tq=tq, tk=tk),
        grid=(b, nh // hp, l // tq),
        in_specs=in_specs,
        out_specs=pl.BlockSpec((None, tq, hp * dv), lambda bb, h, i: (bb, i, h)),
        out_shape=jax.ShapeDtypeStruct((b, l, nh * dv), BF16),
        scratch_shapes=[pltpu.VMEM((hp, tq, LANE), F32), pltpu.VMEM((hp, tq, LANE), F32),
                        pltpu.VMEM((hp, tq, LANE), F32)],
        compiler_params=_params("parallel", "parallel", "arbitrary"),
        name="flash_bias" if has_bias else "flash",
    )(*args)


def _gelu_tanh(x):
    c = math.sqrt(2.0 / math.pi)
    return x * (0.5 * (1.0 + jnp.tanh(c * (x + 0.044715 * (x * x * x)))))


def _s5_kernel(u_ref, h0r_ref, h0i_ref, ar_ref, ai_ref, bd_ref, bdlo_ref, cd_ref, d_ref, wg_ref, bg_ref,
               y_ref, xr_ref, xi_ref, sr, si, xr_scr, xi_scr, *, t_blk, bt, precise):
    i = pl.program_id(0)
    rows = t_blk * bt
    ns = xr_scr.shape[-1]
    nh = ns // 2
    ch = u_ref.shape[-1] // 2

    @pl.when(i == 0)
    def _():
        xr_scr[...] = h0r_ref[...]
        xi_scr[...] = h0i_ref[...]

    u = u_ref[...].reshape(rows, 2 * ch)
    ub = u.astype(BF16)
    for hf in range(2):
        uh = ub[:, hf * ch:(hf + 1) * ch]
        bu = _dot(uh, bd_ref[hf])
        if precise:
            ulo = (u[:, hf * ch:(hf + 1) * ch] - uh.astype(F32)).astype(BF16)
            bu = bu + _dot(ulo, bd_ref[hf]) + _dot(uh, bdlo_ref[hf])
        sr[:, hf * nh:(hf + 1) * nh] = bu[:, :nh]
        si[:, hf * nh:(hf + 1) * nh] = bu[:, nh:]

    for hf in range(2):
        cs = slice(hf * nh, (hf + 1) * nh)
        ar = jnp.broadcast_to(ar_ref[:, cs], (bt, nh))
        ai = jnp.broadcast_to(ai_ref[:, cs], (bt, nh))

        def step(t, carry, cs=cs, ar=ar, ai=ai):
            xr, xi = carry
            o = pl.multiple_of(t * bt, bt)
            nr = ar * xr - ai * xi + sr[pl.ds(o, bt), cs]
            ni = ar * xi + ai * xr + si[pl.ds(o, bt), cs]
            sr[pl.ds(o, bt), cs] = nr
            si[pl.ds(o, bt), cs] = ni
            return nr, ni

        xr, xi = lax.fori_loop(0, t_blk, step, (xr_scr[:, cs], xi_scr[:, cs]))
        xr_scr[:, cs] = xr
        xi_scr[:, cs] = xi

    ys = []
    for hf in range(2):
        cs = slice(hf * nh, (hf + 1) * nh)
        ys.append(_dot(sr[:, cs].astype(BF16), cd_ref[hf, :nh, :]) + _dot(si[:, cs].astype(BF16), cd_ref[hf, nh:, :]))
    y = jnp.concatenate(ys, axis=1)
    y = _gelu_tanh(y + d_ref[...] * u)
    y = y * jax.nn.sigmoid(_dot(y.astype(BF16), wg_ref[...]) + bg_ref[...])
    y_ref[...] = y.reshape(t_blk, bt, 2 * ch)

    @pl.when(i == pl.num_programs(0) - 1)
    def _():
        xr_ref[...] = xr_scr[...]
        xi_ref[...] = xi_scr[...]


def s5_mix(u3, h0r, h0i, prm, t_blk, precise):
    t, bt, c = u3.shape
    ns = h0r.shape[-1]
    rows = t_blk * bt
    ar, ai, bd, bdlo, cd, d, wg, bg = prm
    blk = pl.BlockSpec((t_blk, bt, c), lambda i: (i, 0, 0))
    ins = [u3, h0r, h0i, ar, ai, bd, bdlo, cd, d, wg, bg]
    return pl.pallas_call(
        functools.partial(_s5_kernel, t_blk=t_blk, bt=bt, precise=precise),
        grid=(t // t_blk,),
        in_specs=[blk] + [_full_spec(a) for a in ins[1:]],
        out_specs=[blk, _full_spec(h0r), _full_spec(h0i)],
        out_shape=[jax.ShapeDtypeStruct((t, bt, c), F32), jax.ShapeDtypeStruct(h0r.shape, F32),
                   jax.ShapeDtypeStruct(h0i.shape, F32)],
        scratch_shapes=[pltpu.VMEM((rows, ns), F32), pltpu.VMEM((rows, ns), F32),
                        pltpu.VMEM((bt, ns), F32), pltpu.VMEM((bt, ns), F32)],
        compiler_params=_params("arbitrary"),
        name="s5",
    )(*ins)


def _foxproj_kernel(x_ref, g_ref, sh_ref, sc_ref, gq_ref, gk_ref, bf_ref, w_ref,
                    k_ref, v_ref, lf_ref, qb_ref, kb_ref, vb_ref, fc_ref, carry_scr, *, seg, tm):
    l_idx = pl.program_id(1)
    h = _modulate(x_ref[...], g_ref[...], sc_ref[...], sh_ref[...]).astype(BF16)
    proj = _dot(h, w_ref[...])
    hw = FOX_HEADS * FOX_HD
    gq = gq_ref[...]
    gk = gk_ref[...]
    for hh in range(FOX_HEADS):
        sl = slice(hh * FOX_HD, (hh + 1) * FOX_HD)
        qh = proj[:, sl]
        qb_ref[:, sl] = (qh * _rms(qh, FOX_HD) * gq).astype(BF16)
        kh = proj[:, hw + hh * FOX_HD:hw + (hh + 1) * FOX_HD]
        kn = kh * _rms(kh, FOX_HD) * gk
        k_ref[:, sl] = kn
        kb_ref[:, sl] = kn.astype(BF16)
    vv = proj[:, 2 * hw:3 * hw]
    v_ref[...] = vv
    vb_ref[...] = vv.astype(BF16)
    z = proj[:, 3 * hw:] + bf_ref[...]
    lf = jnp.minimum(z, 0.0) - jnp.log1p(jnp.exp(-jnp.abs(z)))
    lf_ref[...] = lf[:, :FOX_HEADS]
    row = lax.broadcasted_iota(jnp.int32, (tm, tm), 0)
    col = lax.broadcasted_iota(jnp.int32, (tm, tm), 1)
    keep = col <= row
    if seg < tm:
        keep = jnp.logical_and(keep, (col // seg) == (row // seg))
    tri = jnp.where(keep, 1.0, 0.0).astype(BF16)
    hi, mid, lo = _split3(lf)
    cs = _dot(tri, hi) + _dot(tri, mid) + _dot(tri, lo)
    if seg > tm:
        @pl.when(l_idx == 0)
        def _():
            carry_scr[...] = jnp.zeros_like(carry_scr)

        cs = cs + carry_scr[...]
        carry_scr[...] = cs[tm - 1:tm, :]
    fc_ref[...] = cs


def fox_project(x, g, shift, scale, gq, gk, bf, w, seg, tm):
    nb, r, d = x.shape
    hw = FOX_HEADS * FOX_HD

    def rs(c):
        return pl.BlockSpec((None, tm, c), lambda b, l: (b, l, 0))

    out_shape = [
        jax.ShapeDtypeStruct((nb, r, hw), F32), jax.ShapeDtypeStruct((nb, r, hw), F32),
        jax.ShapeDtypeStruct((nb, r, FOX_HEADS), F32),
        jax.ShapeDtypeStruct((nb, r, hw), BF16), jax.ShapeDtypeStruct((nb, r, hw), BF16),
        jax.ShapeDtypeStruct((nb, r, hw), BF16), jax.ShapeDtypeStruct((nb, r, LANE), F32),
    ]
    out_specs = [rs(hw), rs(hw), rs(FOX_HEADS), rs(hw), rs(hw), rs(hw), rs(LANE)]
    return pl.pallas_call(
        functools.partial(_foxproj_kernel, seg=seg, tm=tm),
        grid=(nb, r // tm),
        in_specs=[_row_spec(x, tm), _full_spec(g), _mod_spec(shift, tm), _mod_spec(scale, tm),
                  _full_spec(gq), _full_spec(gk), _full_spec(bf), _full_spec(w)],
        out_specs=out_specs,
        out_shape=out_shape,
        scratch_shapes=[pltpu.VMEM((1, LANE), F32)],
        compiler_params=_params("parallel", "arbitrary"),
        name="fox_project",
    )(x, g, shift, scale, gq, gk, bf, w)


def _page_dma(pools, bufs, sems, pps, page_of):
    def each(b, s, slot, act):
        def body(i, c):
            pg = page_of(b, s, i)
            for n, (pool, buf) in enumerate(zip(pools, bufs)):
                act(pltpu.make_async_copy(pool.at[pg], buf.at[slot, i], sems.at[slot, n]))
            return c

        lax.fori_loop(0, pps, body, 0)

    return (lambda b, s, slot: each(b, s, slot, lambda cp: cp.start()),
            lambda b, s, slot: each(b, s, slot, lambda cp: cp.wait()))


def _resident_slot(start, wait):
    b, s = pl.program_id(0), pl.program_id(1)
    n_steps = pl.num_programs(1)
    t = b * n_steps + s
    slot = t % 2

    @pl.when(t == 0)
    def _():
        start(b, s, slot)

    last = s + 1 == n_steps

    @pl.when(t + 1 < pl.num_programs(0) * n_steps)
    def _():
        start(jnp.where(last, b + 1, b), jnp.where(last, 0, s + 1), 1 - slot)

    wait(b, s, slot)
    return slot


def _mla_sample_kernel(pt_ref, q_ref, cn_ref, kn_ref, wfull_ref, wukt_ref, wuv_ref, ckv_hbm, kpe_hbm,
                       o_ref, ckv_buf, kpe_buf, sems, qabs_scr, qpe_scr, ckv_scr, m_scr, l_scr, acc_scr,
                       *, pps, page_base):
    s_idx = pl.program_id(1)
    nq = q_ref.shape[0]
    nrow = MLA_HEADS * nq
    start, wait = _page_dma((ckv_hbm, kpe_hbm), (ckv_buf, kpe_buf), sems, pps,
                            lambda b, s, i: page_base + pt_ref[b, s * pps + i])
    slot = _resident_slot(start, wait)

    @pl.when(s_idx == 0)
    def _():
        m_scr[...] = jnp.full_like(m_scr, NEG_BIG)
        l_scr[...] = jnp.zeros_like(l_scr)
        acc_scr[...] = jnp.zeros_like(acc_scr)
        for hh in range(MLA_HEADS):
            res = _dot(q_ref[:, hh * LANE:(hh + 1) * LANE], wfull_ref[hh])
            qabs_scr[hh * nq:(hh + 1) * nq, :] = res[:, :KV_LORA]
            qpe_scr[hh * nq:(hh + 1) * nq, :] = res[:, KV_LORA:]

    wstack = jnp.concatenate([wukt_ref[...], qabs_scr[...].astype(BF16)], axis=0)
    qpe = qpe_scr[...].astype(BF16)
    nk = MLA_HEADS * MLA_NOPE

    def scores(cb, kpe_t):
        a = _dot_nt(wstack, cb)
        spe = _dot(qpe[:, :MLA_ROPE], kpe_t)
        rows = []
        for hh in range(MLA_HEADS):
            kr = a[hh * MLA_NOPE:(hh + 1) * MLA_NOPE, :]
            ms = jnp.sum(kr * kr, axis=0, keepdims=True) * (1.0 / MLA_NOPE)
            rows.append(a[nk + hh * nq:nk + (hh + 1) * nq, :] * lax.rsqrt(ms + RMS_EPS))
        return jnp.concatenate(rows, axis=0) + spe

    def online(s, vals):
        m_prev = m_scr[...]
        m_new = jnp.maximum(m_prev, jnp.max(s, axis=-1, keepdims=True))
        alpha = jnp.exp2(m_prev - m_new)
        p = jnp.exp2(s - m_new)
        l_scr[...] = alpha * l_scr[...] + jnp.sum(p, axis=-1, keepdims=True)
        acc_scr[...] = alpha * acc_scr[...] + _dot(p.astype(BF16), vals)
        m_scr[...] = m_new

    for i in range(pps):
        ckv_scr[i * PAGE:(i + 1) * PAGE, :] = ckv_buf[slot, i].astype(BF16)
    cb = ckv_scr[...]
    kt = jnp.concatenate([kpe_buf[slot, i] for i in range(pps)], axis=1)
    online(scores(cb, kt.astype(BF16)), cb)

    @pl.when(s_idx == pl.num_programs(1) - 1)
    def _():
        pad = PAGE - nq
        cb = jnp.concatenate([cn_ref[...], jnp.zeros((pad, KV_LORA), F32)], axis=0).astype(BF16)
        kb = kn_ref[...].astype(BF16)
        s = scores(cb, kb)
        row = lax.broadcasted_iota(jnp.int32, (nrow, PAGE), 0)
        col = lax.broadcasted_iota(jnp.int32, (nrow, PAGE), 1)
        s = jnp.where(col <= row % nq, s, NEG_BIG)
        online(s, cb)
        o_lat = (acc_scr[...] / l_scr[...]).astype(BF16)
        out = _dot(o_lat[0:nq, :], wuv_ref[0])
        for hh in range(1, MLA_HEADS):
            out = out + _dot(o_lat[hh * nq:(hh + 1) * nq, :], wuv_ref[hh])
        o_ref[...] = out


def mla_attend_sample(qf, ckv_new, kpe_new, pool_ckv, pool_kpe, page_base, page_table, wfull, wukt, wuv, pps):
    db, nq, hw = qf.shape
    n_pages = page_table.shape[1]
    assert n_pages % pps == 0
    nrow = MLA_HEADS * nq

    def fixed(arr):
        nd = arr.ndim
        return pl.BlockSpec(arr.shape, lambda b, s, pt: (0,) * nd)

    def per_seq(arr):
        return pl.BlockSpec((None,) + arr.shape[1:], lambda b, s, pt: (b, 0, 0))

    hbm = pl.BlockSpec(memory_space=pl.ANY)
    in_specs = [per_seq(qf), per_seq(ckv_new), per_seq(kpe_new), fixed(wfull), fixed(wukt), fixed(wuv), hbm, hbm]
    out_w = MLA_HEADS * MLA_V
    grid_spec = pltpu.PrefetchScalarGridSpec(
        num_scalar_prefetch=1,
        grid=(db, n_pages // pps),
        in_specs=in_specs,
        out_specs=pl.BlockSpec((None, nq, out_w), lambda b, s, pt: (b, 0, 0)),
        scratch_shapes=[pltpu.VMEM((2, pps) + pool_ckv.shape[1:], F32), pltpu.VMEM((2, pps) + pool_kpe.shape[1:], F32),
                        pltpu.SemaphoreType.DMA((2, 2)),
                        pltpu.VMEM((nrow, KV_LORA), F32), pltpu.VMEM((nrow, LANE), F32),
                        pltpu.VMEM((pps * PAGE, KV_LORA), BF16),
                        pltpu.VMEM((nrow, 1), F32), pltpu.VMEM((nrow, 1), F32), pltpu.VMEM((nrow, KV_LORA), F32)],
    )
    return pl.pallas_call(
        functools.partial(_mla_sample_kernel, pps=pps, page_base=page_base),
        grid_spec=grid_spec,
        out_shape=jax.ShapeDtypeStruct((db, nq, out_w), F32),
        compiler_params=_params("arbitrary", "arbitrary"),
        name="mla_sample",
    )(page_table, qf, ckv_new, kpe_new, wfull, wukt, wuv, pool_ckv, pool_kpe)


def _suffix_flat(x):
    lane = lax.broadcasted_iota(jnp.int32, x.shape, 1)
    row = lax.broadcasted_iota(jnp.int32, x.shape, 0)
    y = x
    for s in (8, 16, 32, 64):
        y = y + jnp.where(lane + s < LANE, pltpu.roll(y, LANE - s, 1), 0.0)
    z = jnp.where(lane < FOX_HEADS, y, 0.0)
    for s in (8, 16, 32, 64):
        z = z + pltpu.roll(z, s, 1)
    v = z
    for s in (1, 2, 4):
        v = v + jnp.where(row + s < 8, pltpu.roll(v, 8 - s, 0), 0.0)
    return (y - x) + (v - z), v[0:1, :]


def _fox_sample_kernel(pt_ref, q_ref, kn_ref, vn_ref, fq_ref, fkn_ref, k_hbm, v_hbm, lf_hbm,
                       o_ref, k_buf, v_buf, lf_buf, sems, carry_scr, m_scr, l_scr, acc_scr, *, pps, page_base):
    s_idx = pl.program_id(1)
    nrow = q_ref.shape[0]
    prow = PAGE * FOX_HEADS
    n_pages = pt_ref.shape[1]
    start, wait = _page_dma((k_hbm, v_hbm, lf_hbm), (k_buf, v_buf, lf_buf), sems, pps,
                            lambda b, s, i: page_base + pt_ref[b, n_pages - 1 - (s * pps + i)])
    slot = _resident_slot(start, wait)

    @pl.when(s_idx == 0)
    def _():
        m_scr[...] = jnp.full_like(m_scr, NEG_BIG)
        l_scr[...] = jnp.zeros_like(l_scr)
        acc_scr[...] = jnp.zeros_like(acc_scr)
        carry_scr[...] = jnp.zeros_like(carry_scr)

    q = q_ref[...]
    row = lax.broadcasted_iota(jnp.int32, (nrow, LANE), 0)
    lane = lax.broadcasted_iota(jnp.int32, (nrow, LANE), 1)
    same_head = (row % FOX_HEADS) == (lane % FOX_HEADS)
    fqm = fq_ref[...] + jnp.where(same_head, 0.0, NEG_BIG)

    def online(state, s, vals):
        m_prev, l_prev, acc = state
        m_new = jnp.maximum(m_prev, jnp.max(s, axis=-1, keepdims=True))
        alpha = jnp.exp2(m_prev - m_new)
        p = jnp.exp2(s - m_new)
        l_new = alpha * l_prev + jnp.sum(p, axis=-1, keepdims=True)
        pb = p.astype(BF16)
        acc = alpha * acc
        o = 0
        for v in vals:
            acc = acc + _dot(pb[:, o:o + v.shape[0]], v)
            o += v.shape[0]
        return m_new, l_new, acc

    carry = carry_scr[...]
    state = (m_scr[...], l_scr[...], acc_scr[...])
    pending = None
    for g in range(pps // FOX_SUB):
        parts = []
        vals = []
        for i in range(g * FOX_SUB, (g + 1) * FOX_SUB):
            r, tot = _suffix_flat(lf_buf[slot, i])
            r = (r + carry) * LOG2E
            carry = carry + tot
            raw = _dot_nt(q, k_buf[slot, i].astype(BF16))
            vals.append(v_buf[slot, i].astype(BF16))
            for c in range(prow // LANE):
                parts.append(raw[:, c * LANE:(c + 1) * LANE] + (r[c:c + 1, :] + fqm))
        if pending is not None:
            state = online(state, *pending)
        pending = (jnp.concatenate(parts, axis=1), vals)
    state = online(state, *pending)
    carry_scr[...] = carry
    m_scr[...], l_scr[...], acc_scr[...] = state

    @pl.when(s_idx == pl.num_programs(1) - 1)
    def _():
        pad = LANE - kn_ref.shape[0]
        kb = jnp.concatenate([kn_ref[...], jnp.zeros((pad, FOX_HD), F32)], axis=0).astype(BF16)
        vb = jnp.concatenate([vn_ref[...], jnp.zeros((pad, FOX_HD), F32)], axis=0).astype(BF16)
        s = _dot_nt(q, kb) + fqm - fkn_ref[...]
        s = jnp.where(lane // FOX_HEADS <= row // FOX_HEADS, s, NEG_BIG)
        _, l_fin, acc_fin = online((m_scr[...], l_scr[...], acc_scr[...]), s, [vb])
        o_ref[...] = acc_fin / l_fin


def fox_attend_sample(qb, k_new, v_new, fq_col, fk_row, pool_k, pool_v, pool_lf, page_base, page_table, pps):
    db, nrow, hd = qb.shape
    n_pages = page_table.shape[1]
    assert n_pages % pps == 0

    def per_seq(arr):
        return pl.BlockSpec((None,) + arr.shape[1:], lambda b, s, pt: (b, 0, 0))

    hbm = pl.BlockSpec(memory_space=pl.ANY)
    in_specs = [per_seq(qb), per_seq(k_new), per_seq(v_new), per_seq(fq_col), per_seq(fk_row), hbm, hbm, hbm]
    grid_spec = pltpu.PrefetchScalarGridSpec(
        num_scalar_prefetch=1,
        grid=(db, n_pages // pps),
        in_specs=in_specs,
        out_specs=pl.BlockSpec((None, nrow, hd), lambda b, s, pt: (b, 0, 0)),
        scratch_shapes=[pltpu.VMEM((2, pps) + pool_k.shape[1:], F32), pltpu.VMEM((2, pps) + pool_v.shape[1:], F32),
                        pltpu.VMEM((2, pps) + pool_lf.shape[1:], F32), pltpu.SemaphoreType.DMA((2, 3)),
                        pltpu.VMEM((1, LANE), F32),
                        pltpu.VMEM((nrow, 1), F32), pltpu.VMEM((nrow, 1), F32), pltpu.VMEM((nrow, hd), F32)],
    )
    return pl.pallas_call(
        functools.partial(_fox_sample_kernel, pps=pps, page_base=page_base),
        grid_spec=grid_spec,
        out_shape=jax.ShapeDtypeStruct((db, nrow, hd), F32),
        compiler_params=_params("arbitrary", "arbitrary"),
        name="fox_sample",
    )(page_table, qb, k_new, v_new, fq_col, fk_row, pool_k, pool_v, pool_lf)


def _rope_perm():
    half = MLA_ROPE // 2
    idx = jnp.arange(MLA_ROPE)
    return jnp.where(idx < half, idx + half, idx - half), jnp.where(idx < half, -1.0, 1.0).astype(F32)


def _pad_rope_block(w):
    z = jnp.zeros(w.shape[:-1] + (MLA_NOPE,), w.dtype)
    z2 = jnp.zeros(w.shape[:-1] + (LANE - MLA_NOPE - MLA_ROPE,), w.dtype)
    return jnp.concatenate([z, w, z2], axis=-1)


def _mla_weights(a_w_in, a_w_uq, a_w_ukv):
    perm, _ = _rope_perm()
    o1 = Q_LORA
    o2 = o1 + KV_LORA
    o3 = o2 + MLA_ROPE
    w_kpe = a_w_in[:, o2:o3]
    win = jnp.concatenate([a_w_in[:, :o2], a_w_in[:, o3:], _pad_rope_block(w_kpe), _pad_rope_block(w_kpe[:, perm])],
                          axis=1).astype(BF16)
    wq3 = a_w_uq.reshape(Q_LORA, MLA_HEADS, MLA_NOPE + MLA_ROPE)
    zpad = jnp.zeros((Q_LORA, MLA_HEADS, LANE - MLA_NOPE - MLA_ROPE), F32)
    wq_a = jnp.concatenate([wq3, zpad], axis=-1).reshape(Q_LORA, MLA_HEADS * LANE)
    wq_b = _pad_rope_block(wq3[..., MLA_NOPE:][..., perm]).reshape(Q_LORA, MLA_HEADS * LANE)
    wq = jnp.concatenate([wq_a, wq_b], axis=1).astype(BF16)
    wkv3 = a_w_ukv.reshape(KV_LORA, MLA_HEADS, MLA_NOPE + MLA_V)
    wk = jnp.concatenate([wkv3[..., :MLA_NOPE], jnp.zeros((KV_LORA, MLA_HEADS, LANE - MLA_NOPE), F32)], axis=-1)
    wv = wkv3[..., MLA_NOPE:]
    wkv = jnp.concatenate([wk.reshape(KV_LORA, -1), wv.reshape(KV_LORA, -1)], axis=1).astype(BF16)
    return win, wq, wkv, wkv3


def _rope_tables(pos, qn_g, qr_g, kr_g):
    half = MLA_ROPE // 2
    perm, sign = _rope_perm()
    freq = ROPE_THETA ** (-jnp.arange(half, dtype=F32) / half)
    ang = pos.astype(F32)[:, None] * freq[None, :]
    cos = jnp.concatenate([jnp.cos(ang), jnp.cos(ang)], axis=1)
    sin = jnp.concatenate([jnp.sin(ang), jnp.sin(ang)], axis=1)
    n = pos.shape[0]
    zn = jnp.zeros((n, MLA_NOPE), F32)
    zp = jnp.zeros((n, LANE - MLA_NOPE - MLA_ROPE), F32)
    qs = MLA_SCALE * LOG2E
    gaq = jnp.concatenate([jnp.broadcast_to(qn_g[None, :], (n, MLA_NOPE)), qr_g[None, :] * cos, zp], axis=1) * qs
    gbq = jnp.concatenate([zn, (sign * qr_g[perm])[None, :] * sin, zp], axis=1) * qs
    gak = jnp.concatenate([zn, kr_g[None, :] * cos, zp], axis=1)
    gbk = jnp.concatenate([zn, (sign * kr_g[perm])[None, :] * sin, zp], axis=1)
    return gaq, gbq, gak, gbk


def _s5_params(a_re, a_im, log_dt, b_re, b_im, c_re, c_im, d_skip, w_glu, b_glu):
    g, n = a_re.shape
    dt = jnp.exp(log_dt)[:, None]
    mag = jnp.exp(dt * a_re)
    abar_re, abar_im = mag * jnp.cos(dt * a_im), mag * jnp.sin(dt * a_im)
    den = a_re * a_re + a_im * a_im
    w_re = ((abar_re - 1) * a_re + abar_im * a_im) / den
    w_im = (abar_im * a_re - (abar_re - 1) * a_im) / den
    bbar_re = w_re[..., None] * b_re - w_im[..., None] * b_im
    bbar_im = w_re[..., None] * b_im + w_im[..., None] * b_re
    gh = g // 2
    eye = jnp.eye(gh, dtype=F32)

    def pack_b(bb):
        return jnp.einsum('gnc,gh->gchn', bb, eye).reshape(gh * S5_GROUP, gh * n)

    def pack_c(cc):
        return jnp.einsum('gcn,gh->gnhc', cc, eye).reshape(gh * n, gh * S5_GROUP)

    bd = jnp.stack([jnp.concatenate([pack_b(bbar_re[h * gh:(h + 1) * gh]), pack_b(bbar_im[h * gh:(h + 1) * gh])], axis=1)
                    for h in range(2)])
    cd = jnp.stack([jnp.concatenate([pack_c(c_re[h * gh:(h + 1) * gh]), -pack_c(c_im[h * gh:(h + 1) * gh])], axis=0)
                    for h in range(2)])
    bd_hi = bd.astype(BF16)
    bd_lo = (bd - bd_hi.astype(F32)).astype(BF16)
    return (abar_re.reshape(1, g * n), abar_im.reshape(1, g * n), bd_hi, bd_lo, cd.astype(BF16),
            d_skip.reshape(1, -1), w_glu.astype(BF16), b_glu.reshape(1, -1))


def kernel(x_prompt, x_sample, c_prompt, c_sample, cache_mla_ckv, cache_mla_kpe, state_s5_re, state_s5_im,
           cache_fox_k, cache_fox_v, cache_fox_logf, page_table, w_ada, b_ada, norm_g, ffn_w1, ffn_w3, ffn_w2,
           a_w_in, a_q_norm, a_kv_norm, a_w_uq, a_w_ukv, a_qn_norm, a_qr_norm, a_kn_norm, a_kr_norm,
           s5_a_re, s5_a_im, s5_log_dt, s5_b_re, s5_b_im, s5_c_re, s5_c_im, s5_d, s5_w_glu, s5_b_glu,
           a_w_out, c_w_in, c_b_f, c_q_norm, c_k_norm, c_w_out):
    B, L, D = x_prompt.shape
    DB, DS, _ = x_sample.shape
    RS = DB * DS
    n_pages = page_table.shape[1]
    n_past = n_pages * PAGE
    depth = w_ada.shape[0]

    tm_p = min(512, L)
    tm_ffn = min(1024, L)
    tf = 256
    tq = min(1024, L)
    tk = min(512, L)

    m_all = ada_modulation(jnp.concatenate([c_prompt, c_sample], axis=0), w_ada, b_ada)
    w1b, w3b, w2b = ffn_w1.astype(BF16), ffn_w3.astype(BF16), ffn_w2.astype(BF16)

    xp = x_prompt
    xs = x_sample.reshape(1, RS, D)
    outs_p = {}
    outs_s = {}
    for i in range(depth):
        mp = m_all[i, :B].reshape(B, 3, 3, 1, D)
        ms = jnp.repeat(m_all[i, B:].reshape(DB, 3, 3, D), DS, axis=0).reshape(1, RS, 3, 3, D)

        def mod_p(s, k):
            return mp[:, s, k]

        def mod_s(s, k):
            return ms[:, :, s, k]

        g = norm_g[i]
        xp = ffn_sublayer(xp, g[0], mod_p(0, 0), mod_p(0, 1), mod_p(0, 2), w1b, w3b, w2b, i, 0, tm_ffn, tf)
        xs = ffn_sublayer(xs, g[0], mod_s(0, 0), mod_s(0, 1), mod_s(0, 2), w1b, w3b, w2b, i, 0, RS, tf)
        j = i // 2
        g1 = g[1].reshape(1, D)
        if i % 2 == 0:
            win, wq, wkv, wkv3 = _mla_weights(a_w_in[j], a_w_uq[j], a_w_ukv[j])
            gq = a_q_norm[j].reshape(1, -1)
            gkv = a_kv_norm[j].reshape(1, -1)
            gkn = jnp.concatenate([a_kn_norm[j], jnp.zeros((LANE - MLA_NOPE,), F32)]).reshape(1, LANE)
            s5p = _s5_params(s5_a_re[j], s5_a_im[j], s5_log_dt[j], s5_b_re[j], s5_b_im[j], s5_c_re[j], s5_c_im[j],
                             s5_d[j], s5_w_glu[j], s5_b_glu[j])
            wo = a_w_out[j].astype(BF16)
            hv = MLA_HEADS * MLA_V
            wo_att, wo_ssm = wo[:hv], wo[hv:]
            cw = s5_d.shape[-1]
            ns = s5_a_re.shape[1] * s5_a_re.shape[2]

            tabs = _rope_tables(jnp.arange(L), a_qn_norm[j], a_qr_norm[j], a_kr_norm[j])
            ckv, kpe, u_tm, qf, kf, vb = mla_s5_project(xp, g1, mod_p(1, 0), mod_p(1, 1), tabs, gq, gkv, gkn,
                                                        win, wq, wkv, tm_p)
            att = flash_prompt(qf, kf, vb, None, None, 2, MLA_V, tq, tk)
            zeros = jnp.zeros((B, ns), F32)
            ssm, sr, si = s5_mix(u_tm.reshape(L, B, cw), zeros, zeros, s5p, min(128, L), False)
            ssm2 = ssm.reshape(L, B * cw)
            mix_p = (mod_p(1, 2), [att, ssm2],
                     [lambda tm, w=hv: pl.BlockSpec((None, tm, w), lambda b, l: (b, l, 0)),
                      lambda tm, w=cw: pl.BlockSpec((tm, w), lambda b, l: (l, b))],
                     [wo_att, wo_ssm])
            outs_p.setdefault('ckv', []).append(ckv)
            outs_p.setdefault('kpe', []).append(kpe)
            outs_p.setdefault('s5r', []).append(sr.reshape(B, -1, S5_STATE))
            outs_p.setdefault('s5i', []).append(si.reshape(B, -1, S5_STATE))

            pos_s = n_past + jnp.tile(jnp.arange(DS), DB)
            tabs = _rope_tables(pos_s, a_qn_norm[j], a_qr_norm[j], a_kr_norm[j])
            ckv, kpe, u_s, qf, kf, vb = mla_s5_project(xs, g1, mod_s(1, 0), mod_s(1, 1), tabs, gq, gkv, gkn,
                                                       win, wq, wkv, RS)
            wuk = wkv3[..., :MLA_NOPE] * a_kn_norm[j][None, None, :]
            wabs = jnp.transpose(wuk, (1, 2, 0))
            top = jnp.concatenate([wabs, jnp.zeros((MLA_HEADS, MLA_NOPE, LANE), F32)], axis=-1)
            eye_blk = jnp.concatenate([jnp.zeros((MLA_ROPE, KV_LORA), F32), jnp.eye(MLA_ROPE, LANE, dtype=F32)], axis=-1)
            mid = jnp.broadcast_to(eye_blk[None], (MLA_HEADS, MLA_ROPE, KV_LORA + LANE))
            bot = jnp.zeros((MLA_HEADS, LANE - MLA_NOPE - MLA_ROPE, KV_LORA + LANE), F32)
            wfull = jnp.concatenate([top, mid, bot], axis=1).astype(BF16)
            wukt = jnp.transpose(wkv3[..., :MLA_NOPE], (1, 2, 0)).reshape(MLA_HEADS * MLA_NOPE, KV_LORA).astype(BF16)
            wuv = jnp.einsum('khd,hg->hkgd', wkv3[..., MLA_NOPE:], jnp.eye(MLA_HEADS, dtype=F32))
            wuv = wuv.reshape(MLA_HEADS, KV_LORA, hv).astype(BF16)
            kpe_t = jnp.transpose(kpe.reshape(DB, DS, MLA_ROPE), (0, 2, 1))
            kpe_t = jnp.concatenate([kpe_t, jnp.zeros((DB, MLA_ROPE, PAGE - DS), F32)], axis=-1)
            pool_ckv = cache_mla_ckv.reshape(-1, PAGE, KV_LORA)
            pool_kpe = jnp.swapaxes(cache_mla_kpe, 2, 3).reshape(-1, MLA_ROPE, PAGE)
            att_s = mla_attend_sample(qf.reshape(DB, DS, -1), ckv.reshape(DB, DS, -1), kpe_t,
                                      pool_ckv, pool_kpe, j * cache_mla_ckv.shape[1], page_table,
                                      wfull, wukt, wuv, min(32, n_pages))
            u3 = jnp.transpose(u_s.reshape(DB, DS, cw), (1, 0, 2))
            ssm, sr, si = s5_mix(u3, state_s5_re[j].reshape(DB, ns), state_s5_im[j].reshape(DB, ns), s5p, DS, True)
            ssm_s = jnp.transpose(ssm, (1, 0, 2)).reshape(1, RS, cw)
            mix_s = (mod_s(1, 2), [att_s.reshape(1, RS, hv), ssm_s],
                     [lambda tm, w=hv: pl.BlockSpec((None, tm, w), lambda b, l: (b, l, 0)),
                      lambda tm, w=cw: pl.BlockSpec((None, tm, w), lambda b, l: (b, l, 0))],
                     [wo_att, wo_ssm])
            outs_s.setdefault('ckv', []).append(ckv.reshape(DB, DS, -1))
            outs_s.setdefault('kpe', []).append(kpe.reshape(DB, DS, -1))
            outs_s.setdefault('s5r', []).append(sr.reshape(DB, -1, S5_STATE))
            outs_s.setdefault('s5i', []).append(si.reshape(DB, -1, S5_STATE))
        else:
            hw = FOX_HEADS * FOX_HD
            wf = jnp.concatenate([c_w_in[j], jnp.zeros((D, LANE - FOX_HEADS), F32)], axis=1)
            wf = wf.astype(BF16)
            gqf = (c_q_norm[j] * (FOX_SCALE * LOG2E)).reshape(1, FOX_HD)
            gkf = c_k_norm[j].reshape(1, FOX_HD)
            bf = jnp.concatenate([c_b_f[j], jnp.zeros((LANE - FOX_HEADS,), F32)]).reshape(1, LANE)
            wo = c_w_out[j].astype(BF16)

            k32, v32, lf, qb, kb, vb, fc = fox_project(xp, g1, mod_p(1, 0), mod_p(1, 1), gqf, gkf, bf, wf, L, tm_p)
            fcs = jnp.transpose(fc[:, :, :FOX_HEADS], (0, 2, 1)) * LOG2E
            fk = fcs.reshape(B, FOX_HEADS, L // tk, 1, tk)
            o = flash_prompt(qb, kb, vb, fc, fk, 2, FOX_HD, tq, tk)
            mix_p = (mod_p(1, 2), [o], [lambda tm, w=hw: pl.BlockSpec((None, tm, w), lambda b, l: (b, l, 0))], [wo])
            outs_p.setdefault('fk', []).append(k32.reshape(B, L, FOX_HEADS, FOX_HD))
            outs_p.setdefault('fv', []).append(v32.reshape(B, L, FOX_HEADS, FOX_HD))
            outs_p.setdefault('flf', []).append(lf)

            k32, v32, lf, qb, kb, vb, fc = fox_project(xs, g1, mod_s(1, 0), mod_s(1, 1), gqf, gkf, bf, wf, DS, RS)
            nr = DS * FOX_HEADS
            f_new = fc[0, :, :FOX_HEADS].reshape(DB, nr) * LOG2E
            fq_col = f_new[..., None]
            fk_row = jnp.concatenate([f_new, jnp.zeros((DB, LANE - nr), F32)], axis=-1)[:, None, :]
            n_pool = cache_fox_k.shape[1]
            pool_k = cache_fox_k.reshape(-1, PAGE * FOX_HEADS, FOX_HD)
            pool_v = cache_fox_v.reshape(-1, PAGE * FOX_HEADS, FOX_HD)
            pool_lf = cache_fox_logf.reshape(-1, PAGE * FOX_HEADS // LANE, LANE)
            o = fox_attend_sample(qb.reshape(DB, nr, FOX_HD), k32.reshape(DB, nr, FOX_HD), v32.reshape(DB, nr, FOX_HD),
                                  fq_col, fk_row, pool_k, pool_v, pool_lf, j * n_pool, page_table, 16)
            mix_s = (mod_s(1, 2), [o.reshape(1, RS, hw)],
                     [lambda tm, w=hw: pl.BlockSpec((None, tm, w), lambda b, l: (b, l, 0))], [wo])
            outs_s.setdefault('fk', []).append(k32.reshape(DB, DS, FOX_HEADS, FOX_HD))
            outs_s.setdefault('fv', []).append(v32.reshape(DB, DS, FOX_HEADS, FOX_HD))
            outs_s.setdefault('flf', []).append(lf.reshape(DB, DS, FOX_HEADS))
        xp = ffn_sublayer(xp, g[2], mod_p(2, 0), mod_p(2, 1), mod_p(2, 2), w1b, w3b, w2b, i, 1, tm_ffn, tf, mix_p)
        xs = ffn_sublayer(xs, g[2], mod_s(2, 0), mod_s(2, 1), mod_s(2, 2), w1b, w3b, w2b, i, 1, RS, tf, mix_s)

    def st(d, key):
        return jnp.stack(d[key])

    return (xp, xs.reshape(DB, DS, D),
            st(outs_p, 'ckv'), st(outs_p, 'kpe'), st(outs_p, 's5r'), st(outs_p, 's5i'),
            st(outs_p, 'fk'), st(outs_p, 'fv'), st(outs_p, 'flf'),
            st(outs_s, 'ckv'), st(outs_s, 'kpe'), st(outs_s, 's5r'), st(outs_s, 's5i'),
            st(outs_s, 'fk'), st(outs_s, 'fv'), st(outs_s, 'flf'))
```

```python
import functools
import math

import jax
import jax.numpy as jnp
from jax import lax
from jax.experimental import pallas as pl
from jax.experimental.pallas import tpu as pltpu

F32 = jnp.float32
BF16 = jnp.bfloat16

LANE = 128
VMEM_LIMIT_BYTES = 56 * 1024 * 1024

RMS_EPS = 1e-6
ROPE_THETA = 10000.0
NEG_BIG = -1e30
LOG2E = math.log2(math.e)

MLA_HEADS = 8
MLA_NOPE = 64
MLA_ROPE = 32
MLA_V = 64
MLA_SCALE = (MLA_NOPE + MLA_ROPE) ** -0.5
Q_LORA = 384
KV_LORA = 256
S5_GROUP = 16
S5_STATE = 64
FOX_HEADS = 8
FOX_HD = 128
FOX_SCALE = FOX_HD ** -0.5
PAGE = 128
FOX_SUB = 8
CUM_BLOCK = 128

NT_DIMS = (((1,), (1,)), ((), ()))


def _params(*sem):
    return pltpu.CompilerParams(dimension_semantics=sem, vmem_limit_bytes=VMEM_LIMIT_BYTES)


def _dot(a, b):
    return jnp.dot(a, b, preferred_element_type=F32)


def _dot_nt(a, b):
    return lax.dot_general(a, b, NT_DIMS, preferred_element_type=F32)


def _split3(x):
    hi = x.astype(BF16)
    r1 = x - hi.astype(F32)
    mid = r1.astype(BF16)
    lo = (r1 - mid.astype(F32)).astype(BF16)
    return hi, mid, lo


def _rms(x, n):
    return lax.rsqrt(jnp.sum(x * x, axis=-1, keepdims=True) * (1.0 / n) + RMS_EPS)


def _modulate(x, g, scale, shift):
    return (x * _rms(x, x.shape[-1]) * g) * (1.0 + scale) + shift


def _silu(x):
    return x * jax.nn.sigmoid(x)


def _ada_kernel(c_ref, w_ref, b_ref, o_ref):
    a = _silu(c_ref[...]).astype(BF16)
    o_ref[...] = _dot(a, w_ref[...].astype(BF16)) + b_ref[...]


def ada_modulation(c_all, w_ada, b_ada):
    depth, d, n = w_ada.shape
    m = c_all.shape[0]
    tn = 1024
    return pl.pallas_call(
        _ada_kernel,
        grid=(depth, n // tn),
        in_specs=[
            pl.BlockSpec((m, d), lambda i, j: (0, 0)),
            pl.BlockSpec((None, d, tn), lambda i, j: (i, 0, j)),
            pl.BlockSpec((None, 1, tn), lambda i, j: (i, 0, j)),
        ],
        out_specs=pl.BlockSpec((None, m, tn), lambda i, j: (i, 0, j)),
        out_shape=jax.ShapeDtypeStruct((depth, m, n), F32),
        compiler_params=_params("parallel", "parallel"),
        name="ada",
    )(c_all, w_ada, b_ada.reshape(depth, 1, n))


def _row_spec(arr, tm):
    return pl.BlockSpec((None, tm, arr.shape[-1]), lambda b, l: (b, l, 0))


def _mod_spec(arr, tm):
    if arr.shape[1] == 1:
        return pl.BlockSpec((None, 1, arr.shape[-1]), lambda b, l: (b, 0, 0))
    return pl.BlockSpec((None, tm, arr.shape[-1]), lambda b, l: (b, l, 0))


def _full_spec(arr):
    nd = arr.ndim
    return pl.BlockSpec(arr.shape, lambda *_: (0,) * nd)


def _ffn_kernel(x_ref, g_ref, sh_ref, sc_ref, gt_ref, w1_ref, w3_ref, w2_ref, *rest, tf, n_mix):
    o_ref = rest[-1]
    x = x_ref[...]
    if n_mix:
        gm_ref = rest[0]
        a_refs = rest[1:1 + n_mix]
        wo_refs = rest[1 + n_mix:1 + 2 * n_mix]
        mix = _dot(a_refs[0][...].astype(BF16), wo_refs[0][...])
        for a_ref, wo_ref in zip(a_refs[1:], wo_refs[1:]):
            mix = mix + _dot(a_ref[...].astype(BF16), wo_ref[...])
        x = x + gm_ref[...] * mix
    h = _modulate(x, g_ref[...], sc_ref[...], sh_ref[...]).astype(BF16)
    acc = None
    for c in range(w1_ref.shape[-1] // tf):
        cs = slice(c * tf, (c + 1) * tf)
        t = (_silu(_dot(h, w1_ref[:, cs])) * _dot(h, w3_ref[:, cs])).astype(BF16)
        part = _dot(t, w2_ref[cs, :])
        acc = part if acc is None else acc + part
    o_ref[...] = x + 0.5 * gt_ref[...] * acc


def ffn_sublayer(x, g, shift, scale, gate, w1, w3, w2, li, half, tm, tf, mix=None):
    nb, r, d = x.shape
    f = w1.shape[-1]

    def resident(shape, idx):
        return pl.BlockSpec(shape, lambda b, l: idx, pipeline_mode=pl.Buffered(1))

    in_specs = [_row_spec(x, tm), pl.BlockSpec((1, d), lambda b, l: (0, 0)),
                _mod_spec(shift, tm), _mod_spec(scale, tm), _mod_spec(gate, tm),
                resident((None, None, d, f), (li, half, 0, 0)), resident((None, None, d, f), (li, half, 0, 0)),
                resident((None, None, f, d), (li, half, 0, 0))]
    args = [x, g.reshape(1, d), shift, scale, gate, w1, w3, w2]
    n_mix = 0
    if mix is not None:
        gate_mix, acts, spec_fns, wos = mix
        n_mix = len(acts)
        in_specs += [_mod_spec(gate_mix, tm)] + [fn(tm) for fn in spec_fns]
        in_specs += [resident(w.shape, (0,) * w.ndim) for w in wos]
        args += [gate_mix] + list(acts) + list(wos)
    return pl.pallas_call(
        functools.partial(_ffn_kernel, tf=tf, n_mix=n_mix),
        grid=(nb, r // tm),
        in_specs=in_specs,
        out_specs=_row_spec(x, tm),
        out_shape=jax.ShapeDtypeStruct(x.shape, F32),
        compiler_params=_params("parallel", "parallel"),
        name="ffn_mix" if n_mix else "ffn",
    )(*args)


def _aproj_kernel(x_ref, g_ref, sh_ref, sc_ref, gaq_ref, gbq_ref, gak_ref, gbk_ref, gq_ref, gkv_ref, gkn_ref,
                  win_ref, wq_ref, wkv_ref, ckv_ref, kpe_ref, u_ref, qf_ref, kf_ref, v_ref):
    h = _modulate(x_ref[...], g_ref[...], sc_ref[...], sh_ref[...]).astype(BF16)
    proj = _dot(h, win_ref[...])
    o1 = Q_LORA
    o2 = o1 + KV_LORA
    o3 = o2 + 512
    cq = proj[:, :o1]
    cqn = (cq * _rms(cq, Q_LORA) * gq_ref[...]).astype(BF16)
    ckv_raw = proj[:, o1:o2]
    ckv = ckv_raw * _rms(ckv_raw, KV_LORA) * gkv_ref[...]
    ckv_ref[...] = ckv
    u_ref[...] = proj[:, o2:o3]
    ka = proj[:, o3:o3 + LANE]
    kb = proj[:, o3 + LANE:o3 + 2 * LANE]
    kpe_blk = _rms(ka, MLA_ROPE) * (ka * gak_ref[...] + kb * gbk_ref[...])
    kpe_ref[...] = kpe_blk[:, MLA_NOPE:MLA_NOPE + MLA_ROPE]

    qraw = _dot(cqn, wq_ref[...])
    kv = _dot(ckv.astype(BF16), wkv_ref[...])
    lane = lax.broadcasted_iota(jnp.int32, (1, LANE), 1)
    is_n = lane < MLA_NOPE
    is_p = jnp.logical_and(lane >= MLA_NOPE, lane < MLA_NOPE + MLA_ROPE)
    gaq = gaq_ref[...]
    gbq = gbq_ref[...]
    gkn = gkn_ref[...]
    hw = MLA_HEADS * LANE
    for hh in range(MLA_HEADS):
        sl = slice(hh * LANE, (hh + 1) * LANE)
        a = qraw[:, sl]
        b = qraw[:, hw + hh * LANE:hw + (hh + 1) * LANE]
        sq = a * a
        msn = jnp.sum(jnp.where(is_n, sq, 0.0), axis=-1, keepdims=True) * (1.0 / MLA_NOPE)
        msp = jnp.sum(jnp.where(is_p, sq, 0.0), axis=-1, keepdims=True) * (1.0 / MLA_ROPE)
        r = jnp.where(is_n, lax.rsqrt(msn + RMS_EPS), lax.rsqrt(msp + RMS_EPS))
        qf_ref[:, sl] = (r * (a * gaq + b * gbq)).astype(BF16)
        kk = kv[:, sl]
        kf_ref[:, sl] = (kk * _rms(kk, MLA_NOPE) * gkn + kpe_blk).astype(BF16)
    v_ref[...] = kv[:, hw:].astype(BF16)


def mla_s5_project(x, g, shift, scale, tabs, gq, gkv, gkn, win, wq, wkv, tm):
    nb, r, d = x.shape
    gaq, gbq, gak, gbk = tabs
    tab_spec = pl.BlockSpec((tm, LANE), lambda b, l: (l, 0))
    hw = MLA_HEADS * LANE
    hv = MLA_HEADS * MLA_V
    u_w = win.shape[1] - Q_LORA - KV_LORA - 2 * LANE
    out_shape = [
        jax.ShapeDtypeStruct((nb, r, KV_LORA), F32),
        jax.ShapeDtypeStruct((nb, r, MLA_ROPE), F32),
        jax.ShapeDtypeStruct((r, nb * u_w), F32),
        jax.ShapeDtypeStruct((nb, r, hw), BF16),
        jax.ShapeDtypeStruct((nb, r, hw), BF16),
        jax.ShapeDtypeStruct((nb, r, hv), BF16),
    ]

    def rs(c):
        return pl.BlockSpec((None, tm, c), lambda b, l: (b, l, 0))

    out_specs = [rs(KV_LORA), rs(MLA_ROPE), pl.BlockSpec((tm, u_w), lambda b, l: (l, b)), rs(hw), rs(hw), rs(hv)]
    return pl.pallas_call(
        _aproj_kernel,
        grid=(nb, r // tm),
        in_specs=[_row_spec(x, tm), _full_spec(g), _mod_spec(shift, tm), _mod_spec(scale, tm),
                  tab_spec, tab_spec, tab_spec, tab_spec, _full_spec(gq), _full_spec(gkv), _full_spec(gkn),
                  _full_spec(win), _full_spec(wq), _full_spec(wkv)],
        out_specs=out_specs,
        out_shape=out_shape,
        compiler_params=_params("parallel", "parallel"),
        name="mla_s5_project",
    )(x, g, shift, scale, gaq, gbq, gak, gbk, gq, gkv, gkn, win, wq, wkv)


def _lane_fold(x, op):
    out = x[:, :LANE]
    for c in range(1, x.shape[1] // LANE):
        out = op(out, x[:, c * LANE:(c + 1) * LANE])
    return out


def _flash_kernel(*refs, hp, dv, has_bias, tq, tk):
    if has_bias:
        q_ref, k_ref, v_ref, fq_ref, fk_ref, o_ref, mr_scr, lp_scr, acc_scr = refs
    else:
        q_ref, k_ref, v_ref, o_ref, mr_scr, lp_scr, acc_scr = refs
    qi = pl.program_id(2)
    nd = tq // tk
    heads = range(hp)
    qs = [q_ref[:, i * LANE:(i + 1) * LANE] for i in heads]
    fqb = None
    if has_bias:
        fblk = fq_ref[...] * LOG2E
        lane_id = lax.broadcasted_iota(jnp.int32, (1, LANE), 1)
        fqb = []
        for i in heads:
            pick = lane_id == pl.program_id(1) * hp + i
            colv = jnp.sum(jnp.where(pick, fblk, 0.0), axis=-1, keepdims=True)
            fqb.append(jnp.broadcast_to(colv, (tq, tk)))

    def keys(i, j):
        return k_ref[pl.ds(pl.multiple_of(j * tk, tk), tk), i * LANE:(i + 1) * LANE]

    def raw(i, j):
        return _dot_nt(qs[i], keys(i, j))

    def vals(i, j):
        g = (i * dv) // LANE
        return v_ref[pl.ds(pl.multiple_of(j * tk, tk), tk), g * LANE:(g + 1) * LANE]

    def scores(i, j):
        s = raw(i, j)
        if has_bias:
            s = s + fqb[i] - fk_ref[i, j]
        return s

    def diag_chunk(i, d):
        r0 = d * tk
        j = qi * nd + d
        s = _dot_nt(qs[i][r0:, :], keys(i, j))
        if has_bias:
            s = s + fqb[i][r0:, :] - fk_ref[i, j]
        rows = tq - r0
        keep = lax.broadcasted_iota(jnp.int32, (rows, tk), 1) <= lax.broadcasted_iota(jnp.int32, (rows, tk), 0)
        return jnp.where(keep, s, NEG_BIG)

    def merge_rows(full, part, r0, op):
        if full is None:
            return part
        if r0 == 0:
            return op(full, part)
        return jnp.concatenate([full[:r0], op(full[r0:], part)], axis=0)

    s_d = [[diag_chunk(i, d) for d in range(nd)] for i in heads]
    for i in heads:
        mr = None
        for d in range(nd):
            mr = merge_rows(mr, _lane_fold(s_d[i][d], jnp.maximum), d * tk, jnp.maximum)
        mr_scr[i] = mr

    def pass1(j, c):
        for i in heads:
            mr_scr[i] = jnp.maximum(mr_scr[i], _lane_fold(scores(i, j), jnp.maximum))
        return c

    lax.fori_loop(0, qi * nd, pass1, 0)
    shift = []
    for i in heads:
        mr = mr_scr[i]
        m = jnp.max(mr, axis=-1, keepdims=True)
        acc = None
        lp = None
        for d in range(nd):
            m_d = m if d == 0 else jnp.max(mr[d * tk:, :], axis=-1, keepdims=True)
            p_d = jnp.exp2(s_d[i][d] - m_d)
            acc = merge_rows(acc, _dot(p_d.astype(BF16), vals(i, qi * nd + d)), d * tk, jnp.add)
            lp = merge_rows(lp, _lane_fold(p_d, jnp.add), d * tk, jnp.add)
        acc_scr[i] = acc
        lp_scr[i] = lp
        shift.append((fqb[i] - m) if has_bias else jnp.broadcast_to(m, (tq, tk)))

    def pass2(j, c):
        for i in heads:
            if has_bias:
                p = jnp.exp2(raw(i, j) + shift[i] - fk_ref[i, j])
            else:
                p = jnp.exp2(raw(i, j) - shift[i])
            acc_scr[i] += _dot(p.astype(BF16), vals(i, j))
            lp_scr[i] += _lane_fold(p, jnp.add)
        return c

    lax.fori_loop(0, qi * nd, pass2, 0)
    outs = [acc_scr[i] / jnp.sum(lp_scr[i], axis=-1, keepdims=True) for i in heads]
    lane = lax.broadcasted_iota(jnp.int32, (1, LANE), 1)
    per_group = LANE // dv
    groups = []
    for g in range(hp // per_group):
        out = outs[g * per_group]
        for t in range(1, per_group):
            out = jnp.where(lane >= t * dv, outs[g * per_group + t], out)
        groups.append(out)
    o_ref[...] = (groups[0] if len(groups) == 1 else jnp.concatenate(groups, axis=1)).astype(o_ref.dtype)


def flash_prompt(q, k, v, fq, fk, hp, dv, tq, tk):
    b, l, hw = q.shape
    nh = hw // LANE
    assert (hp * dv) % LANE == 0 and LANE % dv == 0 and l % tq == 0 and tq % tk == 0
    has_bias = fq is not None
    in_specs = [
        pl.BlockSpec((None, tq, hp * LANE), lambda bb, h, i: (bb, i, h)),
        pl.BlockSpec((None, l, hp * LANE), lambda bb, h, i: (bb, 0, h)),
        pl.BlockSpec((None, l, hp * dv), lambda bb, h, i: (bb, 0, h)),
    ]
    args = [q, k, v]
    if has_bias:
        in_specs += [
            pl.BlockSpec((None, tq, LANE), lambda bb, h, i: (bb, i, 0)),
            pl.BlockSpec((None, hp, l // tk, 1, tk), lambda bb, h, i: (bb, h, 0, 0, 0)),
        ]
        args += [fq, fk]
    return pl.pallas_call(
        functools.partial(_flash_kernel, hp=hp, dv=dv, has_bias=has_bias, tq=tq, tk=tk),
        grid=(b, nh // hp, l // tq),
        in_specs=in_specs,
        out_specs=pl.BlockSpec((None, tq, hp * dv), lambda bb, h, i: (bb, i, h)),
        out_shape=jax.ShapeDtypeStruct((b, l, nh * dv), BF16),
        scratch_shapes=[pltpu.VMEM((hp, tq, LANE), F32), pltpu.VMEM((hp, tq, LANE), F32),
                        pltpu.VMEM((hp, tq, LANE), F32)],
        compiler_params=_params("parallel", "parallel", "arbitrary"),
        name="flash_bias" if has_bias else "flash",
    )(*args)


def _gelu_tanh(x):
    c = math.sqrt(2.0 / math.pi)
    return x * (0.5 * (1.0 + jnp.tanh(c * (x + 0.044715 * (x * x * x)))))


def _s5_kernel(u_ref, h0r_ref, h0i_ref, ar_ref, ai_ref, bd_ref, bdlo_ref, cd_ref, d_ref, wg_ref, bg_ref,
               y_ref, xr_ref, xi_ref, sr, si, xr_scr, xi_scr, *, t_blk, bt, precise):
    i = pl.program_id(0)
    rows = t_blk * bt
    ns = xr_scr.shape[-1]
    nh = ns // 2
    ch = u_ref.shape[-1] // 2

    @pl.when(i == 0)
    def _():
        xr_scr[...] = h0r_ref[...]
        xi_scr[...] = h0i_ref[...]

    u = u_ref[...].reshape(rows, 2 * ch)
    ub = u.astype(BF16)
    for hf in range(2):
        uh = ub[:, hf * ch:(hf + 1) * ch]
        bu = _dot(uh, bd_ref[hf])
        if precise:
            ulo = (u[:, hf * ch:(hf + 1) * ch] - uh.astype(F32)).astype(BF16)
            bu = bu + _dot(ulo, bd_ref[hf]) + _dot(uh, bdlo_ref[hf])
        sr[:, hf * nh:(hf + 1) * nh] = bu[:, :nh]
        si[:, hf * nh:(hf + 1) * nh] = bu[:, nh:]

    for hf in range(2):
        cs = slice(hf * nh, (hf + 1) * nh)
        ar = jnp.broadcast_to(ar_ref[:, cs], (bt, nh))
        ai = jnp.broadcast_to(ai_ref[:, cs], (bt, nh))

        def step(t, carry, cs=cs, ar=ar, ai=ai):
            xr, xi = carry
            o = pl.multiple_of(t * bt, bt)
            nr = ar * xr - ai * xi + sr[pl.ds(o, bt), cs]
            ni = ar * xi + ai * xr + si[pl.ds(o, bt), cs]
            sr[pl.ds(o, bt), cs] = nr
            si[pl.ds(o, bt), cs] = ni
            return nr, ni

        xr, xi = lax.fori_loop(0, t_blk, step, (xr_scr[:, cs], xi_scr[:, cs]))
        xr_scr[:, cs] = xr
        xi_scr[:, cs] = xi

    ys = []
    for hf in range(2):
        cs = slice(hf * nh, (hf + 1) * nh)
        ys.append(_dot(sr[:, cs].astype(BF16), cd_ref[hf, :nh, :]) + _dot(si[:, cs].astype(BF16), cd_ref[hf, nh:, :]))
    y = jnp.concatenate(ys, axis=1)
    y = _gelu_tanh(y + d_ref[...] * u)
    y = y * jax.nn.sigmoid(_dot(y.astype(BF16), wg_ref[...]) + bg_ref[...])
    y_ref[...] = y.reshape(t_blk, bt, 2 * ch)

    @pl.when(i == pl.num_programs(0) - 1)
    def _():
        xr_ref[...] = xr_scr[...]
        xi_ref[...] = xi_scr[...]


def s5_mix(u3, h0r, h0i, prm, t_blk, precise):
    t, bt, c = u3.shape
    ns = h0r.shape[-1]
    rows = t_blk * bt
    ar, ai, bd, bdlo, cd, d, wg, bg = prm
    blk = pl.BlockSpec((t_blk, bt, c), lambda i: (i, 0, 0))
    ins = [u3, h0r, h0i, ar, ai, bd, bdlo, cd, d, wg, bg]
    return pl.pallas_call(
        functools.partial(_s5_kernel, t_blk=t_blk, bt=bt, precise=precise),
        grid=(t // t_blk,),
        in_specs=[blk] + [_full_spec(a) for a in ins[1:]],
        out_specs=[blk, _full_spec(h0r), _full_spec(h0i)],
        out_shape=[jax.ShapeDtypeStruct((t, bt, c), F32), jax.ShapeDtypeStruct(h0r.shape, F32),
                   jax.ShapeDtypeStruct(h0i.shape, F32)],
        scratch_shapes=[pltpu.VMEM((rows, ns), F32), pltpu.VMEM((rows, ns), F32),
                        pltpu.VMEM((bt, ns), F32), pltpu.VMEM((bt, ns), F32)],
        compiler_params=_params("arbitrary"),
        name="s5",
    )(*ins)


def _foxproj_kernel(x_ref, g_ref, sh_ref, sc_ref, gq_ref, gk_ref, bf_ref, w_ref,
                    k_ref, v_ref, lf_ref, qb_ref, kb_ref, vb_ref, fc_ref, carry_scr, *, seg, tm):
    l_idx = pl.program_id(1)
    h = _modulate(x_ref[...], g_ref[...], sc_ref[...], sh_ref[...]).astype(BF16)
    proj = _dot_nt(h, w_ref[...])
    hw = FOX_HEADS * FOX_HD
    gq = gq_ref[...]
    gk = gk_ref[...]
    for hh in range(FOX_HEADS):
        sl = slice(hh * FOX_HD, (hh + 1) * FOX_HD)
        qh = proj[:, sl]
        qb_ref[:, sl] = (qh * _rms(qh, FOX_HD) * gq).astype(BF16)
        kh = proj[:, hw + hh * FOX_HD:hw + (hh + 1) * FOX_HD]
        kn = kh * _rms(kh, FOX_HD) * gk
        k_ref[:, sl] = kn
        kb_ref[:, sl] = kn.astype(BF16)
    vv = proj[:, 2 * hw:3 * hw]
    v_ref[...] = vv
    vb_ref[...] = vv.astype(BF16)
    z = proj[:, 3 * hw:] + bf_ref[...]
    lf = jnp.minimum(z, 0.0) - jnp.log1p(jnp.exp(-jnp.abs(z)))
    lf_ref[...] = lf[:, :FOX_HEADS]
    sub = min(CUM_BLOCK, tm)
    row = lax.broadcasted_iota(jnp.int32, (sub, sub), 0)
    col = lax.broadcasted_iota(jnp.int32, (sub, sub), 1)
    keep = col <= row
    if seg < sub:
        keep = jnp.logical_and(keep, (col // seg) == (row // seg))
    tri = jnp.where(keep, 1.0, 0.0).astype(BF16)
    carry = None
    if seg > tm:
        @pl.when(l_idx == 0)
        def _():
            carry_scr[...] = jnp.zeros_like(carry_scr)

        carry = carry_scr[...]
    for c in range(tm // sub):
        hi, mid, lo = _split3(lf[c * sub:(c + 1) * sub, :])
        cs = _dot(tri, hi) + _dot(tri, mid) + _dot(tri, lo)
        if carry is not None:
            cs = cs + carry
            carry = cs[sub - 1:sub, :]
        fc_ref[c * sub:(c + 1) * sub, :] = cs
    if carry is not None:
        carry_scr[...] = carry


def fox_project(x, g, shift, scale, gq, gk, bf, w, seg, tm):
    nb, r, d = x.shape
    hw = FOX_HEADS * FOX_HD
    assert seg > tm or min(CUM_BLOCK, tm) % seg == 0

    def rs(c):
        return pl.BlockSpec((None, tm, c), lambda b, l: (b, l, 0))

    out_shape = [
        jax.ShapeDtypeStruct((nb, r, hw), F32), jax.ShapeDtypeStruct((nb, r, hw), F32),
        jax.ShapeDtypeStruct((nb, r, FOX_HEADS), F32),
        jax.ShapeDtypeStruct((nb, r, hw), BF16), jax.ShapeDtypeStruct((nb, r, hw), BF16),
        jax.ShapeDtypeStruct((nb, r, hw), BF16), jax.ShapeDtypeStruct((nb, r, LANE), F32),
    ]
    out_specs = [rs(hw), rs(hw), rs(FOX_HEADS), rs(hw), rs(hw), rs(hw), rs(LANE)]
    return pl.pallas_call(
        functools.partial(_foxproj_kernel, seg=seg, tm=tm),
        grid=(nb, r // tm),
        in_specs=[_row_spec(x, tm), _full_spec(g), _mod_spec(shift, tm), _mod_spec(scale, tm),
                  _full_spec(gq), _full_spec(gk), _full_spec(bf), _full_spec(w)],
        out_specs=out_specs,
        out_shape=out_shape,
        scratch_shapes=[pltpu.VMEM((1, LANE), F32)],
        compiler_params=_params("parallel", "arbitrary"),
        name="fox_project",
    )(x, g, shift, scale, gq, gk, bf, w)


def _page_dma(pools, bufs, sems, pps, page_of):
    def each(b, s, slot, act):
        def body(i, c):
            pg = page_of(b, s, i)
            for n, (pool, buf) in enumerate(zip(pools, bufs)):
                act(pltpu.make_async_copy(pool.at[pg], buf.at[slot, i], sems.at[slot, n]))
            return c

        lax.fori_loop(0, pps, body, 0)

    return (lambda b, s, slot: each(b, s, slot, lambda cp: cp.start()),
            lambda b, s, slot: each(b, s, slot, lambda cp: cp.wait()))


def _resident_slot(start, wait):
    b, s = pl.program_id(0), pl.program_id(1)
    n_steps = pl.num_programs(1)
    t = b * n_steps + s
    slot = t % 2

    @pl.when(t == 0)
    def _():
        start(b, s, slot)

    last = s + 1 == n_steps

    @pl.when(t + 1 < pl.num_programs(0) * n_steps)
    def _():
        start(jnp.where(last, b + 1, b), jnp.where(last, 0, s + 1), 1 - slot)

    wait(b, s, slot)
    return slot


def _mla_sample_kernel(pt_ref, q_ref, cn_ref, kn_ref, wfull_ref, wukt_ref, wuv_ref, ckv_hbm, kpe_hbm,
                       o_ref, ckv_buf, kpe_buf, sems, qabs_scr, qpe_scr, ckv_scr, m_scr, l_scr, acc_scr,
                       *, pps, page_base):
    s_idx = pl.program_id(1)
    nq = q_ref.shape[0]
    nrow = MLA_HEADS * nq
    start, wait = _page_dma((ckv_hbm, kpe_hbm), (ckv_buf, kpe_buf), sems, pps,
                            lambda b, s, i: page_base + pt_ref[b, s * pps + i])
    slot = _resident_slot(start, wait)

    @pl.when(s_idx == 0)
    def _():
        m_scr[...] = jnp.full_like(m_scr, NEG_BIG)
        l_scr[...] = jnp.zeros_like(l_scr)
        acc_scr[...] = jnp.zeros_like(acc_scr)
        for hh in range(MLA_HEADS):
            res = _dot(q_ref[:, hh * LANE:(hh + 1) * LANE], wfull_ref[hh])
            qabs_scr[hh * nq:(hh + 1) * nq, :] = res[:, :KV_LORA]
            qpe_scr[hh * nq:(hh + 1) * nq, :] = res[:, KV_LORA:]

    wstack = jnp.concatenate([wukt_ref[...], qabs_scr[...].astype(BF16)], axis=0)
    qpe = qpe_scr[...].astype(BF16)
    nk = MLA_HEADS * MLA_NOPE

    def scores(cb, kpe_t):
        a = _dot_nt(wstack, cb)
        spe = _dot(qpe[:, :MLA_ROPE], kpe_t)
        rows = []
        for hh in range(MLA_HEADS):
            kr = a[hh * MLA_NOPE:(hh + 1) * MLA_NOPE, :]
            ms = jnp.sum(kr * kr, axis=0, keepdims=True) * (1.0 / MLA_NOPE)
            rows.append(a[nk + hh * nq:nk + (hh + 1) * nq, :] * lax.rsqrt(ms + RMS_EPS))
        return jnp.concatenate(rows, axis=0) + spe

    def online(s, vals):
        m_prev = m_scr[...]
        m_new = jnp.maximum(m_prev, jnp.max(s, axis=-1, keepdims=True))
        alpha = jnp.exp2(m_prev - m_new)
        p = jnp.exp2(s - m_new)
        l_scr[...] = alpha * l_scr[...] + jnp.sum(p, axis=-1, keepdims=True)
        acc_scr[...] = alpha * acc_scr[...] + _dot(p.astype(BF16), vals)
        m_scr[...] = m_new

    for i in range(pps):
        ckv_scr[i * PAGE:(i + 1) * PAGE, :] = ckv_buf[slot, i].astype(BF16)
    cb = ckv_scr[...]
    kt = jnp.concatenate([kpe_buf[slot, i] for i in range(pps)], axis=1)
    online(scores(cb, kt.astype(BF16)), cb)

    @pl.when(s_idx == pl.num_programs(1) - 1)
    def _():
        pad = PAGE - nq
        cb = jnp.concatenate([cn_ref[...], jnp.zeros((pad, KV_LORA), F32)], axis=0).astype(BF16)
        kb = kn_ref[...].astype(BF16)
        s = scores(cb, kb)
        row = lax.broadcasted_iota(jnp.int32, (nrow, PAGE), 0)
        col = lax.broadcasted_iota(jnp.int32, (nrow, PAGE), 1)
        s = jnp.where(col <= row % nq, s, NEG_BIG)
        online(s, cb)
        o_lat = (acc_scr[...] / l_scr[...]).astype(BF16)
        out = _dot(o_lat[0:nq, :], wuv_ref[0])
        for hh in range(1, MLA_HEADS):
            out = out + _dot(o_lat[hh * nq:(hh + 1) * nq, :], wuv_ref[hh])
        o_ref[...] = out


def mla_attend_sample(qf, ckv_new, kpe_new, pool_ckv, pool_kpe, page_base, page_table, wfull, wukt, wuv, pps):
    db, nq, hw = qf.shape
    n_pages = page_table.shape[1]
    assert n_pages % pps == 0
    nrow = MLA_HEADS * nq

    def fixed(arr):
        nd = arr.ndim
        return pl.BlockSpec(arr.shape, lambda b, s, pt: (0,) * nd)

    def per_seq(arr):
        return pl.BlockSpec((None,) + arr.shape[1:], lambda b, s, pt: (b, 0, 0))

    hbm = pl.BlockSpec(memory_space=pl.ANY)
    in_specs = [per_seq(qf), per_seq(ckv_new), per_seq(kpe_new), fixed(wfull), fixed(wukt), fixed(wuv), hbm, hbm]
    out_w = MLA_HEADS * MLA_V
    grid_spec = pltpu.PrefetchScalarGridSpec(
        num_scalar_prefetch=1,
        grid=(db, n_pages // pps),
        in_specs=in_specs,
        out_specs=pl.BlockSpec((None, nq, out_w), lambda b, s, pt: (b, 0, 0)),
        scratch_shapes=[pltpu.VMEM((2, pps) + pool_ckv.shape[1:], F32), pltpu.VMEM((2, pps) + pool_kpe.shape[1:], F32),
                        pltpu.SemaphoreType.DMA((2, 2)),
                        pltpu.VMEM((nrow, KV_LORA), F32), pltpu.VMEM((nrow, LANE), F32),
                        pltpu.VMEM((pps * PAGE, KV_LORA), BF16),
                        pltpu.VMEM((nrow, 1), F32), pltpu.VMEM((nrow, 1), F32), pltpu.VMEM((nrow, KV_LORA), F32)],
    )
    return pl.pallas_call(
        functools.partial(_mla_sample_kernel, pps=pps, page_base=page_base),
        grid_spec=grid_spec,
        out_shape=jax.ShapeDtypeStruct((db, nq, out_w), F32),
        compiler_params=_params("arbitrary", "arbitrary"),
        name="mla_sample",
    )(page_table, qf, ckv_new, kpe_new, wfull, wukt, wuv, pool_ckv, pool_kpe)


def _suffix_flat(x):
    lane = lax.broadcasted_iota(jnp.int32, x.shape, 1)
    row = lax.broadcasted_iota(jnp.int32, x.shape, 0)
    y = x
    for s in (8, 16, 32, 64):
        y = y + jnp.where(lane + s < LANE, pltpu.roll(y, LANE - s, 1), 0.0)
    z = jnp.where(lane < FOX_HEADS, y, 0.0)
    for s in (8, 16, 32, 64):
        z = z + pltpu.roll(z, s, 1)
    v = z
    for s in (1, 2, 4):
        v = v + jnp.where(row + s < 8, pltpu.roll(v, 8 - s, 0), 0.0)
    return (y - x) + (v - z), v[0:1, :]


def _fox_sample_kernel(pt_ref, q_ref, kn_ref, vn_ref, fq_ref, fkn_ref, k_hbm, v_hbm, lf_hbm,
                       o_ref, k_buf, v_buf, lf_buf, sems, carry_scr, m_scr, l_scr, acc_scr, *, pps, page_base):
    s_idx = pl.program_id(1)
    nrow = q_ref.shape[0]
    prow = PAGE * FOX_HEADS
    n_pages = pt_ref.shape[1]
    start, wait = _page_dma((k_hbm, v_hbm, lf_hbm), (k_buf, v_buf, lf_buf), sems, pps,
                            lambda b, s, i: page_base + pt_ref[b, n_pages - 1 - (s * pps + i)])
    slot = _resident_slot(start, wait)

    @pl.when(s_idx == 0)
    def _():
        m_scr[...] = jnp.full_like(m_scr, NEG_BIG)
        l_scr[...] = jnp.zeros_like(l_scr)
        acc_scr[...] = jnp.zeros_like(acc_scr)
        carry_scr[...] = jnp.zeros_like(carry_scr)

    q = q_ref[...]
    row = lax.broadcasted_iota(jnp.int32, (nrow, LANE), 0)
    lane = lax.broadcasted_iota(jnp.int32, (nrow, LANE), 1)
    same_head = (row % FOX_HEADS) == (lane % FOX_HEADS)
    fqm = fq_ref[...] + jnp.where(same_head, 0.0, NEG_BIG)

    def online(state, s, vals):
        m_prev, l_prev, acc = state
        m_new = jnp.maximum(m_prev, jnp.max(s, axis=-1, keepdims=True))
        alpha = jnp.exp2(m_prev - m_new)
        p = jnp.exp2(s - m_new)
        l_new = alpha * l_prev + jnp.sum(p, axis=-1, keepdims=True)
        pb = p.astype(BF16)
        acc = alpha * acc
        o = 0
        for v in vals:
            acc = acc + _dot(pb[:, o:o + v.shape[0]], v)
            o += v.shape[0]
        return m_new, l_new, acc

    carry = carry_scr[...]
    state = (m_scr[...], l_scr[...], acc_scr[...])
    pending = None
    for g in range(pps // FOX_SUB):
        parts = []
        vals = []
        for i in range(g * FOX_SUB, (g + 1) * FOX_SUB):
            r, tot = _suffix_flat(lf_buf[slot, i])
            r = (r + carry) * LOG2E
            carry = carry + tot
            raw = _dot_nt(q, k_buf[slot, i].astype(BF16))
            vals.append(v_buf[slot, i].astype(BF16))
            for c in range(prow // LANE):
                parts.append(raw[:, c * LANE:(c + 1) * LANE] + (r[c:c + 1, :] + fqm))
        if pending is not None:
            state = online(state, *pending)
        pending = (jnp.concatenate(parts, axis=1), vals)
    state = online(state, *pending)
    carry_scr[...] = carry
    m_scr[...], l_scr[...], acc_scr[...] = state

    @pl.when(s_idx == pl.num_programs(1) - 1)
    def _():
        pad = LANE - kn_ref.shape[0]
        kb = jnp.concatenate([kn_ref[...], jnp.zeros((pad, FOX_HD), F32)], axis=0).astype(BF16)
        vb = jnp.concatenate([vn_ref[...], jnp.zeros((pad, FOX_HD), F32)], axis=0).astype(BF16)
        s = _dot_nt(q, kb) + fqm - fkn_ref[...]
        s = jnp.where(lane // FOX_HEADS <= row // FOX_HEADS, s, NEG_BIG)
        _, l_fin, acc_fin = online((m_scr[...], l_scr[...], acc_scr[...]), s, [vb])
        o_ref[...] = acc_fin / l_fin


def fox_attend_sample(qb, k_new, v_new, fq_col, fk_row, pool_k, pool_v, pool_lf, page_base, page_table, pps):
    db, nrow, hd = qb.shape
    n_pages = page_table.shape[1]
    assert n_pages % pps == 0

    def per_seq(arr):
        return pl.BlockSpec((None,) + arr.shape[1:], lambda b, s, pt: (b, 0, 0))

    hbm = pl.BlockSpec(memory_space=pl.ANY)
    in_specs = [per_seq(qb), per_seq(k_new), per_seq(v_new), per_seq(fq_col), per_seq(fk_row), hbm, hbm, hbm]
    grid_spec = pltpu.PrefetchScalarGridSpec(
        num_scalar_prefetch=1,
        grid=(db, n_pages // pps),
        in_specs=in_specs,
        out_specs=pl.BlockSpec((None, nrow, hd), lambda b, s, pt: (b, 0, 0)),
        scratch_shapes=[pltpu.VMEM((2, pps) + pool_k.shape[1:], F32), pltpu.VMEM((2, pps) + pool_v.shape[1:], F32),
                        pltpu.VMEM((2, pps) + pool_lf.shape[1:], F32), pltpu.SemaphoreType.DMA((2, 3)),
                        pltpu.VMEM((1, LANE), F32),
                        pltpu.VMEM((nrow, 1), F32), pltpu.VMEM((nrow, 1), F32), pltpu.VMEM((nrow, hd), F32)],
    )
    return pl.pallas_call(
        functools.partial(_fox_sample_kernel, pps=pps, page_base=page_base),
        grid_spec=grid_spec,
        out_shape=jax.ShapeDtypeStruct((db, nrow, hd), F32),
        compiler_params=_params("arbitrary", "arbitrary"),
        name="fox_sample",
    )(page_table, qb, k_new, v_new, fq_col, fk_row, pool_k, pool_v, pool_lf)


def _rope_perm():
    half = MLA_ROPE // 2
    idx = jnp.arange(MLA_ROPE)
    return jnp.where(idx < half, idx + half, idx - half), jnp.where(idx < half, -1.0, 1.0).astype(F32)


def _pad_rope_block(w):
    z = jnp.zeros(w.shape[:-1] + (MLA_NOPE,), w.dtype)
    z2 = jnp.zeros(w.shape[:-1] + (LANE - MLA_NOPE - MLA_ROPE,), w.dtype)
    return jnp.concatenate([z, w, z2], axis=-1)


def _mla_weights(a_w_in, a_w_uq, a_w_ukv):
    perm, _ = _rope_perm()
    o1 = Q_LORA
    o2 = o1 + KV_LORA
    o3 = o2 + MLA_ROPE
    w_kpe = a_w_in[:, o2:o3]
    win = jnp.concatenate([a_w_in[:, :o2], a_w_in[:, o3:], _pad_rope_block(w_kpe), _pad_rope_block(w_kpe[:, perm])],
                          axis=1).astype(BF16)
    wq3 = a_w_uq.reshape(Q_LORA, MLA_HEADS, MLA_NOPE + MLA_ROPE)
    zpad = jnp.zeros((Q_LORA, MLA_HEADS, LANE - MLA_NOPE - MLA_ROPE), F32)
    wq_a = jnp.concatenate([wq3, zpad], axis=-1).reshape(Q_LORA, MLA_HEADS * LANE)
    wq_b = _pad_rope_block(wq3[..., MLA_NOPE:][..., perm]).reshape(Q_LORA, MLA_HEADS * LANE)
    wq = jnp.concatenate([wq_a, wq_b], axis=1).astype(BF16)
    wkv3 = a_w_ukv.reshape(KV_LORA, MLA_HEADS, MLA_NOPE + MLA_V)
    wk = jnp.concatenate([wkv3[..., :MLA_NOPE], jnp.zeros((KV_LORA, MLA_HEADS, LANE - MLA_NOPE), F32)], axis=-1)
    wv = wkv3[..., MLA_NOPE:]
    wkv = jnp.concatenate([wk.reshape(KV_LORA, -1), wv.reshape(KV_LORA, -1)], axis=1).astype(BF16)
    return win, wq, wkv, wkv3


def _rope_tables(pos, qn_g, qr_g, kr_g):
    half = MLA_ROPE // 2
    perm, sign = _rope_perm()
    freq = ROPE_THETA ** (-jnp.arange(half, dtype=F32) / half)
    ang = pos.astype(F32)[:, None] * freq[None, :]
    cos = jnp.concatenate([jnp.cos(ang), jnp.cos(ang)], axis=1)
    sin = jnp.concatenate([jnp.sin(ang), jnp.sin(ang)], axis=1)
    n = pos.shape[0]
    zn = jnp.zeros((n, MLA_NOPE), F32)
    zp = jnp.zeros((n, LANE - MLA_NOPE - MLA_ROPE), F32)
    qs = MLA_SCALE * LOG2E
    gaq = jnp.concatenate([jnp.broadcast_to(qn_g[None, :], (n, MLA_NOPE)), qr_g[None, :] * cos, zp], axis=1) * qs
    gbq = jnp.concatenate([zn, (sign * qr_g[perm])[None, :] * sin, zp], axis=1) * qs
    gak = jnp.concatenate([zn, kr_g[None, :] * cos, zp], axis=1)
    gbk = jnp.concatenate([zn, (sign * kr_g[perm])[None, :] * sin, zp], axis=1)
    return gaq, gbq, gak, gbk


def _s5_params(a_re, a_im, log_dt, b_re, b_im, c_re, c_im, d_skip, w_glu, b_glu):
    g, n = a_re.shape
    dt = jnp.exp(log_dt)[:, None]
    mag = jnp.exp(dt * a_re)
    abar_re, abar_im = mag * jnp.cos(dt * a_im), mag * jnp.sin(dt * a_im)
    den = a_re * a_re + a_im * a_im
    w_re = ((abar_re - 1) * a_re + abar_im * a_im) / den
    w_im = (abar_im * a_re - (abar_re - 1) * a_im) / den
    bbar_re = w_re[..., None] * b_re - w_im[..., None] * b_im
    bbar_im = w_re[..., None] * b_im + w_im[..., None] * b_re
    gh = g // 2
    eye = jnp.eye(gh, dtype=F32)

    def pack_b(bb):
        return jnp.einsum('gnc,gh->gchn', bb, eye).reshape(gh * S5_GROUP, gh * n)

    def pack_c(cc):
        return jnp.einsum('gcn,gh->gnhc', cc, eye).reshape(gh * n, gh * S5_GROUP)

    bd = jnp.stack([jnp.concatenate([pack_b(bbar_re[h * gh:(h + 1) * gh]), pack_b(bbar_im[h * gh:(h + 1) * gh])], axis=1)
                    for h in range(2)])
    cd = jnp.stack([jnp.concatenate([pack_c(c_re[h * gh:(h + 1) * gh]), -pack_c(c_im[h * gh:(h + 1) * gh])], axis=0)
                    for h in range(2)])
    bd_hi = bd.astype(BF16)
    bd_lo = (bd - bd_hi.astype(F32)).astype(BF16)
    return (abar_re.reshape(1, g * n), abar_im.reshape(1, g * n), bd_hi, bd_lo, cd.astype(BF16),
            d_skip.reshape(1, -1), w_glu.astype(BF16), b_glu.reshape(1, -1))


def kernel(x_prompt, x_sample, c_prompt, c_sample, cache_mla_ckv, cache_mla_kpe, state_s5_re, state_s5_im,
           cache_fox_k, cache_fox_v, cache_fox_logf, page_table, w_ada, b_ada, norm_g, ffn_w1, ffn_w3, ffn_w2,
           a_w_in, a_q_norm, a_kv_norm, a_w_uq, a_w_ukv, a_qn_norm, a_qr_norm, a_kn_norm, a_kr_norm,
           s5_a_re, s5_a_im, s5_log_dt, s5_b_re, s5_b_im, s5_c_re, s5_c_im, s5_d, s5_w_glu, s5_b_glu,
           a_w_out, c_w_in, c_b_f, c_q_norm, c_k_norm, c_w_out):
    B, L, D = x_prompt.shape
    DB, DS, _ = x_sample.shape
    RS = DB * DS
    n_pages = page_table.shape[1]
    n_past = n_pages * PAGE
    depth = w_ada.shape[0]

    tm_p = min(512, L)
    tm_ffn = min(1024, L)
    tf = 256
    tq = min(1024, L)
    tk = min(512, L)

    m_all = ada_modulation(jnp.concatenate([c_prompt, c_sample], axis=0), w_ada, b_ada)
    w1b, w3b, w2b = ffn_w1.astype(BF16), ffn_w3.astype(BF16), ffn_w2.astype(BF16)

    xp = x_prompt
    xs = x_sample.reshape(1, RS, D)
    outs_p = {}
    outs_s = {}
    for i in range(depth):
        mp = m_all[i, :B].reshape(B, 3, 3, 1, D)
        ms = jnp.repeat(m_all[i, B:].reshape(DB, 3, 3, D), DS, axis=0).reshape(1, RS, 3, 3, D)

        def mod_p(s, k):
            return mp[:, s, k]

        def mod_s(s, k):
            return ms[:, :, s, k]

        g = norm_g[i]
        xp = ffn_sublayer(xp, g[0], mod_p(0, 0), mod_p(0, 1), mod_p(0, 2), w1b, w3b, w2b, i, 0, tm_ffn, tf)
        xs = ffn_sublayer(xs, g[0], mod_s(0, 0), mod_s(0, 1), mod_s(0, 2), w1b, w3b, w2b, i, 0, RS, tf)
        j = i // 2
        g1 = g[1].reshape(1, D)
        if i % 2 == 0:
            win, wq, wkv, wkv3 = _mla_weights(a_w_in[j], a_w_uq[j], a_w_ukv[j])
            gq = a_q_norm[j].reshape(1, -1)
            gkv = a_kv_norm[j].reshape(1, -1)
            gkn = jnp.concatenate([a_kn_norm[j], jnp.zeros((LANE - MLA_NOPE,), F32)]).reshape(1, LANE)
            s5p = _s5_params(s5_a_re[j], s5_a_im[j], s5_log_dt[j], s5_b_re[j], s5_b_im[j], s5_c_re[j], s5_c_im[j],
                             s5_d[j], s5_w_glu[j], s5_b_glu[j])
            wo = a_w_out[j].astype(BF16)
            hv = MLA_HEADS * MLA_V
            wo_att, wo_ssm = wo[:hv], wo[hv:]
            cw = s5_d.shape[-1]
            ns = s5_a_re.shape[1] * s5_a_re.shape[2]

            tabs = _rope_tables(jnp.arange(L), a_qn_norm[j], a_qr_norm[j], a_kr_norm[j])
            ckv, kpe, u_tm, qf, kf, vb = mla_s5_project(xp, g1, mod_p(1, 0), mod_p(1, 1), tabs, gq, gkv, gkn,
                                                        win, wq, wkv, tm_p)
            att = flash_prompt(qf, kf, vb, None, None, 2, MLA_V, tq, tk)
            zeros = jnp.zeros((B, ns), F32)
            ssm, sr, si = s5_mix(u_tm.reshape(L, B, cw), zeros, zeros, s5p, min(128, L), False)
            ssm2 = ssm.reshape(L, B * cw)
            mix_p = (mod_p(1, 2), [att, ssm2],
                     [lambda tm, w=hv: pl.BlockSpec((None, tm, w), lambda b, l: (b, l, 0)),
                      lambda tm, w=cw: pl.BlockSpec((tm, w), lambda b, l: (l, b))],
                     [wo_att, wo_ssm])
            outs_p.setdefault('ckv', []).append(ckv)
            outs_p.setdefault('kpe', []).append(kpe)
            outs_p.setdefault('s5r', []).append(sr.reshape(B, -1, S5_STATE))
            outs_p.setdefault('s5i', []).append(si.reshape(B, -1, S5_STATE))

            pos_s = n_past + jnp.tile(jnp.arange(DS), DB)
            tabs = _rope_tables(pos_s, a_qn_norm[j], a_qr_norm[j], a_kr_norm[j])
            ckv, kpe, u_s, qf, kf, vb = mla_s5_project(xs, g1, mod_s(1, 0), mod_s(1, 1), tabs, gq, gkv, gkn,
                                                       win, wq, wkv, RS)
            wuk = wkv3[..., :MLA_NOPE] * a_kn_norm[j][None, None, :]
            wabs = jnp.transpose(wuk, (1, 2, 0))
            top = jnp.concatenate([wabs, jnp.zeros((MLA_HEADS, MLA_NOPE, LANE), F32)], axis=-1)
            eye_blk = jnp.concatenate([jnp.zeros((MLA_ROPE, KV_LORA), F32), jnp.eye(MLA_ROPE, LANE, dtype=F32)], axis=-1)
            mid = jnp.broadcast_to(eye_blk[None], (MLA_HEADS, MLA_ROPE, KV_LORA + LANE))
            bot = jnp.zeros((MLA_HEADS, LANE - MLA_NOPE - MLA_ROPE, KV_LORA + LANE), F32)
            wfull = jnp.concatenate([top, mid, bot], axis=1).astype(BF16)
            wukt = jnp.transpose(wkv3[..., :MLA_NOPE], (1, 2, 0)).reshape(MLA_HEADS * MLA_NOPE, KV_LORA).astype(BF16)
            wuv = jnp.einsum('khd,hg->hkgd', wkv3[..., MLA_NOPE:], jnp.eye(MLA_HEADS, dtype=F32))
            wuv = wuv.reshape(MLA_HEADS, KV_LORA, hv).astype(BF16)
            kpe_t = jnp.transpose(kpe.reshape(DB, DS, MLA_ROPE), (0, 2, 1))
            kpe_t = jnp.concatenate([kpe_t, jnp.zeros((DB, MLA_ROPE, PAGE - DS), F32)], axis=-1)
            pool_ckv = cache_mla_ckv.reshape(-1, PAGE, KV_LORA)
            pool_kpe = jnp.swapaxes(cache_mla_kpe, 2, 3).reshape(-1, MLA_ROPE, PAGE)
            att_s = mla_attend_sample(qf.reshape(DB, DS, -1), ckv.reshape(DB, DS, -1), kpe_t,
                                      pool_ckv, pool_kpe, j * cache_mla_ckv.shape[1], page_table,
                                      wfull, wukt, wuv, min(32, n_pages))
            u3 = jnp.transpose(u_s.reshape(DB, DS, cw), (1, 0, 2))
            ssm, sr, si = s5_mix(u3, state_s5_re[j].reshape(DB, ns), state_s5_im[j].reshape(DB, ns), s5p, DS, True)
            ssm_s = jnp.transpose(ssm, (1, 0, 2)).reshape(1, RS, cw)
            mix_s = (mod_s(1, 2), [att_s.reshape(1, RS, hv), ssm_s],
                     [lambda tm, w=hv: pl.BlockSpec((None, tm, w), lambda b, l: (b, l, 0)),
                      lambda tm, w=cw: pl.BlockSpec((None, tm, w), lambda b, l: (b, l, 0))],
                     [wo_att, wo_ssm])
            outs_s.setdefault('ckv', []).append(ckv.reshape(DB, DS, -1))
            outs_s.setdefault('kpe', []).append(kpe.reshape(DB, DS, -1))
            outs_s.setdefault('s5r', []).append(sr.reshape(DB, -1, S5_STATE))
            outs_s.setdefault('s5i', []).append(si.reshape(DB, -1, S5_STATE))
        else:
            hw = FOX_HEADS * FOX_HD
            wf = jnp.concatenate([jnp.swapaxes(c_w_in[j], 0, 1), jnp.zeros((LANE - FOX_HEADS, D), F32)], axis=0)
            wf = wf.astype(BF16)
            gqf = (c_q_norm[j] * (FOX_SCALE * LOG2E)).reshape(1, FOX_HD)
            gkf = c_k_norm[j].reshape(1, FOX_HD)
            bf = jnp.concatenate([c_b_f[j], jnp.zeros((LANE - FOX_HEADS,), F32)]).reshape(1, LANE)
            wo = c_w_out[j].astype(BF16)

            k32, v32, lf, qb, kb, vb, fc = fox_project(xp, g1, mod_p(1, 0), mod_p(1, 1), gqf, gkf, bf, wf, L, tm_p)
            fcs = jnp.transpose(fc[:, :, :FOX_HEADS], (0, 2, 1)) * LOG2E
            fk = fcs.reshape(B, FOX_HEADS, L // tk, 1, tk)
            o = flash_prompt(qb, kb, vb, fc, fk, 2, FOX_HD, tq, tk)
            mix_p = (mod_p(1, 2), [o], [lambda tm, w=hw: pl.BlockSpec((None, tm, w), lambda b, l: (b, l, 0))], [wo])
            outs_p.setdefault('fk', []).append(k32.reshape(B, L, FOX_HEADS, FOX_HD))
            outs_p.setdefault('fv', []).append(v32.reshape(B, L, FOX_HEADS, FOX_HD))
            outs_p.setdefault('flf', []).append(lf)

            k32, v32, lf, qb, kb, vb, fc = fox_project(xs, g1, mod_s(1, 0), mod_s(1, 1), gqf, gkf, bf, wf, DS, RS)
            nr = DS * FOX_HEADS
            f_new = fc[0, :, :FOX_HEADS].reshape(DB, nr) * LOG2E
            fq_col = f_new[..., None]
            fk_row = jnp.concatenate([f_new, jnp.zeros((DB, LANE - nr), F32)], axis=-1)[:, None, :]
            n_pool = cache_fox_k.shape[1]
            pool_k = cache_fox_k.reshape(-1, PAGE * FOX_HEADS, FOX_HD)
            pool_v = cache_fox_v.reshape(-1, PAGE * FOX_HEADS, FOX_HD)
            pool_lf = cache_fox_logf.reshape(-1, PAGE * FOX_HEADS // LANE, LANE)
            o = fox_attend_sample(qb.reshape(DB, nr, FOX_HD), k32.reshape(DB, nr, FOX_HD), v32.reshape(DB, nr, FOX_HD),
                                  fq_col, fk_row, pool_k, pool_v, pool_lf, j * n_pool, page_table, 16)
            mix_s = (mod_s(1, 2), [o.reshape(1, RS, hw)],
                     [lambda tm, w=hw: pl.BlockSpec((None, tm, w), lambda b, l: (b, l, 0))], [wo])
            outs_s.setdefault('fk', []).append(k32.reshape(DB, DS, FOX_HEADS, FOX_HD))
            outs_s.setdefault('fv', []).append(v32.reshape(DB, DS, FOX_HEADS, FOX_HD))
            outs_s.setdefault('flf', []).append(lf.reshape(DB, DS, FOX_HEADS))
        xp = ffn_sublayer(xp, g[2], mod_p(2, 0), mod_p(2, 1), mod_p(2, 2), w1b, w3b, w2b, i, 1, tm_ffn, tf, mix_p)
        xs = ffn_sublayer(xs, g[2], mod_s(2, 0), mod_s(2, 1), mod_s(2, 2), w1b, w3b, w2b, i, 1, RS, tf, mix_s)

    def st(d, key):
        return jnp.stack(d[key])

    return (xp, xs.reshape(DB, DS, D),
            st(outs_p, 'ckv'), st(outs_p, 'kpe'), st(outs_p, 's5r'), st(outs_p, 's5i'),
            st(outs_p, 'fk'), st(outs_p, 'fv'), st(outs_p, 'flf'),
            st(outs_s, 'ckv'), st(outs_s, 'kpe'), st(outs_s, 's5r'), st(outs_s, 's5i'),
            st(outs_s, 'fk'), st(outs_s, 'fv'), st(outs_s, 'flf'))
```

```python
import functools
import math

import jax
import jax.numpy as jnp
from jax import lax
from jax.experimental import pallas as pl
from jax.experimental.pallas import tpu as pltpu

F32 = jnp.float32
BF16 = jnp.bfloat16

LANE = 128
VMEM_LIMIT_BYTES = 56 * 1024 * 1024

RMS_EPS = 1e-6
ROPE_THETA = 10000.0
NEG_BIG = -1e30
LOG2E = math.log2(math.e)

MLA_HEADS = 8
MLA_NOPE = 64
MLA_ROPE = 32
MLA_V = 64
MLA_SCALE = (MLA_NOPE + MLA_ROPE) ** -0.5
Q_LORA = 384
KV_LORA = 256
S5_GROUP = 16
S5_STATE = 64
FOX_HEADS = 8
FOX_HD = 128
FOX_SCALE = FOX_HD ** -0.5
PAGE = 128
FOX_SUB = 8
CUM_BLOCK = 128

NT_DIMS = (((1,), (1,)), ((), ()))


def _params(*sem):
    return pltpu.CompilerParams(dimension_semantics=sem, vmem_limit_bytes=VMEM_LIMIT_BYTES)


def _dot(a, b):
    return jnp.dot(a, b, preferred_element_type=F32)


def _dot_nt(a, b):
    return lax.dot_general(a, b, NT_DIMS, preferred_element_type=F32)


def _split3(x):
    hi = x.astype(BF16)
    r1 = x - hi.astype(F32)
    mid = r1.astype(BF16)
    lo = (r1 - mid.astype(F32)).astype(BF16)
    return hi, mid, lo


def _rms(x, n):
    return lax.rsqrt(jnp.sum(x * x, axis=-1, keepdims=True) * (1.0 / n) + RMS_EPS)


def _modulate(x, g, scale, shift):
    return (x * _rms(x, x.shape[-1]) * g) * (1.0 + scale) + shift


def _silu(x):
    return x * jax.nn.sigmoid(x)


def _ada_kernel(c_ref, w_ref, b_ref, o_ref):
    a = _silu(c_ref[...]).astype(BF16)
    o_ref[...] = _dot(a, w_ref[...].astype(BF16)) + b_ref[...]


def ada_modulation(c_all, w_ada, b_ada):
    depth, d, n = w_ada.shape
    m = c_all.shape[0]
    tn = 1024
    return pl.pallas_call(
        _ada_kernel,
        grid=(depth, n // tn),
        in_specs=[
            pl.BlockSpec((m, d), lambda i, j: (0, 0)),
            pl.BlockSpec((None, d, tn), lambda i, j: (i, 0, j)),
            pl.BlockSpec((None, 1, tn), lambda i, j: (i, 0, j)),
        ],
        out_specs=pl.BlockSpec((None, m, tn), lambda i, j: (i, 0, j)),
        out_shape=jax.ShapeDtypeStruct((depth, m, n), F32),
        compiler_params=_params("parallel", "parallel"),
        name="ada",
    )(c_all, w_ada, b_ada.reshape(depth, 1, n))


def _row_spec(arr, tm):
    return pl.BlockSpec((None, tm, arr.shape[-1]), lambda b, l: (b, l, 0))


def _mod_spec(arr, tm):
    if arr.shape[1] == 1:
        return pl.BlockSpec((None, 1, arr.shape[-1]), lambda b, l: (b, 0, 0))
    return pl.BlockSpec((None, tm, arr.shape[-1]), lambda b, l: (b, l, 0))


def _full_spec(arr):
    nd = arr.ndim
    return pl.BlockSpec(arr.shape, lambda *_: (0,) * nd)


def _ffn_kernel(x_ref, g_ref, sh_ref, sc_ref, gt_ref, w1_ref, w3_ref, w2_ref, *rest, tf, n_mix):
    o_ref = rest[-1]
    x = x_ref[...]
    if n_mix:
        gm_ref = rest[0]
        a_refs = rest[1:1 + n_mix]
        wo_refs = rest[1 + n_mix:1 + 2 * n_mix]
        mix = _dot(a_refs[0][...].astype(BF16), wo_refs[0][...])
        for a_ref, wo_ref in zip(a_refs[1:], wo_refs[1:]):
            mix = mix + _dot(a_ref[...].astype(BF16), wo_ref[...])
        x = x + gm_ref[...] * mix
    h = _modulate(x, g_ref[...], sc_ref[...], sh_ref[...]).astype(BF16)
    acc = None
    for c in range(w1_ref.shape[-1] // tf):
        cs = slice(c * tf, (c + 1) * tf)
        t = (_silu(_dot(h, w1_ref[:, cs])) * _dot(h, w3_ref[:, cs])).astype(BF16)
        part = _dot(t, w2_ref[cs, :])
        acc = part if acc is None else acc + part
    o_ref[...] = x + 0.5 * gt_ref[...] * acc


def ffn_sublayer(x, g, shift, scale, gate, w1, w3, w2, li, half, tm, tf, mix=None):
    nb, r, d = x.shape
    f = w1.shape[-1]

    def resident(shape, idx):
        return pl.BlockSpec(shape, lambda b, l: idx, pipeline_mode=pl.Buffered(1))

    in_specs = [_row_spec(x, tm), pl.BlockSpec((1, d), lambda b, l: (0, 0)),
                _mod_spec(shift, tm), _mod_spec(scale, tm), _mod_spec(gate, tm),
                resident((None, None, d, f), (li, half, 0, 0)), resident((None, None, d, f), (li, half, 0, 0)),
                resident((None, None, f, d), (li, half, 0, 0))]
    args = [x, g.reshape(1, d), shift, scale, gate, w1, w3, w2]
    n_mix = 0
    if mix is not None:
        gate_mix, acts, spec_fns, wos = mix
        n_mix = len(acts)
        in_specs += [_mod_spec(gate_mix, tm)] + [fn(tm) for fn in spec_fns]
        in_specs += [resident(w.shape, (0,) * w.ndim) for w in wos]
        args += [gate_mix] + list(acts) + list(wos)
    return pl.pallas_call(
        functools.partial(_ffn_kernel, tf=tf, n_mix=n_mix),
        grid=(nb, r // tm),
        in_specs=in_specs,
        out_specs=_row_spec(x, tm),
        out_shape=jax.ShapeDtypeStruct(x.shape, F32),
        compiler_params=_params("parallel", "parallel"),
        name="ffn_mix" if n_mix else "ffn",
    )(*args)


def _aproj_kernel(x_ref, g_ref, sh_ref, sc_ref, gaq_ref, gbq_ref, gak_ref, gbk_ref, gq_ref, gkv_ref, gkn_ref,
                  win_ref, wq_ref, wkv_ref, ckv_ref, kpe_ref, u_ref, qf_ref, kf_ref, v_ref):
    h = _modulate(x_ref[...], g_ref[...], sc_ref[...], sh_ref[...]).astype(BF16)
    proj = _dot(h, win_ref[...])
    o1 = Q_LORA
    o2 = o1 + KV_LORA
    o3 = o2 + 512
    cq = proj[:, :o1]
    cqn = (cq * _rms(cq, Q_LORA) * gq_ref[...]).astype(BF16)
    ckv_raw = proj[:, o1:o2]
    ckv = ckv_raw * _rms(ckv_raw, KV_LORA) * gkv_ref[...]
    ckv_ref[...] = ckv
    u_ref[...] = proj[:, o2:o3]
    ka = proj[:, o3:o3 + LANE]
    kb = proj[:, o3 + LANE:o3 + 2 * LANE]
    kpe_blk = _rms(ka, MLA_ROPE) * (ka * gak_ref[...] + kb * gbk_ref[...])
    kpe_ref[...] = kpe_blk[:, MLA_NOPE:MLA_NOPE + MLA_ROPE]

    qraw = _dot(cqn, wq_ref[...])
    kv = _dot(ckv.astype(BF16), wkv_ref[...])
    lane = lax.broadcasted_iota(jnp.int32, (1, LANE), 1)
    is_n = lane < MLA_NOPE
    is_p = jnp.logical_and(lane >= MLA_NOPE, lane < MLA_NOPE + MLA_ROPE)
    gaq = gaq_ref[...]
    gbq = gbq_ref[...]
    gkn = gkn_ref[...]
    hw = MLA_HEADS * LANE
    for hh in range(MLA_HEADS):
        sl = slice(hh * LANE, (hh + 1) * LANE)
        a = qraw[:, sl]
        b = qraw[:, hw + hh * LANE:hw + (hh + 1) * LANE]
        sq = a * a
        msn = jnp.sum(jnp.where(is_n, sq, 0.0), axis=-1, keepdims=True) * (1.0 / MLA_NOPE)
        msp = jnp.sum(jnp.where(is_p, sq, 0.0), axis=-1, keepdims=True) * (1.0 / MLA_ROPE)
        r = jnp.where(is_n, lax.rsqrt(msn + RMS_EPS), lax.rsqrt(msp + RMS_EPS))
        qf_ref[:, sl] = (r * (a * gaq + b * gbq)).astype(BF16)
        kk = kv[:, sl]
        kf_ref[:, sl] = (kk * _rms(kk, MLA_NOPE) * gkn + kpe_blk).astype(BF16)
    v_ref[...] = kv[:, hw:].astype(BF16)


def mla_s5_project(x, g, shift, scale, tabs, gq, gkv, gkn, win, wq, wkv, tm):
    nb, r, d = x.shape
    gaq, gbq, gak, gbk = tabs
    tab_spec = pl.BlockSpec((tm, LANE), lambda b, l: (l, 0))
    hw = MLA_HEADS * LANE
    hv = MLA_HEADS * MLA_V
    u_w = win.shape[1] - Q_LORA - KV_LORA - 2 * LANE
    out_shape = [
        jax.ShapeDtypeStruct((nb, r, KV_LORA), F32),
        jax.ShapeDtypeStruct((nb, r, MLA_ROPE), F32),
        jax.ShapeDtypeStruct((r, nb * u_w), F32),
        jax.ShapeDtypeStruct((nb, r, hw), BF16),
        jax.ShapeDtypeStruct((nb, r, hw), BF16),
        jax.ShapeDtypeStruct((nb, r, hv), BF16),
    ]

    def rs(c):
        return pl.BlockSpec((None, tm, c), lambda b, l: (b, l, 0))

    out_specs = [rs(KV_LORA), rs(MLA_ROPE), pl.BlockSpec((tm, u_w), lambda b, l: (l, b)), rs(hw), rs(hw), rs(hv)]
    return pl.pallas_call(
        _aproj_kernel,
        grid=(nb, r // tm),
        in_specs=[_row_spec(x, tm), _full_spec(g), _mod_spec(shift, tm), _mod_spec(scale, tm),
                  tab_spec, tab_spec, tab_spec, tab_spec, _full_spec(gq), _full_spec(gkv), _full_spec(gkn),
                  _full_spec(win), _full_spec(wq), _full_spec(wkv)],
        out_specs=out_specs,
        out_shape=out_shape,
        compiler_params=_params("parallel", "parallel"),
        name="mla_s5_project",
    )(x, g, shift, scale, gaq, gbq, gak, gbk, gq, gkv, gkn, win, wq, wkv)


def _lane_fold(x, op):
    out = x[:, :LANE]
    for c in range(1, x.shape[1] // LANE):
        out = op(out, x[:, c * LANE:(c + 1) * LANE])
    return out


def _flash_kernel(*refs, hp, dv, has_bias, tq, tk):
    if has_bias:
        q_ref, k_ref, v_ref, fq_ref, fk_ref, o_ref, mr_scr, lp_scr, acc_scr = refs
    else:
        q_ref, k_ref, v_ref, o_ref, mr_scr, lp_scr, acc_scr = refs
    qi = pl.program_id(2)
    nd = tq // tk
    heads = range(hp)
    qs = [q_ref[:, i * LANE:(i + 1) * LANE] for i in heads]
    fqb = None
    if has_bias:
        fblk = fq_ref[...] * LOG2E
        lane_id = lax.broadcasted_iota(jnp.int32, (1, LANE), 1)
        fqb = []
        for i in heads:
            pick = lane_id == pl.program_id(1) * hp + i
            colv = jnp.sum(jnp.where(pick, fblk, 0.0), axis=-1, keepdims=True)
            fqb.append(jnp.broadcast_to(colv, (tq, tk)))

    def keys(i, j):
        return k_ref[pl.ds(pl.multiple_of(j * tk, tk), tk), i * LANE:(i + 1) * LANE]

    def raw(i, j):
        return _dot_nt(qs[i], keys(i, j))

    def vals(i, j):
        g = (i * dv) // LANE
        return v_ref[pl.ds(pl.multiple_of(j * tk, tk), tk), g * LANE:(g + 1) * LANE]

    def scores(i, j):
        s = raw(i, j)
        if has_bias:
            s = s + fqb[i] - fk_ref[i, j]
        return s

    def diag_chunk(i, d):
        r0 = d * tk
        j = qi * nd + d
        s = _dot_nt(qs[i][r0:, :], keys(i, j))
        if has_bias:
            s = s + fqb[i][r0:, :] - fk_ref[i, j]
        rows = tq - r0
        keep = lax.broadcasted_iota(jnp.int32, (rows, tk), 1) <= lax.broadcasted_iota(jnp.int32, (rows, tk), 0)
        return jnp.where(keep, s, NEG_BIG)

    def merge_rows(full, part, r0, op):
        if full is None:
            return part
        if r0 == 0:
            return op(full, part)
        return jnp.concatenate([full[:r0], op(full[r0:], part)], axis=0)

    s_d = [[diag_chunk(i, d) for d in range(nd)] for i in heads]
    for i in heads:
        mr = None
        for d in range(nd):
            mr = merge_rows(mr, _lane_fold(s_d[i][d], jnp.maximum), d * tk, jnp.maximum)
        mr_scr[i] = mr

    def pass1(j, c):
        for i in heads:
            mr_scr[i] = jnp.maximum(mr_scr[i], _lane_fold(scores(i, j), jnp.maximum))
        return c

    lax.fori_loop(0, qi * nd, pass1, 0)
    shift = []
    for i in heads:
        mr = mr_scr[i]
        m = jnp.max(mr, axis=-1, keepdims=True)
        acc = None
        lp = None
        for d in range(nd):
            m_d = m if d == 0 else jnp.max(mr[d * tk:, :], axis=-1, keepdims=True)
            p_d = jnp.exp2(s_d[i][d] - m_d)
            acc = merge_rows(acc, _dot(p_d.astype(BF16), vals(i, qi * nd + d)), d * tk, jnp.add)
            lp = merge_rows(lp, _lane_fold(p_d, jnp.add), d * tk, jnp.add)
        acc_scr[i] = acc
        lp_scr[i] = lp
        shift.append((fqb[i] - m) if has_bias else jnp.broadcast_to(m, (tq, tk)))

    def pass2(j, c):
        for i in heads:
            if has_bias:
                p = jnp.exp2(raw(i, j) + shift[i] - fk_ref[i, j])
            else:
                p = jnp.exp2(raw(i, j) - shift[i])
            acc_scr[i] += _dot(p.astype(BF16), vals(i, j))
            lp_scr[i] += _lane_fold(p, jnp.add)
        return c

    lax.fori_loop(0, qi * nd, pass2, 0)
    outs = [acc_scr[i] / jnp.sum(lp_scr[i], axis=-1, keepdims=True) for i in heads]
    lane = lax.broadcasted_iota(jnp.int32, (1, LANE), 1)
    per_group = LANE // dv
    groups = []
    for g in range(hp // per_group):
        out = outs[g * per_group]
        for t in range(1, per_group):
            out = jnp.where(lane >= t * dv, outs[g * per_group + t], out)
        groups.append(out)
    o_ref[...] = (groups[0] if len(groups) == 1 else jnp.concatenate(groups, axis=1)).astype(o_ref.dtype)


def flash_prompt(q, k, v, fq, fk, hp, dv, tq, tk):
    b, l, hw = q.shape
    nh = hw // LANE
    assert (hp * dv) % LANE == 0 and LANE % dv == 0 and l % tq == 0 and tq % tk == 0
    has_bias = fq is not None
    in_specs = [
        pl.BlockSpec((None, tq, hp * LANE), lambda bb, h, i: (bb, i, h)),
        pl.BlockSpec((None, l, hp * LANE), lambda bb, h, i: (bb, 0, h)),
        pl.BlockSpec((None, l, hp * dv), lambda bb, h, i: (bb, 0, h)),
    ]
    args = [q, k, v]
    if has_bias:
        in_specs += [
            pl.BlockSpec((None, tq, LANE), lambda bb, h, i: (bb, i, 0)),
            pl.BlockSpec((None, hp, l // tk, 1, tk), lambda bb, h, i: (bb, h, 0, 0, 0)),
        ]
        args += [fq, fk]
    return pl.pallas_call(
        functools.partial(_flash_kernel, hp=hp, dv=dv, has_bias=has_bias, tq=tq, tk=tk),
        grid=(b, nh // hp, l // tq),
        in_specs=in_specs,
        out_specs=pl.BlockSpec((None, tq, hp * dv), lambda bb, h, i: (bb, i, h)),
        out_shape=jax.ShapeDtypeStruct((b, l, nh * dv), BF16),
        scratch_shapes=[pltpu.VMEM((hp, tq, LANE), F32), pltpu.VMEM((hp, tq, LANE), F32),
                        pltpu.VMEM((hp, tq, LANE), F32)],
        compiler_params=_params("parallel", "parallel", "arbitrary"),
        name="flash_bias" if has_bias else "flash",
    )(*args)


def _gelu_tanh(x):
    c = math.sqrt(2.0 / math.pi)
    return x * (0.5 * (1.0 + jnp.tanh(c * (x + 0.044715 * (x * x * x)))))


def _s5_kernel(u_ref, h0r_ref, h0i_ref, ar_ref, ai_ref, bd_ref, bdlo_ref, cd_ref, d_ref, wg_ref, bg_ref,
               y_ref, xr_ref, xi_ref, sr, si, xr_scr, xi_scr, *, t_blk, bt, precise):
    i = pl.program_id(0)
    rows = t_blk * bt
    ns = xr_scr.shape[-1]
    nh = ns // 2
    ch = u_ref.shape[-1] // 2

    @pl.when(i == 0)
    def _():
        xr_scr[...] = h0r_ref[...]
        xi_scr[...] = h0i_ref[...]

    u = u_ref[...].reshape(rows, 2 * ch)
    ub = u.astype(BF16)
    for hf in range(2):
        uh = ub[:, hf * ch:(hf + 1) * ch]
        bu = _dot(uh, bd_ref[hf])
        if precise:
            ulo = (u[:, hf * ch:(hf + 1) * ch] - uh.astype(F32)).astype(BF16)
            bu = bu + _dot(ulo, bd_ref[hf]) + _dot(uh, bdlo_ref[hf])
        sr[:, hf * nh:(hf + 1) * nh] = bu[:, :nh]
        si[:, hf * nh:(hf + 1) * nh] = bu[:, nh:]

    for hf in range(2):
        cs = slice(hf * nh, (hf + 1) * nh)
        ar = jnp.broadcast_to(ar_ref[:, cs], (bt, nh))
        ai = jnp.broadcast_to(ai_ref[:, cs], (bt, nh))

        def step(t, carry, cs=cs, ar=ar, ai=ai):
            xr, xi = carry
            o = pl.multiple_of(t * bt, bt)
            nr = ar * xr - ai * xi + sr[pl.ds(o, bt), cs]
            ni = ar * xi + ai * xr + si[pl.ds(o, bt), cs]
            sr[pl.ds(o, bt), cs] = nr
            si[pl.ds(o, bt), cs] = ni
            return nr, ni

        xr, xi = lax.fori_loop(0, t_blk, step, (xr_scr[:, cs], xi_scr[:, cs]))
        xr_scr[:, cs] = xr
        xi_scr[:, cs] = xi

    ys = []
    for hf in range(2):
        cs = slice(hf * nh, (hf + 1) * nh)
        ys.append(_dot(sr[:, cs].astype(BF16), cd_ref[hf, :nh, :]) + _dot(si[:, cs].astype(BF16), cd_ref[hf, nh:, :]))
    y = jnp.concatenate(ys, axis=1)
    y = _gelu_tanh(y + d_ref[...] * u)
    y = y * jax.nn.sigmoid(_dot(y.astype(BF16), wg_ref[...]) + bg_ref[...])
    y_ref[...] = y.reshape(t_blk, bt, 2 * ch)

    @pl.when(i == pl.num_programs(0) - 1)
    def _():
        xr_ref[...] = xr_scr[...]
        xi_ref[...] = xi_scr[...]


def s5_mix(u3, h0r, h0i, prm, t_blk, precise):
    t, bt, c = u3.shape
    ns = h0r.shape[-1]
    rows = t_blk * bt
    ar, ai, bd, bdlo, cd, d, wg, bg = prm
    blk = pl.BlockSpec((t_blk, bt, c), lambda i: (i, 0, 0))
    ins = [u3, h0r, h0i, ar, ai, bd, bdlo, cd, d, wg, bg]
    return pl.pallas_call(
        functools.partial(_s5_kernel, t_blk=t_blk, bt=bt, precise=precise),
        grid=(t // t_blk,),
        in_specs=[blk] + [_full_spec(a) for a in ins[1:]],
        out_specs=[blk, _full_spec(h0r), _full_spec(h0i)],
        out_shape=[jax.ShapeDtypeStruct((t, bt, c), F32), jax.ShapeDtypeStruct(h0r.shape, F32),
                   jax.ShapeDtypeStruct(h0i.shape, F32)],
        scratch_shapes=[pltpu.VMEM((rows, ns), F32), pltpu.VMEM((rows, ns), F32),
                        pltpu.VMEM((bt, ns), F32), pltpu.VMEM((bt, ns), F32)],
        compiler_params=_params("arbitrary"),
        name="s5",
    )(*ins)


def _foxproj_kernel(x_ref, g_ref, sh_ref, sc_ref, gq_ref, gk_ref, bf_ref, w_ref,
                    k_ref, v_ref, lf_ref, qb_ref, kb_ref, vb_ref, fc_ref, carry_scr, *, seg, tm):
    l_idx = pl.program_id(1)
    h = _modulate(x_ref[...], g_ref[...], sc_ref[...], sh_ref[...]).astype(BF16)
    proj = _dot_nt(h, w_ref[...])
    hw = FOX_HEADS * FOX_HD
    gq = gq_ref[...]
    gk = gk_ref[...]
    for hh in range(FOX_HEADS):
        sl = slice(hh * FOX_HD, (hh + 1) * FOX_HD)
        qh = proj[:, sl]
        qb_ref[:, sl] = (qh * _rms(qh, FOX_HD) * gq).astype(BF16)
        kh = proj[:, hw + hh * FOX_HD:hw + (hh + 1) * FOX_HD]
        kn = kh * _rms(kh, FOX_HD) * gk
        k_ref[:, sl] = kn
        kb_ref[:, sl] = kn.astype(BF16)
    vv = proj[:, 2 * hw:3 * hw]
    v_ref[...] = vv
    vb_ref[...] = vv.astype(BF16)
    z = proj[:, 3 * hw:] + bf_ref[...]
    lf = jnp.minimum(z, 0.0) - jnp.log1p(jnp.exp(-jnp.abs(z)))
    lf_ref[...] = lf[:, :FOX_HEADS]
    sub = min(CUM_BLOCK, tm)
    row = lax.broadcasted_iota(jnp.int32, (sub, sub), 0)
    col = lax.broadcasted_iota(jnp.int32, (sub, sub), 1)
    keep = col <= row
    if seg < sub:
        keep = jnp.logical_and(keep, (col // seg) == (row // seg))
    tri = jnp.where(keep, 1.0, 0.0).astype(BF16)
    carry = None
    if seg > tm:
        @pl.when(l_idx == 0)
        def _():
            carry_scr[...] = jnp.zeros_like(carry_scr)

        carry = carry_scr[...]
    for c in range(tm // sub):
        hi, mid, lo = _split3(lf[c * sub:(c + 1) * sub, :])
        cs = _dot(tri, hi) + _dot(tri, mid) + _dot(tri, lo)
        if carry is not None:
            cs = cs + carry
            carry = cs[sub - 1:sub, :]
        fc_ref[c * sub:(c + 1) * sub, :] = cs
    if carry is not None:
        carry_scr[...] = carry


def fox_project(x, g, shift, scale, gq, gk, bf, w, seg, tm):
    nb, r, d = x.shape
    hw = FOX_HEADS * FOX_HD
    assert seg > tm or min(CUM_BLOCK, tm) % seg == 0

    def rs(c):
        return pl.BlockSpec((None, tm, c), lambda b, l: (b, l, 0))

    out_shape = [
        jax.ShapeDtypeStruct((nb, r, hw), F32), jax.ShapeDtypeStruct((nb, r, hw), F32),
        jax.ShapeDtypeStruct((nb, r, FOX_HEADS), F32),
        jax.ShapeDtypeStruct((nb, r, hw), BF16), jax.ShapeDtypeStruct((nb, r, hw), BF16),
        jax.ShapeDtypeStruct((nb, r, hw), BF16), jax.ShapeDtypeStruct((nb, r, LANE), F32),
    ]
    out_specs = [rs(hw), rs(hw), rs(FOX_HEADS), rs(hw), rs(hw), rs(hw), rs(LANE)]
    return pl.pallas_call(
        functools.partial(_foxproj_kernel, seg=seg, tm=tm),
        grid=(nb, r // tm),
        in_specs=[_row_spec(x, tm), _full_spec(g), _mod_spec(shift, tm), _mod_spec(scale, tm),
                  _full_spec(gq), _full_spec(gk), _full_spec(bf), _full_spec(w)],
        out_specs=out_specs,
        out_shape=out_shape,
        scratch_shapes=[pltpu.VMEM((1, LANE), F32)],
        compiler_params=_params("parallel", "arbitrary"),
        name="fox_project",
    )(x, g, shift, scale, gq, gk, bf, w)


def _page_dma(pools, bufs, sems, pps, page_of):
    assert pps % 2 == 0

    def each(b, s, slot, act):
        def body(i, c):
            for par in range(2):
                page = 2 * i + par
                pg = page_of(b, s, page)
                for n, (pool, buf) in enumerate(zip(pools, bufs)):
                    act(pltpu.make_async_copy(pool.at[pg], buf.at[slot, page], sems.at[slot, n]), (n + par) % 2)
            return c

        lax.fori_loop(0, pps // 2, body, 0)

    return (lambda b, s, slot: each(b, s, slot, lambda cp, prio: cp.start(priority=prio)),
            lambda b, s, slot: each(b, s, slot, lambda cp, prio: cp.wait()))


def _resident_slot(start, wait):
    b, s = pl.program_id(0), pl.program_id(1)
    n_steps = pl.num_programs(1)
    t = b * n_steps + s
    slot = t % 2

    @pl.when(t == 0)
    def _():
        start(b, s, slot)

    last = s + 1 == n_steps

    @pl.when(t + 1 < pl.num_programs(0) * n_steps)
    def _():
        start(jnp.where(last, b + 1, b), jnp.where(last, 0, s + 1), 1 - slot)

    wait(b, s, slot)
    return slot


def _mla_sample_kernel(pt_ref, q_ref, cn_ref, kn_ref, wfull_ref, wukt_ref, wuv_ref, ckv_hbm, kpe_hbm,
                       o_ref, ckv_buf, kpe_buf, sems, qabs_scr, qpe_scr, ckv_scr, m_scr, l_scr, acc_scr,
                       *, pps, page_base):
    s_idx = pl.program_id(1)
    nq = q_ref.shape[0]
    nrow = MLA_HEADS * nq
    start, wait = _page_dma((ckv_hbm, kpe_hbm), (ckv_buf, kpe_buf), sems, pps,
                            lambda b, s, i: page_base + pt_ref[b, s * pps + i])
    slot = _resident_slot(start, wait)

    @pl.when(s_idx == 0)
    def _():
        m_scr[...] = jnp.full_like(m_scr, NEG_BIG)
        l_scr[...] = jnp.zeros_like(l_scr)
        acc_scr[...] = jnp.zeros_like(acc_scr)
        for hh in range(MLA_HEADS):
            res = _dot(q_ref[:, hh * LANE:(hh + 1) * LANE], wfull_ref[hh])
            qabs_scr[hh * nq:(hh + 1) * nq, :] = res[:, :KV_LORA]
            qpe_scr[hh * nq:(hh + 1) * nq, :] = res[:, KV_LORA:]

    wstack = jnp.concatenate([wukt_ref[...], qabs_scr[...].astype(BF16)], axis=0)
    qpe = qpe_scr[...].astype(BF16)
    nk = MLA_HEADS * MLA_NOPE

    def scores(cb, kpe_t):
        a = _dot_nt(wstack, cb)
        spe = _dot(qpe[:, :MLA_ROPE], kpe_t)
        rows = []
        for hh in range(MLA_HEADS):
            kr = a[hh * MLA_NOPE:(hh + 1) * MLA_NOPE, :]
            ms = jnp.sum(kr * kr, axis=0, keepdims=True) * (1.0 / MLA_NOPE)
            rows.append(a[nk + hh * nq:nk + (hh + 1) * nq, :] * lax.rsqrt(ms + RMS_EPS))
        return jnp.concatenate(rows, axis=0) + spe

    def online(s, vals):
        m_prev = m_scr[...]
        m_new = jnp.maximum(m_prev, jnp.max(s, axis=-1, keepdims=True))
        alpha = jnp.exp2(m_prev - m_new)
        p = jnp.exp2(s - m_new)
        l_scr[...] = alpha * l_scr[...] + jnp.sum(p, axis=-1, keepdims=True)
        acc_scr[...] = alpha * acc_scr[...] + _dot(p.astype(BF16), vals)
        m_scr[...] = m_new

    for i in range(pps):
        ckv_scr[i * PAGE:(i + 1) * PAGE, :] = ckv_buf[slot, i].astype(BF16)
    cb = ckv_scr[...]
    kt = jnp.concatenate([kpe_buf[slot, i] for i in range(pps)], axis=1)
    online(scores(cb, kt.astype(BF16)), cb)

    @pl.when(s_idx == pl.num_programs(1) - 1)
    def _():
        pad = PAGE - nq
        cb = jnp.concatenate([cn_ref[...], jnp.zeros((pad, KV_LORA), F32)], axis=0).astype(BF16)
        kb = kn_ref[...].astype(BF16)
        s = scores(cb, kb)
        row = lax.broadcasted_iota(jnp.int32, (nrow, PAGE), 0)
        col = lax.broadcasted_iota(jnp.int32, (nrow, PAGE), 1)
        s = jnp.where(col <= row % nq, s, NEG_BIG)
        online(s, cb)
        o_lat = (acc_scr[...] / l_scr[...]).astype(BF16)
        out = _dot(o_lat[0:nq, :], wuv_ref[0])
        for hh in range(1, MLA_HEADS):
            out = out + _dot(o_lat[hh * nq:(hh + 1) * nq, :], wuv_ref[hh])
        o_ref[...] = out


def mla_attend_sample(qf, ckv_new, kpe_new, pool_ckv, pool_kpe, page_base, page_table, wfull, wukt, wuv, pps):
    db, nq, hw = qf.shape
    n_pages = page_table.shape[1]
    assert n_pages % pps == 0
    nrow = MLA_HEADS * nq

    def fixed(arr):
        nd = arr.ndim
        return pl.BlockSpec(arr.shape, lambda b, s, pt: (0,) * nd)

    def per_seq(arr):
        return pl.BlockSpec((None,) + arr.shape[1:], lambda b, s, pt: (b, 0, 0))

    hbm = pl.BlockSpec(memory_space=pl.ANY)
    in_specs = [per_seq(qf), per_seq(ckv_new), per_seq(kpe_new), fixed(wfull), fixed(wukt), fixed(wuv), hbm, hbm]
    out_w = MLA_HEADS * MLA_V
    grid_spec = pltpu.PrefetchScalarGridSpec(
        num_scalar_prefetch=1,
        grid=(db, n_pages // pps),
        in_specs=in_specs,
        out_specs=pl.BlockSpec((None, nq, out_w), lambda b, s, pt: (b, 0, 0)),
        scratch_shapes=[pltpu.VMEM((2, pps) + pool_ckv.shape[1:], F32), pltpu.VMEM((2, pps) + pool_kpe.shape[1:], F32),
                        pltpu.SemaphoreType.DMA((2, 2)),
                        pltpu.VMEM((nrow, KV_LORA), F32), pltpu.VMEM((nrow, LANE), F32),
                        pltpu.VMEM((pps * PAGE, KV_LORA), BF16),
                        pltpu.VMEM((nrow, 1), F32), pltpu.VMEM((nrow, 1), F32), pltpu.VMEM((nrow, KV_LORA), F32)],
    )
    return pl.pallas_call(
        functools.partial(_mla_sample_kernel, pps=pps, page_base=page_base),
        grid_spec=grid_spec,
        out_shape=jax.ShapeDtypeStruct((db, nq, out_w), F32),
        compiler_params=_params("arbitrary", "arbitrary"),
        name="mla_sample",
    )(page_table, qf, ckv_new, kpe_new, wfull, wukt, wuv, pool_ckv, pool_kpe)


def _suffix_flat(x):
    lane = lax.broadcasted_iota(jnp.int32, x.shape, 1)
    row = lax.broadcasted_iota(jnp.int32, x.shape, 0)
    y = x
    for s in (8, 16, 32, 64):
        y = y + jnp.where(lane + s < LANE, pltpu.roll(y, LANE - s, 1), 0.0)
    z = jnp.where(lane < FOX_HEADS, y, 0.0)
    for s in (8, 16, 32, 64):
        z = z + pltpu.roll(z, s, 1)
    v = z
    for s in (1, 2, 4):
        v = v + jnp.where(row + s < 8, pltpu.roll(v, 8 - s, 0), 0.0)
    return (y - x) + (v - z), v[0:1, :]


def _fox_sample_kernel(pt_ref, q_ref, kn_ref, vn_ref, fq_ref, fkn_ref, k_hbm, v_hbm, lf_hbm,
                       o_ref, k_buf, v_buf, lf_buf, sems, carry_scr, m_scr, l_scr, acc_scr, *, pps, page_base):
    s_idx = pl.program_id(1)
    nrow = q_ref.shape[0]
    prow = PAGE * FOX_HEADS
    n_pages = pt_ref.shape[1]
    start, wait = _page_dma((k_hbm, v_hbm, lf_hbm), (k_buf, v_buf, lf_buf), sems, pps,
                            lambda b, s, i: page_base + pt_ref[b, n_pages - 1 - (s * pps + i)])
    slot = _resident_slot(start, wait)

    @pl.when(s_idx == 0)
    def _():
        m_scr[...] = jnp.full_like(m_scr, NEG_BIG)
        l_scr[...] = jnp.zeros_like(l_scr)
        acc_scr[...] = jnp.zeros_like(acc_scr)
        carry_scr[...] = jnp.zeros_like(carry_scr)

    q = q_ref[...]
    row = lax.broadcasted_iota(jnp.int32, (nrow, LANE), 0)
    lane = lax.broadcasted_iota(jnp.int32, (nrow, LANE), 1)
    same_head = (row % FOX_HEADS) == (lane % FOX_HEADS)
    fqm = fq_ref[...] + jnp.where(same_head, 0.0, NEG_BIG)

    def online(state, s, vals):
        m_prev, l_prev, acc = state
        m_new = jnp.maximum(m_prev, jnp.max(s, axis=-1, keepdims=True))
        alpha = jnp.exp2(m_prev - m_new)
        p = jnp.exp2(s - m_new)
        l_new = alpha * l_prev + jnp.sum(p, axis=-1, keepdims=True)
        pb = p.astype(BF16)
        acc = alpha * acc
        o = 0
        for v in vals:
            acc = acc + _dot(pb[:, o:o + v.shape[0]], v)
            o += v.shape[0]
        return m_new, l_new, acc

    carry = carry_scr[...]
    state = (m_scr[...], l_scr[...], acc_scr[...])
    pending = None
    for g in range(pps // FOX_SUB):
        parts = []
        vals = []
        for i in range(g * FOX_SUB, (g + 1) * FOX_SUB):
            r, tot = _suffix_flat(lf_buf[slot, i])
            r = (r + carry) * LOG2E
            carry = carry + tot
            raw = _dot_nt(q, k_buf[slot, i].astype(BF16))
            vals.append(v_buf[slot, i].astype(BF16))
            for c in range(prow // LANE):
                parts.append(raw[:, c * LANE:(c + 1) * LANE] + (r[c:c + 1, :] + fqm))
        if pending is not None:
            state = online(state, *pending)
        pending = (jnp.concatenate(parts, axis=1), vals)
    state = online(state, *pending)
    carry_scr[...] = carry
    m_scr[...], l_scr[...], acc_scr[...] = state

    @pl.when(s_idx == pl.num_programs(1) - 1)
    def _():
        pad = LANE - kn_ref.shape[0]
        kb = jnp.concatenate([kn_ref[...], jnp.zeros((pad, FOX_HD), F32)], axis=0).astype(BF16)
        vb = jnp.concatenate([vn_ref[...], jnp.zeros((pad, FOX_HD), F32)], axis=0).astype(BF16)
        s = _dot_nt(q, kb) + fqm - fkn_ref[...]
        s = jnp.where(lane // FOX_HEADS <= row // FOX_HEADS, s, NEG_BIG)
        _, l_fin, acc_fin = online((m_scr[...], l_scr[...], acc_scr[...]), s, [vb])
        o_ref[...] = acc_fin / l_fin


def fox_attend_sample(qb, k_new, v_new, fq_col, fk_row, pool_k, pool_v, pool_lf, page_base, page_table, pps):
    db, nrow, hd = qb.shape
    n_pages = page_table.shape[1]
    assert n_pages % pps == 0

    def per_seq(arr):
        return pl.BlockSpec((None,) + arr.shape[1:], lambda b, s, pt: (b, 0, 0))

    hbm = pl.BlockSpec(memory_space=pl.ANY)
    in_specs = [per_seq(qb), per_seq(k_new), per_seq(v_new), per_seq(fq_col), per_seq(fk_row), hbm, hbm, hbm]
    grid_spec = pltpu.PrefetchScalarGridSpec(
        num_scalar_prefetch=1,
        grid=(db, n_pages // pps),
        in_specs=in_specs,
        out_specs=pl.BlockSpec((None, nrow, hd), lambda b, s, pt: (b, 0, 0)),
        scratch_shapes=[pltpu.VMEM((2, pps) + pool_k.shape[1:], F32), pltpu.VMEM((2, pps) + pool_v.shape[1:], F32),
                        pltpu.VMEM((2, pps) + pool_lf.shape[1:], F32), pltpu.SemaphoreType.DMA((2, 3)),
                        pltpu.VMEM((1, LANE), F32),
                        pltpu.VMEM((nrow, 1), F32), pltpu.VMEM((nrow, 1), F32), pltpu.VMEM((nrow, hd), F32)],
    )
    return pl.pallas_call(
        functools.partial(_fox_sample_kernel, pps=pps, page_base=page_base),
        grid_spec=grid_spec,
        out_shape=jax.ShapeDtypeStruct((db, nrow, hd), F32),
        compiler_params=_params("arbitrary", "arbitrary"),
        name="fox_sample",
    )(page_table, qb, k_new, v_new, fq_col, fk_row, pool_k, pool_v, pool_lf)


def _rope_perm():
    half = MLA_ROPE // 2
    idx = jnp.arange(MLA_ROPE)
    return jnp.where(idx < half, idx + half, idx - half), jnp.where(idx < half, -1.0, 1.0).astype(F32)


def _pad_rope_block(w):
    z = jnp.zeros(w.shape[:-1] + (MLA_NOPE,), w.dtype)
    z2 = jnp.zeros(w.shape[:-1] + (LANE - MLA_NOPE - MLA_ROPE,), w.dtype)
    return jnp.concatenate([z, w, z2], axis=-1)


def _mla_weights(a_w_in, a_w_uq, a_w_ukv):
    perm, _ = _rope_perm()
    o1 = Q_LORA
    o2 = o1 + KV_LORA
    o3 = o2 + MLA_ROPE
    w_kpe = a_w_in[:, o2:o3]
    win = jnp.concatenate([a_w_in[:, :o2], a_w_in[:, o3:], _pad_rope_block(w_kpe), _pad_rope_block(w_kpe[:, perm])],
                          axis=1).astype(BF16)
    wq3 = a_w_uq.reshape(Q_LORA, MLA_HEADS, MLA_NOPE + MLA_ROPE)
    zpad = jnp.zeros((Q_LORA, MLA_HEADS, LANE - MLA_NOPE - MLA_ROPE), F32)
    wq_a = jnp.concatenate([wq3, zpad], axis=-1).reshape(Q_LORA, MLA_HEADS * LANE)
    wq_b = _pad_rope_block(wq3[..., MLA_NOPE:][..., perm]).reshape(Q_LORA, MLA_HEADS * LANE)
    wq = jnp.concatenate([wq_a, wq_b], axis=1).astype(BF16)
    wkv3 = a_w_ukv.reshape(KV_LORA, MLA_HEADS, MLA_NOPE + MLA_V)
    wk = jnp.concatenate([wkv3[..., :MLA_NOPE], jnp.zeros((KV_LORA, MLA_HEADS, LANE - MLA_NOPE), F32)], axis=-1)
    wv = wkv3[..., MLA_NOPE:]
    wkv = jnp.concatenate([wk.reshape(KV_LORA, -1), wv.reshape(KV_LORA, -1)], axis=1).astype(BF16)
    return win, wq, wkv, wkv3


def _rope_tables(pos, qn_g, qr_g, kr_g):
    half = MLA_ROPE // 2
    perm, sign = _rope_perm()
    freq = ROPE_THETA ** (-jnp.arange(half, dtype=F32) / half)
    ang = pos.astype(F32)[:, None] * freq[None, :]
    cos = jnp.concatenate([jnp.cos(ang), jnp.cos(ang)], axis=1)
    sin = jnp.concatenate([jnp.sin(ang), jnp.sin(ang)], axis=1)
    n = pos.shape[0]
    zn = jnp.zeros((n, MLA_NOPE), F32)
    zp = jnp.zeros((n, LANE - MLA_NOPE - MLA_ROPE), F32)
    qs = MLA_SCALE * LOG2E
    gaq = jnp.concatenate([jnp.broadcast_to(qn_g[None, :], (n, MLA_NOPE)), qr_g[None, :] * cos, zp], axis=1) * qs
    gbq = jnp.concatenate([zn, (sign * qr_g[perm])[None, :] * sin, zp], axis=1) * qs
    gak = jnp.concatenate([zn, kr_g[None, :] * cos, zp], axis=1)
    gbk = jnp.concatenate([zn, (sign * kr_g[perm])[None, :] * sin, zp], axis=1)
    return gaq, gbq, gak, gbk


def _s5_params(a_re, a_im, log_dt, b_re, b_im, c_re, c_im, d_skip, w_glu, b_glu):
    g, n = a_re.shape
    dt = jnp.exp(log_dt)[:, None]
    mag = jnp.exp(dt * a_re)
    abar_re, abar_im = mag * jnp.cos(dt * a_im), mag * jnp.sin(dt * a_im)
    den = a_re * a_re + a_im * a_im
    w_re = ((abar_re - 1) * a_re + abar_im * a_im) / den
    w_im = (abar_im * a_re - (abar_re - 1) * a_im) / den
    bbar_re = w_re[..., None] * b_re - w_im[..., None] * b_im
    bbar_im = w_re[..., None] * b_im + w_im[..., None] * b_re
    gh = g // 2
    eye = jnp.eye(gh, dtype=F32)

    def pack_b(bb):
        return jnp.einsum('gnc,gh->gchn', bb, eye).reshape(gh * S5_GROUP, gh * n)

    def pack_c(cc):
        return jnp.einsum('gcn,gh->gnhc', cc, eye).reshape(gh * n, gh * S5_GROUP)

    bd = jnp.stack([jnp.concatenate([pack_b(bbar_re[h * gh:(h + 1) * gh]), pack_b(bbar_im[h * gh:(h + 1) * gh])], axis=1)
                    for h in range(2)])
    cd = jnp.stack([jnp.concatenate([pack_c(c_re[h * gh:(h + 1) * gh]), -pack_c(c_im[h * gh:(h + 1) * gh])], axis=0)
                    for h in range(2)])
    bd_hi = bd.astype(BF16)
    bd_lo = (bd - bd_hi.astype(F32)).astype(BF16)
    return (abar_re.reshape(1, g * n), abar_im.reshape(1, g * n), bd_hi, bd_lo, cd.astype(BF16),
            d_skip.reshape(1, -1), w_glu.astype(BF16), b_glu.reshape(1, -1))


def kernel(x_prompt, x_sample, c_prompt, c_sample, cache_mla_ckv, cache_mla_kpe, state_s5_re, state_s5_im,
           cache_fox_k, cache_fox_v, cache_fox_logf, page_table, w_ada, b_ada, norm_g, ffn_w1, ffn_w3, ffn_w2,
           a_w_in, a_q_norm, a_kv_norm, a_w_uq, a_w_ukv, a_qn_norm, a_qr_norm, a_kn_norm, a_kr_norm,
           s5_a_re, s5_a_im, s5_log_dt, s5_b_re, s5_b_im, s5_c_re, s5_c_im, s5_d, s5_w_glu, s5_b_glu,
           a_w_out, c_w_in, c_b_f, c_q_norm, c_k_norm, c_w_out):
    B, L, D = x_prompt.shape
    DB, DS, _ = x_sample.shape
    RS = DB * DS
    n_pages = page_table.shape[1]
    n_past = n_pages * PAGE
    depth = w_ada.shape[0]

    tm_p = min(512, L)
    tm_ffn = min(1024, L)
    tf = 256
    tq = min(1024, L)
    tk = min(512, L)

    m_all = ada_modulation(jnp.concatenate([c_prompt, c_sample], axis=0), w_ada, b_ada)
    w1b, w3b, w2b = ffn_w1.astype(BF16), ffn_w3.astype(BF16), ffn_w2.astype(BF16)

    xp = x_prompt
    xs = x_sample.reshape(1, RS, D)
    outs_p = {}
    outs_s = {}
    for i in range(depth):
        mp = m_all[i, :B].reshape(B, 3, 3, 1, D)
        ms = jnp.repeat(m_all[i, B:].reshape(DB, 3, 3, D), DS, axis=0).reshape(1, RS, 3, 3, D)

        def mod_p(s, k):
            return mp[:, s, k]

        def mod_s(s, k):
            return ms[:, :, s, k]

        g = norm_g[i]
        xp = ffn_sublayer(xp, g[0], mod_p(0, 0), mod_p(0, 1), mod_p(0, 2), w1b, w3b, w2b, i, 0, tm_ffn, tf)
        xs = ffn_sublayer(xs, g[0], mod_s(0, 0), mod_s(0, 1), mod_s(0, 2), w1b, w3b, w2b, i, 0, RS, tf)
        j = i // 2
        g1 = g[1].reshape(1, D)
        if i % 2 == 0:
            win, wq, wkv, wkv3 = _mla_weights(a_w_in[j], a_w_uq[j], a_w_ukv[j])
            gq = a_q_norm[j].reshape(1, -1)
            gkv = a_kv_norm[j].reshape(1, -1)
            gkn = jnp.concatenate([a_kn_norm[j], jnp.zeros((LANE - MLA_NOPE,), F32)]).reshape(1, LANE)
            s5p = _s5_params(s5_a_re[j], s5_a_im[j], s5_log_dt[j], s5_b_re[j], s5_b_im[j], s5_c_re[j], s5_c_im[j],
                             s5_d[j], s5_w_glu[j], s5_b_glu[j])
            wo = a_w_out[j].astype(BF16)
            hv = MLA_HEADS * MLA_V
            wo_att, wo_ssm = wo[:hv], wo[hv:]
            cw = s5_d.shape[-1]
            ns = s5_a_re.shape[1] * s5_a_re.shape[2]

            tabs = _rope_tables(jnp.arange(L), a_qn_norm[j], a_qr_norm[j], a_kr_norm[j])
            ckv, kpe, u_tm, qf, kf, vb = mla_s5_project(xp, g1, mod_p(1, 0), mod_p(1, 1), tabs, gq, gkv, gkn,
                                                        win, wq, wkv, tm_p)
            att = flash_prompt(qf, kf, vb, None, None, 2, MLA_V, tq, tk)
            zeros = jnp.zeros((B, ns), F32)
            ssm, sr, si = s5_mix(u_tm.reshape(L, B, cw), zeros, zeros, s5p, min(128, L), False)
            ssm2 = ssm.reshape(L, B * cw)
            mix_p = (mod_p(1, 2), [att, ssm2],
                     [lambda tm, w=hv: pl.BlockSpec((None, tm, w), lambda b, l: (b, l, 0)),
                      lambda tm, w=cw: pl.BlockSpec((tm, w), lambda b, l: (l, b))],
                     [wo_att, wo_ssm])
            outs_p.setdefault('ckv', []).append(ckv)
            outs_p.setdefault('kpe', []).append(kpe)
            outs_p.setdefault('s5r', []).append(sr.reshape(B, -1, S5_STATE))
            outs_p.setdefault('s5i', []).append(si.reshape(B, -1, S5_STATE))

            pos_s = n_past + jnp.tile(jnp.arange(DS), DB)
            tabs = _rope_tables(pos_s, a_qn_norm[j], a_qr_norm[j], a_kr_norm[j])
            ckv, kpe, u_s, qf, kf, vb = mla_s5_project(xs, g1, mod_s(1, 0), mod_s(1, 1), tabs, gq, gkv, gkn,
                                                       win, wq, wkv, RS)
            wuk = wkv3[..., :MLA_NOPE] * a_kn_norm[j][None, None, :]
            wabs = jnp.transpose(wuk, (1, 2, 0))
            top = jnp.concatenate([wabs, jnp.zeros((MLA_HEADS, MLA_NOPE, LANE), F32)], axis=-1)
            eye_blk = jnp.concatenate([jnp.zeros((MLA_ROPE, KV_LORA), F32), jnp.eye(MLA_ROPE, LANE, dtype=F32)], axis=-1)
            mid = jnp.broadcast_to(eye_blk[None], (MLA_HEADS, MLA_ROPE, KV_LORA + LANE))
            bot = jnp.zeros((MLA_HEADS, LANE - MLA_NOPE - MLA_ROPE, KV_LORA + LANE), F32)
            wfull = jnp.concatenate([top, mid, bot], axis=1).astype(BF16)
            wukt = jnp.transpose(wkv3[..., :MLA_NOPE], (1, 2, 0)).reshape(MLA_HEADS * MLA_NOPE, KV_LORA).astype(BF16)
            wuv = jnp.einsum('khd,hg->hkgd', wkv3[..., MLA_NOPE:], jnp.eye(MLA_HEADS, dtype=F32))
            wuv = wuv.reshape(MLA_HEADS, KV_LORA, hv).astype(BF16)
            kpe_t = jnp.transpose(kpe.reshape(DB, DS, MLA_ROPE), (0, 2, 1))
            kpe_t = jnp.concatenate([kpe_t, jnp.zeros((DB, MLA_ROPE, PAGE - DS), F32)], axis=-1)
            pool_ckv = cache_mla_ckv.reshape(-1, PAGE, KV_LORA)
            pool_kpe = jnp.swapaxes(cache_mla_kpe, 2, 3).reshape(-1, MLA_ROPE, PAGE)
            att_s = mla_attend_sample(qf.reshape(DB, DS, -1), ckv.reshape(DB, DS, -1), kpe_t,
                                      pool_ckv, pool_kpe, j * cache_mla_ckv.shape[1], page_table,
                                      wfull, wukt, wuv, min(32, n_pages))
            u3 = jnp.transpose(u_s.reshape(DB, DS, cw), (1, 0, 2))
            ssm, sr, si = s5_mix(u3, state_s5_re[j].reshape(DB, ns), state_s5_im[j].reshape(DB, ns), s5p, DS, True)
            ssm_s = jnp.transpose(ssm, (1, 0, 2)).reshape(1, RS, cw)
            mix_s = (mod_s(1, 2), [att_s.reshape(1, RS, hv), ssm_s],
                     [lambda tm, w=hv: pl.BlockSpec((None, tm, w), lambda b, l: (b, l, 0)),
                      lambda tm, w=cw: pl.BlockSpec((None, tm, w), lambda b, l: (b, l, 0))],
                     [wo_att, wo_ssm])
            outs_s.setdefault('ckv', []).append(ckv.reshape(DB, DS, -1))
            outs_s.setdefault('kpe', []).append(kpe.reshape(DB, DS, -1))
            outs_s.setdefault('s5r', []).append(sr.reshape(DB, -1, S5_STATE))
            outs_s.setdefault('s5i', []).append(si.reshape(DB, -1, S5_STATE))
        else:
            hw = FOX_HEADS * FOX_HD
            wf = jnp.concatenate([jnp.swapaxes(c_w_in[j], 0, 1), jnp.zeros((LANE - FOX_HEADS, D), F32)], axis=0)
            wf = wf.astype(BF16)
            gqf = (c_q_norm[j] * (FOX_SCALE * LOG2E)).reshape(1, FOX_HD)
            gkf = c_k_norm[j].reshape(1, FOX_HD)
            bf = jnp.concatenate([c_b_f[j], jnp.zeros((LANE - FOX_HEADS,), F32)]).reshape(1, LANE)
            wo = c_w_out[j].astype(BF16)

            k32, v32, lf, qb, kb, vb, fc = fox_project(xp, g1, mod_p(1, 0), mod_p(1, 1), gqf, gkf, bf, wf, L, tm_p)
            fcs = jnp.transpose(fc[:, :, :FOX_HEADS], (0, 2, 1)) * LOG2E
            fk = fcs.reshape(B, FOX_HEADS, L // tk, 1, tk)
            o = flash_prompt(qb, kb, vb, fc, fk, 2, FOX_HD, tq, tk)
            mix_p = (mod_p(1, 2), [o], [lambda tm, w=hw: pl.BlockSpec((None, tm, w), lambda b, l: (b, l, 0))], [wo])
            outs_p.setdefault('fk', []).append(k32.reshape(B, L, FOX_HEADS, FOX_HD))
            outs_p.setdefault('fv', []).append(v32.reshape(B, L, FOX_HEADS, FOX_HD))
            outs_p.setdefault('flf', []).append(lf)

            k32, v32, lf, qb, kb, vb, fc = fox_project(xs, g1, mod_s(1, 0), mod_s(1, 1), gqf, gkf, bf, wf, DS, RS)
            nr = DS * FOX_HEADS
            f_new = fc[0, :, :FOX_HEADS].reshape(DB, nr) * LOG2E
            fq_col = f_new[..., None]
            fk_row = jnp.concatenate([f_new, jnp.zeros((DB, LANE - nr), F32)], axis=-1)[:, None, :]
            n_pool = cache_fox_k.shape[1]
            pool_k = cache_fox_k.reshape(-1, PAGE * FOX_HEADS, FOX_HD)
            pool_v = cache_fox_v.reshape(-1, PAGE * FOX_HEADS, FOX_HD)
            pool_lf = cache_fox_logf.reshape(-1, PAGE * FOX_HEADS // LANE, LANE)
            o = fox_attend_sample(qb.reshape(DB, nr, FOX_HD), k32.reshape(DB, nr, FOX_HD), v32.reshape(DB, nr, FOX_HD),
                                  fq_col, fk_row, pool_k, pool_v, pool_lf, j * n_pool, page_table, 16)
            mix_s = (mod_s(1, 2), [o.reshape(1, RS, hw)],
                     [lambda tm, w=hw: pl.BlockSpec((None, tm, w), lambda b, l: (b, l, 0))], [wo])
            outs_s.setdefault('fk', []).append(k32.reshape(DB, DS, FOX_HEADS, FOX_HD))
            outs_s.setdefault('fv', []).append(v32.reshape(DB, DS, FOX_HEADS, FOX_HD))
            outs_s.setdefault('flf', []).append(lf.reshape(DB, DS, FOX_HEADS))
        xp = ffn_sublayer(xp, g[2], mod_p(2, 0), mod_p(2, 1), mod_p(2, 2), w1b, w3b, w2b, i, 1, tm_ffn, tf, mix_p)
        xs = ffn_sublayer(xs, g[2], mod_s(2, 0), mod_s(2, 1), mod_s(2, 2), w1b, w3b, w2b, i, 1, RS, tf, mix_s)

    def st(d, key):
        return jnp.stack(d[key])

    return (xp, xs.reshape(DB, DS, D),
            st(outs_p, 'ckv'), st(outs_p, 'kpe'), st(outs_p, 's5r'), st(outs_p, 's5i'),
            st(outs_p, 'fk'), st(outs_p, 'fv'), st(outs_p, 'flf'),
            st(outs_s, 'ckv'), st(outs_s, 'kpe'), st(outs_s, 's5r'), st(outs_s, 's5i'),
            st(outs_s, 'fk'), st(outs_s, 'fv'), st(outs_s, 'flf'))
```
